```python
import math
import jax, jax.numpy as jnp
from jax import lax
import numpy as np

D_MODEL = 2048
BATCH = 2
SEQ = 8192
DEPTH = 1
DEC_BATCH = 8
DEC_SEQ = 64
PAST_LEN = 2048

CHUNK = 64
D_MIX = D_MODEL
D_ATT = D_MIX // 2
HEAD_DIM = 128
N_HEADS = D_ATT // HEAD_DIM
D_SSM = D_MIX - D_ATT
SSM_GROUP = 16
N_GROUPS = D_SSM // SSM_GROUP
SSM_STATE = 64
D_FF = 4 * D_MODEL
Q_BLOCK = 128
D_IN = 3 * D_ATT + N_HEADS + D_SSM
N_MOD = 6
EPS = 1e-6

kernel_name = "hybrid_fox_s5_streaming_step"


def _rms(x, g):
    xf = x.astype(jnp.float32)
    y = xf * lax.rsqrt(jnp.mean(xf * xf, axis=-1, keepdims=True) + EPS)
    return (y * g.astype(jnp.float32)).astype(x.dtype)


def _fox_block(q, fq, qpos, k, v, fk, kpos):
    s = jnp.einsum('bqhd,bkhd->bhqk', q, k).astype(jnp.float32) * (HEAD_DIM ** -0.5)
    s = s + jnp.transpose(fq, (0, 2, 1))[..., None] - jnp.transpose(fk, (0, 2, 1))[:, :, None, :]
    mask = kpos[None, :] <= qpos[:, None]
    p = jax.nn.softmax(jnp.where(mask, s, -jnp.inf), axis=-1)
    return jnp.einsum('bhqk,bkhd->bqhd', p.astype(v.dtype), v)


def _fox_prompt(q, k, v, logf):
    bsz, t = q.shape[:2]
    f_cum = jnp.cumsum(logf, axis=1)
    pos = jnp.arange(t)
    nb = t // Q_BLOCK
    qb = q.reshape(bsz, nb, Q_BLOCK, N_HEADS, HEAD_DIM).transpose(1, 0, 2, 3, 4)
    fb = f_cum.reshape(bsz, nb, Q_BLOCK, N_HEADS).transpose(1, 0, 2, 3)
    pb = pos.reshape(nb, Q_BLOCK)
    out = lax.map(lambda a: _fox_block(a[0], a[1], a[2], k, v, f_cum, pos), (qb, fb, pb))
    return out.transpose(1, 0, 2, 3, 4).reshape(bsz, t, D_ATT)


def _fox_sample(q, k, v, logf, cache_k, cache_v, cache_logf):
    bsz, s_len = q.shape[:2]
    p_len = cache_k.shape[1]
    k_all = jnp.concatenate([cache_k.astype(k.dtype), k], axis=1)
    v_all = jnp.concatenate([cache_v.astype(v.dtype), v], axis=1)
    f_cum = jnp.cumsum(jnp.concatenate([cache_logf.astype(jnp.float32), logf], axis=1), axis=1)
    kpos = jnp.arange(p_len + s_len)
    qpos = p_len + jnp.arange(s_len)
    out = _fox_block(q, f_cum[:, p_len:], qpos, k_all, v_all, f_cum, kpos)
    return out.reshape(bsz, s_len, D_ATT)


def _s5(u, h0_re, h0_im, log_dt, a_re, a_im, b_re, b_im, c_re, c_im, d_skip):
    bsz, t = u.shape[:2]
    f32 = jnp.float32
    uf = u.astype(f32).reshape(bsz, t, N_GROUPS, SSM_GROUP)
    dt = jnp.exp(log_dt.astype(f32))[:, None]
    lam_re = jnp.minimum(a_re.astype(f32), -1e-4)
    lam_im = a_im.astype(f32)
    mag = jnp.exp(lam_re * dt)
    ang = lam_im * dt
    ab_re = mag * jnp.cos(ang)
    ab_im = mag * jnp.sin(ang)
    den = lam_re * lam_re + lam_im * lam_im
    z_re = ((ab_re - 1.0) * lam_re + ab_im * lam_im) / den
    z_im = (ab_im * lam_re - (ab_re - 1.0) * lam_im) / den
    bu_re = jnp.einsum('btgc,gnc->btgn', uf, b_re.astype(f32))
    bu_im = jnp.einsum('btgc,gnc->btgn', uf, b_im.astype(f32))
    in_re = z_re * bu_re - z_im * bu_im
    in_im = z_re * bu_im + z_im * bu_re
    hr = h0_re.astype(f32)
    hi = h0_im.astype(f32)
    in_re = in_re.at[:, 0].add(ab_re * hr - ab_im * hi)
    in_im = in_im.at[:, 0].add(ab_re * hi + ab_im * hr)
    shape_a = (t, 1, N_GROUPS, SSM_STATE)
    elems = (jnp.broadcast_to(ab_re, shape_a), jnp.broadcast_to(ab_im, shape_a),
             in_re.transpose(1, 0, 2, 3), in_im.transpose(1, 0, 2, 3))

    def combine(e1, e2):
        a1r, a1i, b1r, b1i = e1
        a2r, a2i, b2r, b2i = e2
        return (a2r * a1r - a2i * a1i, a2r * a1i + a2i * a1r,
                a2r * b1r - a2i * b1i + b2r, a2r * b1i + a2i * b1r + b2i)

    _, _, s_re, s_im = lax.associative_scan(combine, elems, axis=0)
    y = (jnp.einsum('tbgn,gcn->btgc', s_re, c_re.astype(f32))
         - jnp.einsum('tbgn,gcn->btgc', s_im, c_im.astype(f32))
         + d_skip.astype(f32).reshape(N_GROUPS, SSM_GROUP) * uf)
    return y.reshape(bsz, t, D_SSM).astype(u.dtype), s_re[-1], s_im[-1]


def _layer(x, c, p, cache):
    (w_ada, b_ada, g_norm1, w_in, g_q, g_k, b_f, log_dt, a_re, a_im, b_re, b_im, c_re, c_im,
     d_skip, w_glu, b_glu, g_att_out, g_ssm_out, w_out, g_norm2, w_ff1, w_ff2) = p
    bsz, t = x.shape[:2]
    mod = (jax.nn.silu(c) @ w_ada + b_ada)[:, None, :]
    sh1, sc1, gt1, sh2, sc2, gt2 = jnp.split(mod, N_MOD, axis=-1)
    h = _rms(x, g_norm1) * (1 + sc1) + sh1
    proj = h @ w_in
    q, k, v, f, u = jnp.split(proj, [D_ATT, 2 * D_ATT, 3 * D_ATT, 3 * D_ATT + N_HEADS], axis=-1)
    q = _rms(q.reshape(bsz, t, N_HEADS, HEAD_DIM), g_q)
    k = _rms(k.reshape(bsz, t, N_HEADS, HEAD_DIM), g_k)
    v = v.reshape(bsz, t, N_HEADS, HEAD_DIM)
    logf = jax.nn.log_sigmoid(f.astype(jnp.float32) + b_f.astype(jnp.float32))
    if cache is None:
        att = _fox_prompt(q, k, v, logf)
        h0_re = jnp.zeros((bsz, N_GROUPS, SSM_STATE), jnp.float32)
        h0_im = h0_re
    else:
        cache_k, cache_v, cache_logf, h0_re, h0_im = cache
        att = _fox_sample(q, k, v, logf, cache_k, cache_v, cache_logf)
    y_ssm, h_re, h_im = _s5(u, h0_re, h0_im, log_dt, a_re, a_im, b_re, b_im, c_re, c_im, d_skip)
    g = jax.nn.gelu(y_ssm)
    ssm = g * jax.nn.sigmoid(g @ w_glu + b_glu)
    mix = jnp.concatenate([_rms(att, g_att_out), _rms(ssm, g_ssm_out)], axis=-1) @ w_out
    x = x + gt1 * mix
    h2 = _rms(x, g_norm2) * (1 + sc2) + sh2
    x = x + gt2 * (jnp.square(jax.nn.relu(h2 @ w_ff1)) @ w_ff2)
    return x, (k, v, logf, h_re, h_im)


def setup_inputs(seed: int = 0) -> dict:
    key = jax.random.key(seed)
    ks = jax.random.split(key, 40)
    n = jax.random.normal
    L = DEPTH
    d = {}
    d["x_prompt"] = n(ks[0], (BATCH, SEQ, D_MODEL), jnp.float32)
    d["x_sample"] = n(ks[1], (DEC_BATCH, DEC_SEQ, D_MODEL), jnp.float32)
    d["c_prompt"] = n(ks[2], (BATCH, D_MODEL), jnp.float32)
    d["c_sample"] = n(ks[3], (DEC_BATCH, D_MODEL), jnp.float32)
    d["cache_k"] = n(ks[4], (L, DEC_BATCH, PAST_LEN, N_HEADS, HEAD_DIM), jnp.float32)
    d["cache_v"] = n(ks[5], (L, DEC_BATCH, PAST_LEN, N_HEADS, HEAD_DIM), jnp.float32)
    d["cache_logf"] = jax.nn.log_sigmoid(3.0 + n(ks[6], (L, DEC_BATCH, PAST_LEN, N_HEADS), jnp.float32))
    d["state_ssm_re"] = 0.5 * n(ks[7], (L, DEC_BATCH, N_GROUPS, SSM_STATE), jnp.float32)
    d["state_ssm_im"] = 0.5 * n(ks[8], (L, DEC_BATCH, N_GROUPS, SSM_STATE), jnp.float32)
    d["w_ada"] = 0.5 * D_MODEL ** -0.5 * n(ks[9], (L, D_MODEL, N_MOD * D_MODEL), jnp.float32)
    d["b_ada"] = 0.02 * n(ks[10], (L, N_MOD * D_MODEL), jnp.float32)
    d["g_norm1"] = 1.0 + 0.02 * n(ks[11], (L, D_MODEL), jnp.float32)
    d["w_in"] = D_MODEL ** -0.5 * n(ks[12], (L, D_MODEL, D_IN), jnp.float32)
    d["g_q"] = 1.0 + 0.02 * n(ks[13], (L, HEAD_DIM), jnp.float32)
    d["g_k"] = 1.0 + 0.02 * n(ks[14], (L, HEAD_DIM), jnp.float32)
    d["b_f"] = jax.random.uniform(ks[15], (L, N_HEADS), jnp.float32, 1.0, 6.0)
    d["log_dt"] = jax.random.uniform(ks[16], (L, N_GROUPS), jnp.float32, math.log(1e-3), math.log(1e-1))
    d["a_re"] = -0.5 + 0.01 * n(ks[17], (L, N_GROUPS, SSM_STATE), jnp.float32)
    d["a_im"] = (math.pi * jnp.arange(SSM_STATE, dtype=jnp.float32))[None, None, :] + 0.01 * n(ks[18], (L, N_GROUPS, SSM_STATE), jnp.float32)
    d["b_re"] = (2 * SSM_GROUP) ** -0.5 * n(ks[19], (L, N_GROUPS, SSM_STATE, SSM_GROUP), jnp.float32)
    d["b_im"] = (2 * SSM_GROUP) ** -0.5 * n(ks[20], (L, N_GROUPS, SSM_STATE, SSM_GROUP), jnp.float32)
    d["c_re"] = SSM_STATE ** -0.5 * n(ks[21], (L, N_GROUPS, SSM_GROUP, SSM_STATE), jnp.float32)
    d["c_im"] = SSM_STATE ** -0.5 * n(ks[22], (L, N_GROUPS, SSM_GROUP, SSM_STATE), jnp.float32)
    d["d_skip"] = n(ks[23], (L, D_SSM), jnp.float32)
    d["w_glu"] = D_SSM ** -0.5 * n(ks[24], (L, D_SSM, D_SSM), jnp.float32)
    d["b_glu"] = 0.02 * n(ks[25], (L, D_SSM), jnp.float32)
    d["g_att_out"] = 1.0 + 0.02 * n(ks[26], (L, D_ATT), jnp.float32)
    d["g_ssm_out"] = 1.0 + 0.02 * n(ks[27], (L, D_SSM), jnp.float32)
    d["w_out"] = D_MIX ** -0.5 * n(ks[28], (L, D_MIX, D_MODEL), jnp.float32)
    d["g_norm2"] = 1.0 + 0.02 * n(ks[29], (L, D_MODEL), jnp.float32)
    d["w_ff1"] = D_MODEL ** -0.5 * n(ks[30], (L, D_MODEL, D_FF), jnp.float32)
    d["w_ff2"] = D_FF ** -0.5 * n(ks[31], (L, D_FF, D_MODEL), jnp.float32)
    return d


def reference(x_prompt, x_sample, c_prompt, c_sample, cache_k, cache_v, cache_logf,
              state_ssm_re, state_ssm_im, w_ada, b_ada, g_norm1, w_in, g_q, g_k, b_f,
              log_dt, a_re, a_im, b_re, b_im, c_re, c_im, d_skip, w_glu, b_glu,
              g_att_out, g_ssm_out, w_out, g_norm2, w_ff1, w_ff2):
    xp, xs = x_prompt, x_sample
    kp, vp, fp, rp, ip = [], [], [], [], []
    ksl, vsl, fsl, rsl, isl = [], [], [], [], []
    for l in range(DEPTH):
        p = (w_ada[l], b_ada[l], g_norm1[l], w_in[l], g_q[l], g_k[l], b_f[l], log_dt[l],
             a_re[l], a_im[l], b_re[l], b_im[l], c_re[l], c_im[l], d_skip[l], w_glu[l],
             b_glu[l], g_att_out[l], g_ssm_out[l], w_out[l], g_norm2[l], w_ff1[l], w_ff2[l])
        xp, (k1, v1, f1, r1, i1) = _layer(xp, c_prompt, p, None)
        xs, (k2, v2, f2, r2, i2) = _layer(
            xs, c_sample, p,
            (cache_k[l], cache_v[l], cache_logf[l], state_ssm_re[l], state_ssm_im[l]))
        kp.append(k1); vp.append(v1); fp.append(f1); rp.append(r1); ip.append(i1)
        ksl.append(k2); vsl.append(v2); fsl.append(f2); rsl.append(r2); isl.append(i2)
    return (xp, xs,
            jnp.stack(kp), jnp.stack(vp), jnp.stack(fp), jnp.stack(rp), jnp.stack(ip),
            jnp.stack(ksl), jnp.stack(vsl), jnp.stack(fsl), jnp.stack(rsl), jnp.stack(isl))
```

```python
import functools
import math

import jax
import jax.numpy as jnp
from jax import lax
from jax.experimental import pallas as pl
from jax.experimental.pallas import tpu as pltpu

F32 = jnp.float32
BF16 = jnp.bfloat16
HIGHEST = lax.Precision.HIGHEST

HEAD_DIM = 128
SSM_GROUP = 16
SSM_STATE = 64
N_MOD = 6
EPS = 1e-6
LOG2E = 1.4426950408889634

LANES = 128
SUBLANES = 8
GROUPS_PER_BLOCK = 16
VMEM_LIMIT = 56 * 1024 * 1024


def _params(*sem):
    return pltpu.CompilerParams(dimension_semantics=sem, vmem_limit_bytes=VMEM_LIMIT)


def _resident(shape):
    nd = len(shape)
    return pl.BlockSpec(shape, lambda *_: (0,) * nd, pipeline_mode=pl.Buffered(1))


def _sigmoid(x):
    return 1.0 / (1.0 + jnp.exp(-x))


def _rms_rows(x, g):
    return x * lax.rsqrt(jnp.mean(x * x, axis=-1, keepdims=True) + EPS) * g


def _dot_nt(a, b):
    return lax.dot_general(a, b, (((1,), (1,)), ((), ())), preferred_element_type=F32)


def _mod_kernel(c_ref, w_ref, b_ref, o_ref):
    c = c_ref[...]
    s = c * _sigmoid(c)
    o_ref[...] = jnp.dot(s, w_ref[...], precision=HIGHEST, preferred_element_type=F32) + b_ref[...]


def _modulation(c_rows, w_ada, b_ada):
    rows, d = c_rows.shape
    n = w_ada.shape[1]
    tn = 512
    return pl.pallas_call(
        _mod_kernel,
        out_shape=jax.ShapeDtypeStruct((rows, n), F32),
        grid=(n // tn,),
        in_specs=[pl.BlockSpec((rows, d), lambda j: (0, 0)),
                  pl.BlockSpec((d, tn), lambda j: (0, j)),
                  pl.BlockSpec((1, tn), lambda j: (0, j))],
        out_specs=pl.BlockSpec((rows, tn), lambda j: (0, j)),
        compiler_params=_params("arbitrary"),
        name="modulation",
    )(c_rows, w_ada, b_ada)


def _s5_param_kernel(logdt_ref, are_ref, aim_ref, bre_ref, bim_ref,
                     zbre_ref, zbim_ref, pwre_ref, pwim_ref):
    dt = jnp.exp(logdt_ref[...])
    lam_re = jnp.minimum(are_ref[...], -1e-4)
    lam_im = aim_ref[...]
    mag = jnp.exp(lam_re * dt)
    ang = lam_im * dt
    ab_re = mag * jnp.cos(ang)
    ab_im = mag * jnp.sin(ang)
    den = lam_re * lam_re + lam_im * lam_im
    z_re = ((ab_re - 1.0) * lam_re + ab_im * lam_im) / den
    z_im = (ab_im * lam_re - (ab_re - 1.0) * lam_im) / den
    b_re = bre_ref[...]
    b_im = bim_ref[...]
    zbre_ref[...] = z_re * b_re - z_im * b_im
    zbim_ref[...] = z_re * b_im + z_im * b_re
    p_re, p_im = ab_re, ab_im
    for k in range(SUBLANES):
        pwre_ref[k] = p_re
        pwim_ref[k] = p_im
        p_re, p_im = p_re * ab_re - p_im * ab_im, p_re * ab_im + p_im * ab_re


def _s5_params(log_dt, a_re, a_im, b_re, b_im):
    g, n, c = b_re.shape
    f = lambda shape: jax.ShapeDtypeStruct(shape, F32)
    return pl.pallas_call(
        _s5_param_kernel,
        out_shape=(f((g, c, n)), f((g, c, n)), f((SUBLANES, g, 1, n)), f((SUBLANES, g, 1, n))),
        name="s5_params",
    )(log_dt.reshape(g, 1, 1), a_re.reshape(g, 1, n), a_im.reshape(g, 1, n),
      jnp.swapaxes(b_re, 1, 2), jnp.swapaxes(b_im, 1, 2))


def _block_diag(w):
    nb, g, r, c = w.shape
    eye = jnp.eye(g, dtype=w.dtype)
    return (w[:, :, :, None, :] * eye[None, :, None, :, None]).reshape(nb, g * r, g * c)


def _in_proj_kernel(x_ref, sh_ref, sc_ref, g1_ref, wqkv_ref, wu_ref, wf_ref, gq_ref, gk_ref, bf_ref,
                    q_ref, k32_ref, kbf_ref, v32_ref, vbf_ref, logf_ref, u_ref,
                    *, n_heads, q_scale, precise_u):
    nb, t, d = x_ref.shape
    rows = nb * t
    d_att = n_heads * HEAD_DIM
    x = x_ref[...]
    y = _rms_rows(x, g1_ref[...])
    h = (y * (1.0 + sc_ref[:, 0]) + sh_ref[:, 0]).reshape(rows, d)
    hb = h.astype(BF16)

    q = jnp.dot(hb, wqkv_ref[:, 0:d_att], preferred_element_type=F32)
    for hh in range(n_heads):
        sl = slice(hh * HEAD_DIM, (hh + 1) * HEAD_DIM)
        qn = _rms_rows(q[:, sl], gq_ref[...]) * q_scale
        q_ref[:, :, sl] = qn.reshape(nb, t, HEAD_DIM).astype(BF16)

    k = jnp.dot(hb, wqkv_ref[:, d_att:2 * d_att], preferred_element_type=F32)
    for hh in range(n_heads):
        sl = slice(hh * HEAD_DIM, (hh + 1) * HEAD_DIM)
        kn = _rms_rows(k[:, sl], gk_ref[...]).reshape(nb, t, HEAD_DIM)
        k32_ref[:, :, sl] = kn
        kbf_ref[:, :, sl] = kn.astype(BF16)

    v = jnp.dot(hb, wqkv_ref[:, 2 * d_att:3 * d_att], preferred_element_type=F32).reshape(nb, t, d_att)
    v32_ref[...] = v
    vbf_ref[...] = v.astype(BF16)

    f = jnp.dot(hb, wf_ref[...], preferred_element_type=F32) + bf_ref[...]
    logf = jnp.minimum(f, 0.0) - jnp.log1p(jnp.exp(-jnp.abs(f)))
    logf_ref[...] = logf[:, 0:n_heads].reshape(nb, t, n_heads)

    if precise_u:
        u = jnp.dot(h, wu_ref[...], precision=HIGHEST, preferred_element_type=F32)
    else:
        u = jnp.dot(hb, wu_ref[...], preferred_element_type=F32)
    u_ref[...] = u.reshape(nb, t, u_ref.shape[-1])


def _in_proj(x, mod4, mod_row, g1, w_qkv, w_u, w_f, g_q, g_k, b_f_pad, *, nb, t, n_heads, precise_u):
    n_seq, seq, d = x.shape
    d_att = n_heads * HEAD_DIM
    d_ssm = w_u.shape[1]
    grid = (n_seq // nb, seq // t)
    row_blk = mod_row // nb
    act = lambda width: pl.BlockSpec((nb, t, width), lambda i, j: (i, j, 0))
    mod = lambda m: pl.BlockSpec((nb, 1, 1, d), lambda i, j, m=m: (row_blk + i, m, 0, 0))
    sds = lambda width, dt: jax.ShapeDtypeStruct((n_seq, seq, width), dt)
    kern = functools.partial(_in_proj_kernel, n_heads=n_heads,
                             q_scale=HEAD_DIM ** -0.5 * LOG2E, precise_u=precise_u)
    return pl.pallas_call(
        kern,
        out_shape=(sds(d_att, BF16), sds(d_att, F32), sds(d_att, BF16), sds(d_att, F32), sds(d_att, BF16),
                   sds(n_heads, F32), sds(d_ssm, F32)),
        grid=grid,
        in_specs=[act(d), mod(0), mod(1), _resident(g1.shape), _resident(w_qkv.shape), _resident(w_u.shape),
                  _resident(w_f.shape), _resident(g_q.shape), _resident(g_k.shape), _resident(b_f_pad.shape)],
        out_specs=(act(d_att), act(d_att), act(d_att), act(d_att), act(d_att), act(n_heads), act(d_ssm)),
        compiler_params=_params("parallel", "parallel"),
        name="in_proj",
    )(x, mod4, mod4, g1, w_qkv, w_u, w_f, g_q, g_k, b_f_pad)


def _cumsum_kernel(x_ref, o_ref, *, n_chunks):
    x = x_ref[...]
    n = x.shape[0]
    li = lax.broadcasted_iota(jnp.int32, (LANES, LANES), 0)
    lj = lax.broadcasted_iota(jnp.int32, (LANES, LANES), 1)
    tri = (li <= lj).astype(F32)
    within = jnp.dot(x, tri, precision=HIGHEST, preferred_element_type=F32)
    tot = jnp.dot(x, jnp.ones((LANES, LANES), F32), precision=HIGHEST, preferred_element_type=F32)
    r = lax.broadcasted_iota(jnp.int32, (n, n), 0)
    c = lax.broadcasted_iota(jnp.int32, (n, n), 1)
    seq_of = lambda i: jnp.floor((i.astype(F32) + 0.5) * (1.0 / n_chunks))
    earlier = jnp.where(seq_of(r) == seq_of(c), jnp.where(c < r, 1.0, 0.0), 0.0)
    before = jnp.dot(earlier, tot, precision=HIGHEST, preferred_element_type=F32)
    o_ref[...] = (within + before) * LOG2E


def _forget_cumsum(logf_rows):
    r, length = logf_rows.shape
    n_chunks = length // LANES
    out = pl.pallas_call(
        functools.partial(_cumsum_kernel, n_chunks=n_chunks),
        out_shape=jax.ShapeDtypeStruct((r * n_chunks, LANES), F32),
        compiler_params=_params(),
        name="forget_cumsum",
    )(logf_rows.reshape(r * n_chunks, LANES))
    return out.reshape(r, length)


def _attn_kernel(q_ref, k_ref, v_ref, fq_ref, fk_ref, o_ref, m_ref, l_ref, acc_ref, *, tq):
    qi = pl.program_id(2)
    q = q_ref[0]
    fq = fq_ref[0, 0]
    m_ref[...] = jnp.full(m_ref.shape, -jnp.inf, F32)
    l_ref[...] = jnp.zeros(l_ref.shape, F32)
    acc_ref[...] = jnp.zeros(acc_ref.shape, F32)

    def chunk(kj, masked):
        off = pl.multiple_of(kj * tq, tq)
        k = k_ref[0, pl.ds(off, tq), :]
        v = v_ref[0, pl.ds(off, tq), :]
        s = _dot_nt(q, k) + fq - fk_ref[0, 0, kj]
        if masked:
            row = lax.broadcasted_iota(jnp.int32, s.shape, 0)
            col = lax.broadcasted_iota(jnp.int32, s.shape, 1)
            s = jnp.where(col <= row, s, -jnp.inf)
        m_old = m_ref[...]
        m_new = jnp.maximum(m_old, jnp.max(s, axis=-1, keepdims=True))
        alpha = jnp.exp2(m_old - m_new)
        p = jnp.exp2(s - m_new)
        l_ref[...] = alpha * l_ref[...] + jnp.sum(p, axis=-1, keepdims=True)
        acc_ref[...] = alpha * acc_ref[...] + jnp.dot(p.astype(BF16), v, preferred_element_type=F32)
        m_ref[...] = m_new

    def body(kj, carry):
        chunk(kj, False)
        return carry

    lax.fori_loop(0, qi, body, 0)
    chunk(qi, True)
    o_ref[0] = (acc_ref[...] * (1.0 / l_ref[...])).astype(o_ref.dtype)


def _prompt_attention(q, k, v, f_cum2, *, n_heads, tq):
    b, t, _ = q.shape
    nq = t // tq
    fq = f_cum2.reshape(b, n_heads, t, 1)
    fk = f_cum2.reshape(b, n_heads, nq, 1, tq)
    return pl.pallas_call(
        functools.partial(_attn_kernel, tq=tq),
        out_shape=jax.ShapeDtypeStruct(q.shape, BF16),
        grid=(b, n_heads, nq),
        in_specs=[pl.BlockSpec((1, tq, HEAD_DIM), lambda bi, h, i: (bi, i, h)),
                  pl.BlockSpec((1, t, HEAD_DIM), lambda bi, h, i: (bi, 0, h)),
                  pl.BlockSpec((1, t, HEAD_DIM), lambda bi, h, i: (bi, 0, h)),
                  pl.BlockSpec((1, 1, tq, 1), lambda bi, h, i: (bi, h, i, 0)),
                  pl.BlockSpec((1, 1, nq, 1, tq), lambda bi, h, i: (bi, h, 0, 0, 0))],
        out_specs=pl.BlockSpec((1, tq, HEAD_DIM), lambda bi, h, i: (bi, i, h)),
        scratch_shapes=[pltpu.VMEM((tq, 1), F32), pltpu.VMEM((tq, 1), F32), pltpu.VMEM((tq, HEAD_DIM), F32)],
        compiler_params=_params("parallel", "parallel", "arbitrary"),
        name="prompt_attention",
    )(q, k, v, fq, fk)


def _sample_attn_kernel(q_ref, kc_ref, vc_ref, kn_ref, vn_ref, fq_ref, fk_ref, o_ref, *, past):
    q = q_ref[0]
    s_len = q.shape[0]
    fq = fq_ref[0, 0]
    fk = fk_ref[0, 0]
    s_c = _dot_nt(q, kc_ref[0].astype(BF16)) + fq - fk[:, 0:past]
    s_n = _dot_nt(q, kn_ref[0]) + fq - fk[:, past:past + s_len]
    row = lax.broadcasted_iota(jnp.int32, s_n.shape, 0)
    col = lax.broadcasted_iota(jnp.int32, s_n.shape, 1)
    s_n = jnp.where(col <= row, s_n, -jnp.inf)
    m = jnp.maximum(jnp.max(s_c, axis=-1, keepdims=True), jnp.max(s_n, axis=-1, keepdims=True))
    p_c = jnp.exp2(s_c - m)
    p_n = jnp.exp2(s_n - m)
    l = jnp.sum(p_c, axis=-1, keepdims=True) + jnp.sum(p_n, axis=-1, keepdims=True)
    o = (jnp.dot(p_c.astype(BF16), vc_ref[0].astype(BF16), preferred_element_type=F32)
         + jnp.dot(p_n.astype(BF16), vn_ref[0], preferred_element_type=F32))
    o_ref[0] = (o * (1.0 / l)).astype(o_ref.dtype)


def _sample_attention(q, k_new, v_new, cache_k, cache_v, f_cum2, *, n_heads):
    b, s_len, _ = q.shape
    past = cache_k.shape[1]
    lpad = f_cum2.shape[-1]
    fq = f_cum2.reshape(b, n_heads, lpad, 1)
    fk = f_cum2.reshape(b, n_heads, 1, lpad)
    new = pl.BlockSpec((1, s_len, HEAD_DIM), lambda bi, h: (bi, 0, h))
    old = pl.BlockSpec((1, past, HEAD_DIM), lambda bi, h: (bi, 0, h))
    return pl.pallas_call(
        functools.partial(_sample_attn_kernel, past=past),
        out_shape=jax.ShapeDtypeStruct(q.shape, BF16),
        grid=(b, n_heads),
        in_specs=[new, old, old, new, new,
                  pl.BlockSpec((1, 1, s_len, 1), lambda bi, h: (bi, h, past // s_len, 0)),
                  pl.BlockSpec((1, 1, 1, lpad), lambda bi, h: (bi, h, 0, 0))],
        out_specs=new,
        compiler_params=_params("parallel", "parallel"),
        name="sample_attention",
    )(q, cache_k, cache_v, k_new, v_new, fq, fk)


def _s5_kernel(u_ref, h0re_ref, h0im_ref, wb_ref, wc_ref, pwre_ref, pwim_ref, d_ref, wglu_ref, bglu_ref,
               gout_ref, o_ref, hre_ref, him_ref, xre_ref, xim_ref, y_ref, *, precise):
    tc = u_ref.shape[1]
    n_blk = wb_ref.shape[0]
    ch = wb_ref.shape[1]
    ns = wb_ref.shape[2] // 2
    groups = tc // SUBLANES

    @pl.when(pl.program_id(1) == 0)
    def _():
        hre_ref[...] = h0re_ref[...]
        him_ref[...] = h0im_ref[...]

    sub = lax.broadcasted_iota(jnp.int32, (SUBLANES, ns), 0)
    for blk in range(n_blk):
        cols = slice(blk * ns, (blk + 1) * ns)
        ub = u_ref[0, :, blk * ch:(blk + 1) * ch]
        if precise:
            bu = jnp.dot(ub, wb_ref[blk], precision=HIGHEST, preferred_element_type=F32)
        else:
            bu = jnp.dot(ub.astype(BF16), wb_ref[blk], preferred_element_type=F32)
        x_re = bu[:, 0:ns].reshape(groups, SUBLANES, ns)
        x_im = bu[:, ns:2 * ns].reshape(groups, SUBLANES, ns)
        for shift in (1, 2, 4):
            a_re = jnp.where(sub >= shift, pwre_ref[shift - 1:shift, cols], 0.0)
            a_im = jnp.where(sub >= shift, pwim_ref[shift - 1:shift, cols], 0.0)
            s_re = pltpu.roll(x_re.reshape(tc, ns), shift, 0).reshape(groups, SUBLANES, ns)
            s_im = pltpu.roll(x_im.reshape(tc, ns), shift, 0).reshape(groups, SUBLANES, ns)
            x_re, x_im = (x_re + (a_re * s_re - a_im * s_im), x_im + (a_re * s_im + a_im * s_re))
        xre_ref[...] = x_re.reshape(tc, ns)
        xim_ref[...] = x_im.reshape(tc, ns)

        c_re = pwre_ref[:, cols]
        c_im = pwim_ref[:, cols]

        def carry_group(g, carry):
            h_re, h_im = carry
            r0 = pl.multiple_of(g * SUBLANES, SUBLANES)
            n_re = xre_ref[pl.ds(r0, SUBLANES), :] + (c_re * h_re - c_im * h_im)
            n_im = xim_ref[pl.ds(r0, SUBLANES), :] + (c_re * h_im + c_im * h_re)
            xre_ref[pl.ds(r0, SUBLANES), :] = n_re
            xim_ref[pl.ds(r0, SUBLANES), :] = n_im
            return n_re[SUBLANES - 1:SUBLANES, :], n_im[SUBLANES - 1:SUBLANES, :]

        h_re, h_im = lax.fori_loop(0, groups, carry_group, (hre_ref[0, :, cols], him_ref[0, :, cols]))
        hre_ref[0, :, cols] = h_re
        him_ref[0, :, cols] = h_im

        y = (jnp.dot(xre_ref[...].astype(BF16), wc_ref[blk, 0:ns, :], preferred_element_type=F32)
             + jnp.dot(xim_ref[...].astype(BF16), wc_ref[blk, ns:2 * ns, :], preferred_element_type=F32))
        y_ref[:, blk * ch:(blk + 1) * ch] = y + d_ref[:, blk * ch:(blk + 1) * ch] * ub

    y = y_ref[...]
    g = y * (0.5 * (1.0 + jnp.tanh(math.sqrt(2.0 / math.pi) * (y + 0.044715 * (y * y * y)))))
    gate = _sigmoid(jnp.dot(g.astype(BF16), wglu_ref[...], preferred_element_type=F32) + bglu_ref[...])
    o_ref[0] = _rms_rows(g * gate, gout_ref[...]).astype(o_ref.dtype)


def _s5_mixer(u, h0_re, h0_im, wb, wc, pw_re, pw_im, d_skip, w_glu, b_glu, g_out, *, tc, precise):
    b, t, d_ssm = u.shape
    n_state = h0_re.shape[-1]
    state = pl.BlockSpec((1, 1, n_state), lambda bi, j: (bi, 0, 0))
    act = pl.BlockSpec((1, tc, d_ssm), lambda bi, j: (bi, j, 0))
    ns = wb.shape[2] // 2
    return pl.pallas_call(
        functools.partial(_s5_kernel, precise=precise),
        out_shape=(jax.ShapeDtypeStruct((b, t, d_ssm), BF16),
                   jax.ShapeDtypeStruct((b, 1, n_state), F32), jax.ShapeDtypeStruct((b, 1, n_state), F32)),
        grid=(b, t // tc),
        in_specs=[act, state, state, _resident(wb.shape), _resident(wc.shape), _resident(pw_re.shape),
                  _resident(pw_im.shape), _resident(d_skip.shape), _resident(w_glu.shape),
                  _resident(b_glu.shape), _resident(g_out.shape)],
        out_specs=(act, state, state),
        scratch_shapes=[pltpu.VMEM((tc, ns), F32), pltpu.VMEM((tc, ns), F32), pltpu.VMEM((tc, d_ssm), F32)],
        compiler_params=_params("parallel", "arbitrary"),
        name="s5_mixer",
    )(u, h0_re, h0_im, wb, wc, pw_re, pw_im, d_skip, w_glu, b_glu, g_out)


def _out_proj_kernel(att_ref, ssm_ref, x_ref, gt1_ref, sh2_ref, sc2_ref, gatt_ref, wout_ref, g2_ref,
                     x1_ref, h2_ref):
    nb, t, d = x_ref.shape
    rows = nb * t
    d_att = att_ref.shape[-1]
    a = att_ref[...].astype(F32).reshape(rows, d_att)
    an = _rms_rows(a, gatt_ref[...]).astype(BF16)
    mix = (jnp.dot(an, wout_ref[0:d_att, :], preferred_element_type=F32)
           + jnp.dot(ssm_ref[...].reshape(rows, ssm_ref.shape[-1]), wout_ref[d_att:, :],
                     preferred_element_type=F32))
    x1 = x_ref[...] + gt1_ref[:, 0] * mix.reshape(nb, t, d)
    x1_ref[...] = x1
    h2 = _rms_rows(x1, g2_ref[...]) * (1.0 + sc2_ref[:, 0]) + sh2_ref[:, 0]
    h2_ref[...] = h2.astype(BF16)


def _out_proj(att, ssm, x, mod4, mod_row, g_att, w_out, g2, *, nb, t):
    n_seq, seq, d = x.shape
    row_blk = mod_row // nb
    act = lambda width: pl.BlockSpec((nb, t, width), lambda i, j: (i, j, 0))
    mod = lambda m: pl.BlockSpec((nb, 1, 1, d), lambda i, j, m=m: (row_blk + i, m, 0, 0))
    return pl.pallas_call(
        _out_proj_kernel,
        out_shape=(jax.ShapeDtypeStruct(x.shape, F32), jax.ShapeDtypeStruct(x.shape, BF16)),
        grid=(n_seq // nb, seq // t),
        in_specs=[act(att.shape[-1]), act(ssm.shape[-1]), act(d), mod(2), mod(3), mod(4),
                  _resident(g_att.shape), _resident(w_out.shape), _resident(g2.shape)],
        out_specs=(act(d), act(d)),
        compiler_params=_params("parallel", "parallel"),
        name="out_proj",
    )(att, ssm, x, mod4, mod4, mod4, g_att, w_out, g2)


def _mlp_kernel(h2_ref, x1_ref, gt2_ref, w1_ref, w2_ref, o_ref, acc_ref):
    nb, t, d = x1_ref.shape
    j = pl.program_id(2)

    @pl.when(j == 0)
    def _():
        acc_ref[...] = jnp.zeros(acc_ref.shape, F32)

    a = jnp.dot(h2_ref[...].reshape(nb * t, d), w1_ref[...], preferred_element_type=F32)
    r = jnp.maximum(a, 0.0)
    acc_ref[...] += jnp.dot((r * r).astype(BF16), w2_ref[...], preferred_element_type=F32)

    @pl.when(j == pl.num_programs(2) - 1)
    def _():
        o_ref[...] = x1_ref[...] + gt2_ref[:, 0] * acc_ref[...].reshape(nb, t, d)


def _mlp(h2, x1, mod4, mod_row, w1, w2, *, nb, t, tf):
    n_seq, seq, d = x1.shape
    d_ff = w1.shape[1]
    row_blk = mod_row // nb
    act = pl.BlockSpec((nb, t, d), lambda i, j, f: (i, j, 0))
    return pl.pallas_call(
        _mlp_kernel,
        out_shape=jax.ShapeDtypeStruct(x1.shape, F32),
        grid=(n_seq // nb, seq // t, d_ff // tf),
        in_specs=[act, act,
                  pl.BlockSpec((nb, 1, 1, d), lambda i, j, f: (row_blk + i, 5, 0, 0)),
                  pl.BlockSpec((d, tf), lambda i, j, f: (0, f)),
                  pl.BlockSpec((tf, d), lambda i, j, f: (f, 0))],
        out_specs=act,
        scratch_shapes=[pltpu.VMEM((nb * t, d), F32)],
        compiler_params=_params("parallel", "parallel", "arbitrary"),
        name="mlp",
    )(h2, x1, mod4, w1, w2)


def kernel(x_prompt, x_sample, c_prompt, c_sample, cache_k, cache_v, cache_logf, state_ssm_re, state_ssm_im,
           w_ada, b_ada, g_norm1, w_in, g_q, g_k, b_f, log_dt, a_re, a_im, b_re, b_im, c_re, c_im, d_skip,
           w_glu, b_glu, g_att_out, g_ssm_out, w_out, g_norm2, w_ff1, w_ff2):
    depth = w_ada.shape[0]
    assert depth == 1, "single-layer step"
    bsz, seq, d = x_prompt.shape
    dec_b, dec_s, _ = x_sample.shape
    past = cache_k.shape[2]
    n_heads = cache_k.shape[3]
    d_att = n_heads * HEAD_DIM
    n_groups, n_state = a_re.shape[1], a_re.shape[2]
    d_ssm = n_groups * SSM_GROUP
    n_blk = n_groups // GROUPS_PER_BLOCK
    mod_rows = 16
    assert dec_b + bsz <= mod_rows
    l = 0

    c_rows = jnp.concatenate([c_sample, c_prompt, jnp.zeros((mod_rows - dec_b - bsz, d), F32)], axis=0)
    mod4 = _modulation(c_rows, w_ada[l], b_ada[l][None]).reshape(mod_rows, N_MOD, 1, d)
    row_sample, row_prompt = 0, dec_b

    w = w_in[l]
    w_qkv = w[:, 0:3 * d_att].astype(BF16)
    w_f = jnp.pad(w[:, 3 * d_att:3 * d_att + n_heads], ((0, 0), (0, LANES - n_heads))).astype(BF16)
    w_u32 = w[:, 3 * d_att + n_heads:]
    w_u = w_u32.astype(BF16)
    b_f_pad = jnp.pad(b_f[l], (0, LANES - n_heads))[None]
    g1 = g_norm1[l][None]
    gq, gk = g_q[l][None], g_k[l][None]

    zb_re, zb_im, pw_re, pw_im = _s5_params(log_dt[l], a_re[l], a_im[l], b_re[l], b_im[l])
    pw_re = pw_re.reshape(SUBLANES, n_groups * n_state)
    pw_im = pw_im.reshape(SUBLANES, n_groups * n_state)
    blk4 = lambda m: m.reshape(n_blk, GROUPS_PER_BLOCK, m.shape[1], m.shape[2])
    wb32 = jnp.concatenate([_block_diag(blk4(zb_re)), _block_diag(blk4(zb_im))], axis=-1)
    ct_re = jnp.swapaxes(c_re[l], 1, 2)
    ct_im = jnp.swapaxes(c_im[l], 1, 2)
    wc = jnp.concatenate([_block_diag(blk4(ct_re)), _block_diag(blk4(-ct_im))], axis=1).astype(BF16)
    wb = wb32.astype(BF16)
    d_row = d_skip[l][None]
    w_glu_b = w_glu[l].astype(BF16)
    b_glu_row = b_glu[l][None]
    g_ssm_row = g_ssm_out[l][None]
    g_att_row = g_att_out[l][None]
    w_out_b = w_out[l].astype(BF16)
    g2 = g_norm2[l][None]
    w1 = w_ff1[l].astype(BF16)
    w2 = w_ff2[l].astype(BF16)

    tm = 512

    q_p, k32_p, kbf_p, v32_p, vbf_p, logf_p, u_p = _in_proj(
        x_prompt, mod4, row_prompt, g1, w_qkv, w_u, w_f, gq, gk, b_f_pad,
        nb=1, t=tm, n_heads=n_heads, precise_u=False)
    fcum_p = _forget_cumsum(jnp.swapaxes(logf_p, 1, 2).reshape(bsz * n_heads, seq))
    att_p = _prompt_attention(q_p, kbf_p, vbf_p, fcum_p.reshape(bsz, n_heads, seq), n_heads=n_heads, tq=tm)
    zeros_state = jnp.zeros((bsz, 1, n_groups * n_state), F32)
    ssm_p, hre_p, him_p = _s5_mixer(u_p, zeros_state, zeros_state, wb, wc, pw_re, pw_im, d_row, w_glu_b,
                                    b_glu_row, g_ssm_row, tc=256, precise=False)
    x1_p, h2_p = _out_proj(att_p, ssm_p, x_prompt, mod4, row_prompt, g_att_row, w_out_b, g2, nb=1, t=tm)
    y_p = _mlp(h2_p, x1_p, mod4, row_prompt, w1, w2, nb=1, t=tm, tf=1024)

    q_s, k32_s, kbf_s, v32_s, vbf_s, logf_s, u_s = _in_proj(
        x_sample, mod4, row_sample, g1, w_qkv, w_u32, w_f, gq, gk, b_f_pad,
        nb=dec_b, t=dec_s, n_heads=n_heads, precise_u=True)
    total = past + dec_s
    lpad = -(-total // LANES) * LANES
    logf_all = jnp.concatenate([cache_logf[l], logf_s, jnp.zeros((dec_b, lpad - total, n_heads), F32)], axis=1)
    fcum_s = _forget_cumsum(jnp.swapaxes(logf_all, 1, 2).reshape(dec_b * n_heads, lpad))
    att_s = _sample_attention(q_s, kbf_s, vbf_s, cache_k[l].reshape(dec_b, past, d_att),
                              cache_v[l].reshape(dec_b, past, d_att),
                              fcum_s.reshape(dec_b, n_heads, lpad), n_heads=n_heads)
    ssm_s, hre_s, him_s = _s5_mixer(u_s, state_ssm_re[l].reshape(dec_b, 1, -1),
                                    state_ssm_im[l].reshape(dec_b, 1, -1), wb32, wc, pw_re, pw_im, d_row,
                                    w_glu_b, b_glu_row, g_ssm_row, tc=dec_s, precise=True)
    x1_s, h2_s = _out_proj(att_s, ssm_s, x_sample, mod4, row_sample, g_att_row, w_out_b, g2,
                           nb=dec_b, t=dec_s)
    y_s = _mlp(h2_s, x1_s, mod4, row_sample, w1, w2, nb=dec_b, t=dec_s, tf=1024)

    heads = lambda a, n, s: a.reshape(1, n, s, n_heads, HEAD_DIM)
    states = lambda a, n: a.reshape(1, n, n_groups, n_state)
    return (y_p, y_s,
            heads(k32_p, bsz, seq), heads(v32_p, bsz, seq), logf_p[None],
            states(hre_p, bsz), states(him_p, bsz),
            heads(k32_s, dec_b, dec_s), heads(v32_s, dec_b, dec_s), logf_s[None],
            states(hre_s, dec_b), states(him_s, dec_b))
```

```python
import functools
import math

import jax
import jax.numpy as jnp
from jax import lax
from jax.experimental import pallas as pl
from jax.experimental.pallas import tpu as pltpu

F32 = jnp.float32
BF16 = jnp.bfloat16
HIGHEST = lax.Precision.HIGHEST

HEAD_DIM = 128
SSM_GROUP = 16
N_MOD = 6
EPS = 1e-6
LOG2E = 1.4426950408889634

LANES = 128
SUBLANES = 8
GROUPS_PER_BLOCK = 16
MAX_SEG = 32
N_POW = MAX_SEG + 2
VMEM_LIMIT = 56 * 1024 * 1024


def _params(*sem):
    return pltpu.CompilerParams(dimension_semantics=sem, vmem_limit_bytes=VMEM_LIMIT)


def _resident(shape):
    nd = len(shape)
    return pl.BlockSpec(shape, lambda *_: (0,) * nd, pipeline_mode=pl.Buffered(1))


def _sigmoid(x):
    return 1.0 / (1.0 + jnp.exp(-x))


def _rms_rows(x, g):
    return x * lax.rsqrt(jnp.mean(x * x, axis=-1, keepdims=True) + EPS) * g


def _dot_nt(a, b):
    return lax.dot_general(a, b, (((1,), (1,)), ((), ())), preferred_element_type=F32)


def _cmul_add(x_re, x_im, a_re, a_im, h_re, h_im):
    return x_re + (a_re * h_re - a_im * h_im), x_im + (a_re * h_im + a_im * h_re)


def _stream_permutation(n, inverse):
    i = lax.broadcasted_iota(jnp.int32, (n, n), 0)
    j = lax.broadcasted_iota(jnp.int32, (n, n), 1)
    if inverse:
        i, j = j, i
    src = (i & (SUBLANES - 1)) * (n // SUBLANES) + (i >> 3)
    return jnp.where(j == src, 1.0, 0.0).astype(BF16)


def _mod_kernel(c_ref, w_ref, b_ref, o_ref):
    c = c_ref[...]
    s = c * _sigmoid(c)
    o_ref[...] = jnp.dot(s, w_ref[...], precision=HIGHEST, preferred_element_type=F32) + b_ref[...]


def _modulation(c_rows, w_ada, b_ada):
    rows, d = c_rows.shape
    n = w_ada.shape[1]
    tn = 512
    return pl.pallas_call(
        _mod_kernel,
        out_shape=jax.ShapeDtypeStruct((rows, n), F32),
        grid=(n // tn,),
        in_specs=[pl.BlockSpec((rows, d), lambda j: (0, 0)),
                  pl.BlockSpec((d, tn), lambda j: (0, j)),
                  pl.BlockSpec((1, tn), lambda j: (0, j))],
        out_specs=pl.BlockSpec((rows, tn), lambda j: (0, j)),
        compiler_params=_params("arbitrary"),
        name="modulation",
    )(c_rows, w_ada, b_ada)


def _s5_param_kernel(logdt_ref, are_ref, aim_ref, bre_ref, bim_ref, zbre_ref, zbim_ref, pwre_ref, pwim_ref):
    dt = jnp.exp(logdt_ref[...])
    lam_re = jnp.minimum(are_ref[...], -1e-4)
    lam_im = aim_ref[...]
    mag = jnp.exp(lam_re * dt)
    ang = lam_im * dt
    ab_re = mag * jnp.cos(ang)
    ab_im = mag * jnp.sin(ang)
    den = lam_re * lam_re + lam_im * lam_im
    z_re = ((ab_re - 1.0) * lam_re + ab_im * lam_im) / den
    z_im = (ab_im * lam_re - (ab_re - 1.0) * lam_im) / den
    b_re = bre_ref[...]
    b_im = bim_ref[...]
    zbre_ref[...] = z_re * b_re - z_im * b_im
    zbim_ref[...] = z_re * b_im + z_im * b_re
    p_re, p_im = ab_re, ab_im
    for k in range(MAX_SEG):
        pwre_ref[k] = p_re
        pwim_ref[k] = p_im
        if k + 1 < MAX_SEG:
            p_re, p_im = p_re * ab_re - p_im * ab_im, p_re * ab_im + p_im * ab_re
    for k in range(MAX_SEG, N_POW):
        p_re, p_im = p_re * p_re - p_im * p_im, 2.0 * (p_re * p_im)
        pwre_ref[k] = p_re
        pwim_ref[k] = p_im


def _s5_params(log_dt, a_re, a_im, b_re, b_im):
    g, n, c = b_re.shape
    f = lambda shape: jax.ShapeDtypeStruct(shape, F32)
    return pl.pallas_call(
        _s5_param_kernel,
        out_shape=(f((g, c, n)), f((g, c, n)), f((N_POW, g, 1, n)), f((N_POW, g, 1, n))),
        name="s5_params",
    )(log_dt.reshape(g, 1, 1), a_re.reshape(g, 1, n), a_im.reshape(g, 1, n),
      jnp.swapaxes(b_re, 1, 2), jnp.swapaxes(b_im, 1, 2))


def _pow_row(seg_rows):
    if seg_rows <= MAX_SEG:
        return seg_rows - 1
    return MAX_SEG - 1 + int(math.log2(seg_rows // MAX_SEG))


def _block_diag(w):
    nb, g, r, c = w.shape
    eye = jnp.eye(g, dtype=w.dtype)
    return (w[:, :, :, None, :] * eye[None, :, None, :, None]).reshape(nb, g * r, g * c)


def _in_proj_kernel(x_ref, sh_ref, sc_ref, g1_ref, wqkv_ref, wu_ref, wf_ref, gq_ref, gk_ref, bf_ref,
                    q_ref, k32_ref, kbf_ref, v32_ref, vbf_ref, logf_ref, u_ref,
                    *, n_heads, q_scale, s5_chunk):
    nb, t, d = x_ref.shape
    rows = nb * t
    d_att = n_heads * HEAD_DIM
    x = x_ref[...]
    y = _rms_rows(x, g1_ref[...])
    h = (y * (1.0 + sc_ref[:, 0]) + sh_ref[:, 0]).reshape(rows, d)
    hb = h.astype(BF16)

    q = jnp.dot(hb, wqkv_ref[:, 0:d_att], preferred_element_type=F32)
    for hh in range(n_heads):
        sl = slice(hh * HEAD_DIM, (hh + 1) * HEAD_DIM)
        qn = _rms_rows(q[:, sl], gq_ref[...]) * q_scale
        q_ref[:, :, sl] = qn.reshape(nb, t, HEAD_DIM).astype(BF16)

    k = jnp.dot(hb, wqkv_ref[:, d_att:2 * d_att], preferred_element_type=F32)
    for hh in range(n_heads):
        sl = slice(hh * HEAD_DIM, (hh + 1) * HEAD_DIM)
        kn = _rms_rows(k[:, sl], gk_ref[...]).reshape(nb, t, HEAD_DIM)
        k32_ref[:, :, hh, :] = kn
        kbf_ref[:, :, sl] = kn.astype(BF16)

    v = jnp.dot(hb, wqkv_ref[:, 2 * d_att:3 * d_att], preferred_element_type=F32)
    for hh in range(n_heads):
        sl = slice(hh * HEAD_DIM, (hh + 1) * HEAD_DIM)
        v32_ref[:, :, hh, :] = v[:, sl].reshape(nb, t, HEAD_DIM)
    vbf_ref[...] = v.reshape(nb, t, d_att).astype(BF16)

    f = jnp.dot(hb, wf_ref[...], preferred_element_type=F32) + bf_ref[...]
    logf = jnp.minimum(f, 0.0) - jnp.log1p(jnp.exp(-jnp.abs(f)))
    logf_ref[...] = logf[:, 0:n_heads].reshape(nb, t, n_heads)

    u = jnp.dot(hb, wu_ref[...], preferred_element_type=F32).astype(BF16)
    perm = _stream_permutation(s5_chunk, inverse=False)
    for c in range(rows // s5_chunk):
        uc = jnp.dot(perm, u[c * s5_chunk:(c + 1) * s5_chunk], preferred_element_type=F32).astype(BF16)
        if nb == 1:
            u_ref[0, c * s5_chunk:(c + 1) * s5_chunk, :] = uc
        else:
            per = t // s5_chunk
            u_ref[c // per, (c % per) * s5_chunk:(c % per + 1) * s5_chunk, :] = uc


def _in_proj(x, mod4, mod_row, g1, w_qkv, w_u, w_f, g_q, g_k, b_f_pad, *, nb, t, n_heads, s5_chunk):
    n_seq, seq, d = x.shape
    d_att = n_heads * HEAD_DIM
    d_ssm = w_u.shape[1]
    assert t % s5_chunk == 0
    grid = (n_seq // nb, seq // t)
    row_blk = mod_row // nb
    act = lambda width: pl.BlockSpec((nb, t, width), lambda i, j: (i, j, 0))
    act4 = pl.BlockSpec((nb, t, n_heads, HEAD_DIM), lambda i, j: (i, j, 0, 0))
    mod = lambda m: pl.BlockSpec((nb, 1, 1, d), lambda i, j, m=m: (row_blk + i, m, 0, 0))
    sds = lambda width, dt: jax.ShapeDtypeStruct((n_seq, seq, width), dt)
    sds4 = jax.ShapeDtypeStruct((n_seq, seq, n_heads, HEAD_DIM), F32)
    kern = functools.partial(_in_proj_kernel, n_heads=n_heads, q_scale=HEAD_DIM ** -0.5 * LOG2E,
                             s5_chunk=s5_chunk)
    return pl.pallas_call(
        kern,
        out_shape=(sds(d_att, BF16), sds4, sds(d_att, BF16), sds4, sds(d_att, BF16),
                   sds(n_heads, F32), sds(d_ssm, BF16)),
        grid=grid,
        in_specs=[act(d), mod(0), mod(1), _resident(g1.shape), _resident(w_qkv.shape), _resident(w_u.shape),
                  _resident(w_f.shape), _resident(g_q.shape), _resident(g_k.shape), _resident(b_f_pad.shape)],
        out_specs=(act(d_att), act4, act(d_att), act4, act(d_att), act(n_heads), act(d_ssm)),
        compiler_params=_params("parallel", "parallel"),
        name="in_proj",
    )(x, mod4, mod4, g1, w_qkv, w_u, w_f, g_q, g_k, b_f_pad)


def _cumsum_kernel(x_ref, o_ref, carry_ref):
    @pl.when(pl.program_id(1) == 0)
    def _():
        carry_ref[...] = jnp.zeros(carry_ref.shape, F32)

    x = x_ref[0]
    n = x.shape[0]
    r = lax.broadcasted_iota(jnp.int32, (n, n), 0)
    c = lax.broadcasted_iota(jnp.int32, (n, n), 1)
    tri = jnp.where(c <= r, 1.0, 0.0)
    f = jnp.dot(tri, x, precision=HIGHEST, preferred_element_type=F32) + carry_ref[...]
    o_ref[0] = f * LOG2E
    carry_ref[...] = f[n - 1:n, :]


def _forget_cumsum(logf, *, tc):
    b, length, h = logf.shape
    blk = pl.BlockSpec((1, tc, h), lambda i, j: (i, j, 0))
    return pl.pallas_call(
        _cumsum_kernel,
        out_shape=jax.ShapeDtypeStruct(logf.shape, F32),
        grid=(b, length // tc),
        in_specs=[blk],
        out_specs=blk,
        scratch_shapes=[pltpu.VMEM((1, h), F32)],
        compiler_params=_params("parallel", "arbitrary"),
        name="forget_cumsum",
    )(logf)


def _bias_columns(f_rows, head, key_side):
    n = f_rows.shape[0]
    lane_h = lax.broadcasted_iota(jnp.int32, f_rows.shape, 1)
    col = jnp.sum(jnp.where(lane_h == head, f_rows, 0.0), axis=-1, keepdims=True)
    hi = col.astype(BF16).astype(F32)
    rest = col - hi
    mid = rest.astype(BF16).astype(F32)
    lo = rest - mid
    lane = lax.broadcasted_iota(jnp.int32, (n, LANES), 1)
    if key_side:
        split = jnp.where(lane == 3, -hi, jnp.where(lane == 4, -mid, jnp.where(lane == 5, -lo, 0.0)))
        vals = jnp.where(lane < 3, 1.0, split)
    else:
        split = jnp.where(lane == 0, hi, jnp.where(lane == 1, mid, jnp.where(lane == 2, lo, 0.0)))
        vals = jnp.where((lane >= 3) & (lane < 6), 1.0, split)
    return vals.astype(BF16)


def _attn_kernel(q_ref, k_ref, v_ref, f_ref, o_ref, kaug_ref, vaug_ref, qa_ref, s0_ref, s1_ref, m_ref, acc_ref,
                 *, tq, tk, build_rows):
    head = pl.program_id(1)
    qi = pl.program_id(2)
    t = k_ref.shape[1]
    dh = HEAD_DIM

    @pl.when(qi == 0)
    def _build_keys():
        kaug_ref[:, 0:dh] = k_ref[0]
        vaug_ref[:, 0:dh] = v_ref[0]
        vaug_ref[:, dh:2 * dh] = jnp.ones((t, dh), BF16)

        def piece(i, carry):
            r0 = pl.multiple_of(i * build_rows, build_rows)
            kaug_ref[pl.ds(r0, build_rows), dh:2 * dh] = _bias_columns(
                f_ref[0, pl.ds(r0, build_rows), :], head, True)
            return carry

        lax.fori_loop(0, t // build_rows, piece, 0)

    q0 = pl.multiple_of(qi * tq, tq)
    qa_ref[:, 0:dh] = q_ref[0]
    qa_ref[:, dh:2 * dh] = _bias_columns(f_ref[0, pl.ds(q0, tq), :], head, False)
    m_ref[...] = jnp.full(m_ref.shape, -jnp.inf, F32)
    acc_ref[...] = jnp.zeros(acc_ref.shape, F32)

    def scores(c, s_ref):
        off = pl.multiple_of(c * tk, tk)
        s_ref[...] = _dot_nt(qa_ref[...], kaug_ref[pl.ds(off, tk), :])

    def absorb(c, s_ref, masked):
        off = pl.multiple_of(c * tk, tk)
        s = s_ref[...]
        if masked:
            ahead = (lax.broadcasted_iota(jnp.int32, s.shape, 1)
                     - lax.broadcasted_iota(jnp.int32, s.shape, 0))
            s = jnp.where(ahead <= q0 - off, s, -jnp.inf)
        m_old = m_ref[...]
        m_new = jnp.maximum(m_old, jnp.max(s, axis=-1, keepdims=True))
        alpha = jnp.exp2(m_old - m_new)
        p = jnp.exp2(s - jnp.tile(m_new, (1, tk // LANES)))
        pv = jnp.dot(p.astype(BF16), vaug_ref[pl.ds(off, tk), :], preferred_element_type=F32)
        acc_ref[...] = jnp.tile(alpha, (1, 2)) * acc_ref[...] + pv
        m_ref[...] = m_new

    n_full = (qi * tq) // tk
    scores(0, s0_ref)

    def pair(p, carry):
        c = 2 * p
        scores(c + 1, s1_ref)
        absorb(c, s0_ref, False)
        scores(c + 2, s0_ref)
        absorb(c + 1, s1_ref, False)
        return carry

    lax.fori_loop(0, n_full // 2, pair, 0)

    @pl.when(n_full % 2 == 1)
    def _odd_tail():
        scores(n_full, s1_ref)
        absorb(n_full - 1, s0_ref, False)
        absorb(n_full, s1_ref, True)

    @pl.when(n_full % 2 == 0)
    def _even_tail():
        absorb(n_full, s0_ref, True)

    acc = acc_ref[...]
    o_ref[0] = (acc[:, 0:dh] * (1.0 / acc[:, dh:2 * dh])).astype(o_ref.dtype)


def _prompt_attention(q, k, v, f_cum2, *, n_heads, tq, tk):
    b, t, _ = q.shape
    assert t % tk == 0 and tk % tq == 0
    per_head = lambda rows: pl.BlockSpec((1, rows, HEAD_DIM), (lambda bi, h, i: (bi, i, h)) if rows == tq
                                         else (lambda bi, h, i: (bi, 0, h)))
    return pl.pallas_call(
        functools.partial(_attn_kernel, tq=tq, tk=tk, build_rows=1024),
        out_shape=jax.ShapeDtypeStruct(q.shape, BF16),
        grid=(b, n_heads, t // tq),
        in_specs=[per_head(tq), per_head(t), per_head(t),
                  pl.BlockSpec((1, t, n_heads), lambda bi, h, i: (bi, 0, 0))],
        out_specs=per_head(tq),
        scratch_shapes=[pltpu.VMEM((t, 2 * HEAD_DIM), BF16), pltpu.VMEM((t, 2 * HEAD_DIM), BF16),
                        pltpu.VMEM((tq, 2 * HEAD_DIM), BF16), pltpu.VMEM((tq, tk), F32), pltpu.VMEM((tq, tk), F32),
                        pltpu.VMEM((tq, LANES), F32), pltpu.VMEM((tq, 2 * HEAD_DIM), F32)],
        compiler_params=_params("parallel", "arbitrary", "arbitrary"),
        name="prompt_attention",
    )(q, k, v, f_cum2)


def _sample_attn_kernel(q_ref, kc_ref, vc_ref, kn_ref, vn_ref, f_ref, o_ref, *, past, n_heads):
    s_len = q_ref.shape[1]
    f_old = f_ref[0, 0:past, :]
    f_new = f_ref[0, past:past + s_len, :]
    row = lax.broadcasted_iota(jnp.int32, (s_len, s_len), 0)
    col = lax.broadcasted_iota(jnp.int32, (s_len, s_len), 1)
    for hh in range(n_heads):
        sl = slice(hh * HEAD_DIM, (hh + 1) * HEAD_DIM)
        qa = jnp.concatenate([q_ref[0, :, sl], _bias_columns(f_new, hh, False)], axis=-1)
        kca = jnp.concatenate([kc_ref[0, :, hh, :].astype(BF16), _bias_columns(f_old, hh, True)], axis=-1)
        kna = jnp.concatenate([kn_ref[0, :, sl], _bias_columns(f_new, hh, True)], axis=-1)
        s_c = _dot_nt(qa, kca)
        s_n = jnp.where(col <= row, _dot_nt(qa, kna), -jnp.inf)
        m = jnp.maximum(jnp.max(s_c, axis=-1, keepdims=True), jnp.max(s_n, axis=-1, keepdims=True))
        p_c = jnp.exp2(s_c - m)
        p_n = jnp.exp2(s_n - m)
        l = jnp.sum(p_c, axis=-1, keepdims=True) + jnp.sum(p_n, axis=-1, keepdims=True)
        o = (jnp.dot(p_c.astype(BF16), vc_ref[0, :, hh, :].astype(BF16), preferred_element_type=F32)
             + jnp.dot(p_n.astype(BF16), vn_ref[0, :, sl], preferred_element_type=F32))
        o_ref[0, :, sl] = (o * (1.0 / l)).astype(o_ref.dtype)


def _sample_attention(q, k_new, v_new, cache_k, cache_v, f_cum2, *, n_heads):
    b, s_len, d_att = q.shape
    past = cache_k.shape[1]
    new = pl.BlockSpec((1, s_len, d_att), lambda bi: (bi, 0, 0))
    old = pl.BlockSpec((1, past, n_heads, HEAD_DIM), lambda bi: (bi, 0, 0, 0))
    return pl.pallas_call(
        functools.partial(_sample_attn_kernel, past=past, n_heads=n_heads),
        out_shape=jax.ShapeDtypeStruct(q.shape, BF16),
        grid=(b,),
        in_specs=[new, old, old, new, new, pl.BlockSpec((1, past + s_len, n_heads), lambda bi: (bi, 0, 0))],
        out_specs=new,
        compiler_params=_params("parallel"),
        name="sample_attention",
    )(q, cache_k, cache_v, k_new, v_new, f_cum2)


def _s5_kernel(u_ref, h0re_ref, h0im_ref, wb_ref, wc_ref, pwre_ref, pwim_ref, d_ref, wglu_ref, bglu_ref,
               gout_ref, o_ref, hre_ref, him_ref, xre_ref, xim_ref, y_ref):
    tc = u_ref.shape[1]
    seg = tc // SUBLANES
    n_blk, ch = wb_ref.shape[0], wb_ref.shape[1]
    ns = wb_ref.shape[2] // 2

    @pl.when(pl.program_id(1) == 0)
    def _():
        hre_ref[...] = h0re_ref[...]
        him_ref[...] = h0im_ref[...]

    sub = lax.broadcasted_iota(jnp.int32, (SUBLANES, ns), 0)
    rows = lambda r: slice(r * SUBLANES, (r + 1) * SUBLANES)
    for blk in range(n_blk):
        cols = slice(blk * ns, (blk + 1) * ns)
        chs = slice(blk * ch, (blk + 1) * ch)
        ub = u_ref[0, :, chs]
        bu = jnp.dot(ub, wb_ref[blk], preferred_element_type=F32)

        a_re, a_im = pwre_ref[rows(0), cols], pwim_ref[rows(0), cols]
        h_re = bu[rows(0), 0:ns]
        h_im = bu[rows(0), ns:2 * ns]
        xre_ref[rows(0), :] = h_re
        xim_ref[rows(0), :] = h_im
        for r in range(1, seg):
            h_re, h_im = _cmul_add(bu[rows(r), 0:ns], bu[rows(r), ns:2 * ns], a_re, a_im, h_re, h_im)
            xre_ref[rows(r), :] = h_re
            xim_ref[rows(r), :] = h_im

        s_re = jnp.where(sub == 0, hre_ref[0, :, cols], pltpu.roll(h_re, 1, 0))
        s_im = jnp.where(sub == 0, him_ref[0, :, cols], pltpu.roll(h_im, 1, 0))
        for shift in (1, 2, 4):
            row = _pow_row(seg * shift)
            m_re = jnp.where(sub >= shift, pwre_ref[rows(row), cols], 0.0)
            m_im = jnp.where(sub >= shift, pwim_ref[rows(row), cols], 0.0)
            s_re, s_im = _cmul_add(s_re, s_im, m_re, m_im, pltpu.roll(s_re, shift, 0), pltpu.roll(s_im, shift, 0))

        for r in range(seg):
            t_re, t_im = _cmul_add(xre_ref[rows(r), :], xim_ref[rows(r), :],
                                   pwre_ref[rows(r), cols], pwim_ref[rows(r), cols], s_re, s_im)
            xre_ref[rows(r), :] = t_re
            xim_ref[rows(r), :] = t_im
        hre_ref[0, :, cols] = t_re[SUBLANES - 1:SUBLANES, :]
        him_ref[0, :, cols] = t_im[SUBLANES - 1:SUBLANES, :]

        y = (jnp.dot(xre_ref[...].astype(BF16), wc_ref[blk, 0:ns, :], preferred_element_type=F32)
             + jnp.dot(xim_ref[...].astype(BF16), wc_ref[blk, ns:2 * ns, :], preferred_element_type=F32))
        y_ref[:, chs] = y + d_ref[:, chs] * ub.astype(F32)

    y = y_ref[...]
    g = y * (0.5 * (1.0 + jnp.tanh(math.sqrt(2.0 / math.pi) * (y + 0.044715 * (y * y * y)))))
    gate = _sigmoid(jnp.dot(g.astype(BF16), wglu_ref[...], preferred_element_type=F32) + bglu_ref[...])
    out = _rms_rows(g * gate, gout_ref[...]).astype(BF16)
    o_ref[0] = jnp.dot(_stream_permutation(tc, inverse=True), out, preferred_element_type=F32).astype(o_ref.dtype)


def _s5_mixer(u, h0_re, h0_im, wb, wc, pw_re, pw_im, d_skip, w_glu, b_glu, g_out, *, tc):
    b, t, d_ssm = u.shape
    n_state = h0_re.shape[-1]
    assert tc // SUBLANES <= MAX_SEG
    state = pl.BlockSpec((1, 1, n_state), lambda bi, j: (bi, 0, 0))
    act = pl.BlockSpec((1, tc, d_ssm), lambda bi, j: (bi, j, 0))
    ns = wb.shape[2] // 2
    return pl.pallas_call(
        _s5_kernel,
        out_shape=(jax.ShapeDtypeStruct((b, t, d_ssm), BF16),
                   jax.ShapeDtypeStruct((b, 1, n_state), F32), jax.ShapeDtypeStruct((b, 1, n_state), F32)),
        grid=(b, t // tc),
        in_specs=[act, state, state, _resident(wb.shape), _resident(wc.shape), _resident(pw_re.shape),
                  _resident(pw_im.shape), _resident(d_skip.shape), _resident(w_glu.shape),
                  _resident(b_glu.shape), _resident(g_out.shape)],
        out_specs=(act, state, state),
        scratch_shapes=[pltpu.VMEM((tc, ns), F32), pltpu.VMEM((tc, ns), F32), pltpu.VMEM((tc, d_ssm), F32)],
        compiler_params=_params("parallel", "arbitrary"),
        name="s5_mixer",
    )(u, h0_re, h0_im, wb, wc, pw_re, pw_im, d_skip, w_glu, b_glu, g_out)


def _out_proj_kernel(att_ref, ssm_ref, x_ref, gt1_ref, sh2_ref, sc2_ref, gatt_ref, wout_ref, g2_ref,
                     x1_ref, h2_ref):
    nb, t, d = x_ref.shape
    rows = nb * t
    d_att = att_ref.shape[-1]
    a = att_ref[...].astype(F32).reshape(rows, d_att)
    an = _rms_rows(a, gatt_ref[...]).astype(BF16)
    mix = (jnp.dot(an, wout_ref[0:d_att, :], preferred_element_type=F32)
           + jnp.dot(ssm_ref[...].reshape(rows, ssm_ref.shape[-1]), wout_ref[d_att:, :],
                     preferred_element_type=F32))
    x1 = x_ref[...] + gt1_ref[:, 0] * mix.reshape(nb, t, d)
    x1_ref[...] = x1
    h2 = _rms_rows(x1, g2_ref[...]) * (1.0 + sc2_ref[:, 0]) + sh2_ref[:, 0]
    h2_ref[...] = h2.astype(BF16)


def _out_proj(att, ssm, x, mod4, mod_row, g_att, w_out, g2, *, nb, t):
    n_seq, seq, d = x.shape
    row_blk = mod_row // nb
    act = lambda width: pl.BlockSpec((nb, t, width), lambda i, j: (i, j, 0))
    mod = lambda m: pl.BlockSpec((nb, 1, 1, d), lambda i, j, m=m: (row_blk + i, m, 0, 0))
    return pl.pallas_call(
        _out_proj_kernel,
        out_shape=(jax.ShapeDtypeStruct(x.shape, F32), jax.ShapeDtypeStruct(x.shape, BF16)),
        grid=(n_seq // nb, seq // t),
        in_specs=[act(att.shape[-1]), act(ssm.shape[-1]), act(d), mod(2), mod(3), mod(4),
                  _resident(g_att.shape), _resident(w_out.shape), _resident(g2.shape)],
        out_specs=(act(d), act(d)),
        compiler_params=_params("parallel", "parallel"),
        name="out_proj",
    )(att, ssm, x, mod4, mod4, mod4, g_att, w_out, g2)


def _mlp_kernel(h2_ref, x1_ref, gt2_ref, w1_ref, w2_ref, o_ref, acc_ref):
    nb, t, d = x1_ref.shape
    j = pl.program_id(2)

    @pl.when(j == 0)
    def _():
        acc_ref[...] = jnp.zeros(acc_ref.shape, F32)

    a = jnp.dot(h2_ref[...].reshape(nb * t, d), w1_ref[...], preferred_element_type=F32)
    r = jnp.maximum(a, 0.0)
    acc_ref[...] += jnp.dot((r * r).astype(BF16), w2_ref[...], preferred_element_type=F32)

    @pl.when(j == pl.num_programs(2) - 1)
    def _():
        o_ref[...] = x1_ref[...] + gt2_ref[:, 0] * acc_ref[...].reshape(nb, t, d)


def _mlp(h2, x1, mod4, mod_row, w1, w2, *, nb, t, tf):
    n_seq, seq, d = x1.shape
    d_ff = w1.shape[1]
    row_blk = mod_row // nb
    act = pl.BlockSpec((nb, t, d), lambda i, j, f: (i, j, 0))
    return pl.pallas_call(
        _mlp_kernel,
        out_shape=jax.ShapeDtypeStruct(x1.shape, F32),
        grid=(n_seq // nb, seq // t, d_ff // tf),
        in_specs=[act, act,
                  pl.BlockSpec((nb, 1, 1, d), lambda i, j, f: (row_blk + i, 5, 0, 0)),
                  pl.BlockSpec((d, tf), lambda i, j, f: (0, f)),
                  pl.BlockSpec((tf, d), lambda i, j, f: (f, 0))],
        out_specs=act,
        scratch_shapes=[pltpu.VMEM((nb * t, d), F32)],
        compiler_params=_params("parallel", "parallel", "arbitrary"),
        name="mlp",
    )(h2, x1, mod4, w1, w2)


def kernel(x_prompt, x_sample, c_prompt, c_sample, cache_k, cache_v, cache_logf, state_ssm_re, state_ssm_im,
           w_ada, b_ada, g_norm1, w_in, g_q, g_k, b_f, log_dt, a_re, a_im, b_re, b_im, c_re, c_im, d_skip,
           w_glu, b_glu, g_att_out, g_ssm_out, w_out, g_norm2, w_ff1, w_ff2):
    depth = w_ada.shape[0]
    assert depth == 1, "single-layer step"
    bsz, seq, d = x_prompt.shape
    dec_b, dec_s, _ = x_sample.shape
    past = cache_k.shape[2]
    n_heads = cache_k.shape[3]
    d_att = n_heads * HEAD_DIM
    n_groups, n_state = a_re.shape[1], a_re.shape[2]
    n_blk = n_groups // GROUPS_PER_BLOCK
    mod_rows = 16
    assert dec_b + bsz <= mod_rows
    l = 0

    c_rows = jnp.concatenate([c_sample, c_prompt, jnp.zeros((mod_rows - dec_b - bsz, d), F32)], axis=0)
    mod4 = _modulation(c_rows, w_ada[l], b_ada[l][None]).reshape(mod_rows, N_MOD, 1, d)
    row_sample, row_prompt = 0, dec_b

    w = w_in[l]
    w_qkv = w[:, 0:3 * d_att].astype(BF16)
    w_f = jnp.pad(w[:, 3 * d_att:3 * d_att + n_heads], ((0, 0), (0, LANES - n_heads))).astype(BF16)
    w_u = w[:, 3 * d_att + n_heads:].astype(BF16)
    b_f_pad = jnp.pad(b_f[l], (0, LANES - n_heads))[None]
    g1 = g_norm1[l][None]
    gq, gk = g_q[l][None], g_k[l][None]

    zb_re, zb_im, pw_re, pw_im = _s5_params(log_dt[l], a_re[l], a_im[l], b_re[l], b_im[l])
    pw_re = jnp.repeat(pw_re.reshape(N_POW, n_groups * n_state), SUBLANES, axis=0)
    pw_im = jnp.repeat(pw_im.reshape(N_POW, n_groups * n_state), SUBLANES, axis=0)
    blk4 = lambda m: m.reshape(n_blk, GROUPS_PER_BLOCK, m.shape[1], m.shape[2])
    wb = jnp.concatenate([_block_diag(blk4(zb_re)), _block_diag(blk4(zb_im))], axis=-1).astype(BF16)
    ct_re = jnp.swapaxes(c_re[l], 1, 2)
    ct_im = jnp.swapaxes(c_im[l], 1, 2)
    wc = jnp.concatenate([_block_diag(blk4(ct_re)), _block_diag(blk4(-ct_im))], axis=1).astype(BF16)
    d_row = d_skip[l][None]
    w_glu_b = w_glu[l].astype(BF16)
    b_glu_row = b_glu[l][None]
    g_ssm_row = g_ssm_out[l][None]
    g_att_row = g_att_out[l][None]
    w_out_b = w_out[l].astype(BF16)
    g2 = g_norm2[l][None]
    w1 = w_ff1[l].astype(BF16)
    w2 = w_ff2[l].astype(BF16)

    tm = 512
    tc_prompt = SUBLANES * MAX_SEG
    tc_sample = dec_s

    q_p, k32_p, kbf_p, v32_p, vbf_p, logf_p, u_p = _in_proj(
        x_prompt, mod4, row_prompt, g1, w_qkv, w_u, w_f, gq, gk, b_f_pad,
        nb=1, t=tm, n_heads=n_heads, s5_chunk=tc_prompt)
    fcum_p = _forget_cumsum(logf_p, tc=1024)
    att_p = _prompt_attention(q_p, kbf_p, vbf_p, fcum_p, n_heads=n_heads, tq=512, tk=1024)
    zeros_state = jnp.zeros((bsz, 1, n_groups * n_state), F32)
    ssm_p, hre_p, him_p = _s5_mixer(u_p, zeros_state, zeros_state, wb, wc, pw_re, pw_im, d_row, w_glu_b,
                                    b_glu_row, g_ssm_row, tc=tc_prompt)
    x1_p, h2_p = _out_proj(att_p, ssm_p, x_prompt, mod4, row_prompt, g_att_row, w_out_b, g2, nb=1, t=tm)
    y_p = _mlp(h2_p, x1_p, mod4, row_prompt, w1, w2, nb=1, t=tm, tf=1024)

    q_s, k32_s, kbf_s, v32_s, vbf_s, logf_s, u_s = _in_proj(
        x_sample, mod4, row_sample, g1, w_qkv, w_u, w_f, gq, gk, b_f_pad,
        nb=dec_b, t=dec_s, n_heads=n_heads, s5_chunk=tc_sample)
    logf_all = jnp.concatenate([cache_logf[l], logf_s], axis=1)
    fcum_s = _forget_cumsum(logf_all, tc=(past + dec_s) // 2)
    att_s = _sample_attention(q_s, kbf_s, vbf_s, cache_k[l], cache_v[l], fcum_s, n_heads=n_heads)
    ssm_s, hre_s, him_s = _s5_mixer(u_s, state_ssm_re[l].reshape(dec_b, 1, -1),
                                    state_ssm_im[l].reshape(dec_b, 1, -1), wb, wc, pw_re, pw_im, d_row,
                                    w_glu_b, b_glu_row, g_ssm_row, tc=tc_sample)
    x1_s, h2_s = _out_proj(att_s, ssm_s, x_sample, mod4, row_sample, g_att_row, w_out_b, g2,
                           nb=dec_b, t=dec_s)
    y_s = _mlp(h2_s, x1_s, mod4, row_sample, w1, w2, nb=dec_b, t=dec_s, tf=1024)

    states = lambda a, n: a.reshape(1, n, n_groups, n_state)
    return (y_p, y_s,
            k32_p[None], v32_p[None], logf_p[None], states(hre_p, bsz), states(him_p, bsz),
            k32_s[None], v32_s[None], logf_s[None], states(hre_s, dec_b), states(him_s, dec_b))
```

```python
import functools
import math

import jax
import jax.numpy as jnp
from jax import lax
from jax.experimental import pallas as pl
from jax.experimental.pallas import tpu as pltpu

F32 = jnp.float32
BF16 = jnp.bfloat16
HIGHEST = lax.Precision.HIGHEST

HEAD_DIM = 128
SSM_GROUP = 16
N_MOD = 6
EPS = 1e-6
LOG2E = 1.4426950408889634

LANES = 128
SUBLANES = 8
GROUPS_PER_BLOCK = 16
MAX_SEG = 32
N_POW = MAX_SEG + 2
VMEM_LIMIT = 56 * 1024 * 1024


def _params(*sem):
    return pltpu.CompilerParams(dimension_semantics=sem, vmem_limit_bytes=VMEM_LIMIT)


def _resident(shape):
    nd = len(shape)
    return pl.BlockSpec(shape, lambda *_: (0,) * nd, pipeline_mode=pl.Buffered(1))


def _sigmoid(x):
    return 1.0 / (1.0 + jnp.exp(-x))


def _rms_rows(x, g):
    return x * lax.rsqrt(jnp.mean(x * x, axis=-1, keepdims=True) + EPS) * g


def _dot_nt(a, b):
    return lax.dot_general(a, b, (((1,), (1,)), ((), ())), preferred_element_type=F32)


def _cmul_add(x_re, x_im, a_re, a_im, h_re, h_im):
    return x_re + (a_re * h_re - a_im * h_im), x_im + (a_re * h_im + a_im * h_re)


def _stream_permutation(n, inverse):
    i = lax.broadcasted_iota(jnp.int32, (n, n), 0)
    j = lax.broadcasted_iota(jnp.int32, (n, n), 1)
    if inverse:
        i, j = j, i
    src = (i & (SUBLANES - 1)) * (n // SUBLANES) + (i >> 3)
    return jnp.where(j == src, 1.0, 0.0).astype(BF16)


def _mod_kernel(c_ref, w_ref, b_ref, o_ref):
    c = c_ref[...]
    s = c * _sigmoid(c)
    o_ref[...] = jnp.dot(s, w_ref[...], precision=HIGHEST, preferred_element_type=F32) + b_ref[...]


def _modulation(c_rows, w_ada, b_ada):
    rows, d = c_rows.shape
    n = w_ada.shape[1]
    tn = 512
    return pl.pallas_call(
        _mod_kernel,
        out_shape=jax.ShapeDtypeStruct((rows, n), F32),
        grid=(n // tn,),
        in_specs=[pl.BlockSpec((rows, d), lambda j: (0, 0)),
                  pl.BlockSpec((d, tn), lambda j: (0, j)),
                  pl.BlockSpec((1, tn), lambda j: (0, j))],
        out_specs=pl.BlockSpec((rows, tn), lambda j: (0, j)),
        compiler_params=_params("arbitrary"),
        name="modulation",
    )(c_rows, w_ada, b_ada)


def _s5_param_kernel(logdt_ref, are_ref, aim_ref, bre_ref, bim_ref, zbre_ref, zbim_ref, pwre_ref, pwim_ref):
    dt = jnp.exp(logdt_ref[...])
    lam_re = jnp.minimum(are_ref[...], -1e-4)
    lam_im = aim_ref[...]
    mag = jnp.exp(lam_re * dt)
    ang = lam_im * dt
    ab_re = mag * jnp.cos(ang)
    ab_im = mag * jnp.sin(ang)
    den = lam_re * lam_re + lam_im * lam_im
    z_re = ((ab_re - 1.0) * lam_re + ab_im * lam_im) / den
    z_im = (ab_im * lam_re - (ab_re - 1.0) * lam_im) / den
    b_re = bre_ref[...]
    b_im = bim_ref[...]
    zbre_ref[...] = z_re * b_re - z_im * b_im
    zbim_ref[...] = z_re * b_im + z_im * b_re
    p_re, p_im = ab_re, ab_im
    for k in range(MAX_SEG):
        pwre_ref[k] = p_re
        pwim_ref[k] = p_im
        if k + 1 < MAX_SEG:
            p_re, p_im = p_re * ab_re - p_im * ab_im, p_re * ab_im + p_im * ab_re
    for k in range(MAX_SEG, N_POW):
        p_re, p_im = p_re * p_re - p_im * p_im, 2.0 * (p_re * p_im)
        pwre_ref[k] = p_re
        pwim_ref[k] = p_im


def _s5_params(log_dt, a_re, a_im, b_re, b_im):
    g, n, c = b_re.shape
    f = lambda shape: jax.ShapeDtypeStruct(shape, F32)
    return pl.pallas_call(
        _s5_param_kernel,
        out_shape=(f((g, c, n)), f((g, c, n)), f((N_POW, g, 1, n)), f((N_POW, g, 1, n))),
        name="s5_params",
    )(log_dt.reshape(g, 1, 1), a_re.reshape(g, 1, n), a_im.reshape(g, 1, n),
      jnp.swapaxes(b_re, 1, 2), jnp.swapaxes(b_im, 1, 2))


def _pow_row(seg_rows):
    if seg_rows <= MAX_SEG:
        return seg_rows - 1
    return MAX_SEG - 1 + int(math.log2(seg_rows // MAX_SEG))


def _block_diag(w):
    nb, g, r, c = w.shape
    eye = jnp.eye(g, dtype=w.dtype)
    return (w[:, :, :, None, :] * eye[None, :, None, :, None]).reshape(nb, g * r, g * c)


def _in_proj_kernel(x_ref, sh_ref, sc_ref, g1_ref, wqkv_ref, wu_ref, wf_ref, gq_ref, gk_ref, bf_ref,
                    q_ref, k32_ref, kbf_ref, v32_ref, vbf_ref, logf_ref, u_ref,
                    *, n_heads, q_scale, s5_chunk):
    nb, t, d = x_ref.shape
    rows = nb * t
    d_att = n_heads * HEAD_DIM
    x = x_ref[...]
    y = _rms_rows(x, g1_ref[...])
    h = (y * (1.0 + sc_ref[:, 0]) + sh_ref[:, 0]).reshape(rows, d)
    hb = h.astype(BF16)

    q = jnp.dot(hb, wqkv_ref[:, 0:d_att], preferred_element_type=F32)
    for hh in range(n_heads):
        sl = slice(hh * HEAD_DIM, (hh + 1) * HEAD_DIM)
        qn = _rms_rows(q[:, sl], gq_ref[...]) * q_scale
        q_ref[:, :, sl] = qn.reshape(nb, t, HEAD_DIM).astype(BF16)

    k = jnp.dot(hb, wqkv_ref[:, d_att:2 * d_att], preferred_element_type=F32)
    for hh in range(n_heads):
        sl = slice(hh * HEAD_DIM, (hh + 1) * HEAD_DIM)
        kn = _rms_rows(k[:, sl], gk_ref[...])
        kbf_ref[:, :, sl] = kn.reshape(nb, t, HEAD_DIM).astype(BF16)
        for bi in range(nb):
            k32_ref[bi, pl.ds(hh, t, stride=n_heads), :] = kn[bi * t:(bi + 1) * t]

    v = jnp.dot(hb, wqkv_ref[:, 2 * d_att:3 * d_att], preferred_element_type=F32)
    for hh in range(n_heads):
        sl = slice(hh * HEAD_DIM, (hh + 1) * HEAD_DIM)
        for bi in range(nb):
            v32_ref[bi, pl.ds(hh, t, stride=n_heads), :] = v[bi * t:(bi + 1) * t, sl]
    vbf_ref[...] = v.reshape(nb, t, d_att).astype(BF16)

    f = jnp.dot(hb, wf_ref[...], preferred_element_type=F32) + bf_ref[...]
    logf = jnp.minimum(f, 0.0) - jnp.log1p(jnp.exp(-jnp.abs(f)))
    logf_ref[...] = logf[:, 0:n_heads].reshape(nb, t, n_heads)

    u = jnp.dot(hb, wu_ref[...], preferred_element_type=F32).astype(BF16)
    perm = _stream_permutation(s5_chunk, inverse=False)
    for c in range(rows // s5_chunk):
        uc = jnp.dot(perm, u[c * s5_chunk:(c + 1) * s5_chunk], preferred_element_type=F32).astype(BF16)
        if nb == 1:
            u_ref[0, c * s5_chunk:(c + 1) * s5_chunk, :] = uc
        else:
            per = t // s5_chunk
            u_ref[c // per, (c % per) * s5_chunk:(c % per + 1) * s5_chunk, :] = uc


def _in_proj(x, mod4, mod_row, g1, w_qkv, w_u, w_f, g_q, g_k, b_f_pad, *, nb, t, n_heads, s5_chunk):
    n_seq, seq, d = x.shape
    d_att = n_heads * HEAD_DIM
    d_ssm = w_u.shape[1]
    assert t % s5_chunk == 0
    grid = (n_seq // nb, seq // t)
    row_blk = mod_row // nb
    act = lambda width: pl.BlockSpec((nb, t, width), lambda i, j: (i, j, 0))
    act4 = pl.BlockSpec((nb, t * n_heads, HEAD_DIM), lambda i, j: (i, j, 0))
    mod = lambda m: pl.BlockSpec((nb, 1, 1, d), lambda i, j, m=m: (row_blk + i, m, 0, 0))
    sds = lambda width, dt: jax.ShapeDtypeStruct((n_seq, seq, width), dt)
    sds4 = jax.ShapeDtypeStruct((n_seq, seq * n_heads, HEAD_DIM), F32)
    kern = functools.partial(_in_proj_kernel, n_heads=n_heads, q_scale=HEAD_DIM ** -0.5 * LOG2E,
                             s5_chunk=s5_chunk)
    return pl.pallas_call(
        kern,
        out_shape=(sds(d_att, BF16), sds4, sds(d_att, BF16), sds4, sds(d_att, BF16),
                   sds(n_heads, F32), sds(d_ssm, BF16)),
        grid=grid,
        in_specs=[act(d), mod(0), mod(1), _resident(g1.shape), _resident(w_qkv.shape), _resident(w_u.shape),
                  _resident(w_f.shape), _resident(g_q.shape), _resident(g_k.shape), _resident(b_f_pad.shape)],
        out_specs=(act(d_att), act4, act(d_att), act4, act(d_att), act(n_heads), act(d_ssm)),
        compiler_params=_params("parallel", "parallel"),
        name="in_proj",
    )(x, mod4, mod4, g1, w_qkv, w_u, w_f, g_q, g_k, b_f_pad)


def _cumsum_kernel(x_ref, o_ref, *, n_chunks):
    x = x_ref[...]
    n = x.shape[0]
    li = lax.broadcasted_iota(jnp.int32, (LANES, LANES), 0)
    lj = lax.broadcasted_iota(jnp.int32, (LANES, LANES), 1)
    tri = jnp.where(li <= lj, 1.0, 0.0)
    within = jnp.dot(x, tri, precision=HIGHEST, preferred_element_type=F32)
    tot = jnp.dot(x, jnp.ones((LANES, LANES), F32), precision=HIGHEST, preferred_element_type=F32)
    r = lax.broadcasted_iota(jnp.int32, (n, n), 0)
    c = lax.broadcasted_iota(jnp.int32, (n, n), 1)
    seq_of = lambda i: jnp.floor((i.astype(F32) + 0.5) * (1.0 / n_chunks))
    earlier = jnp.where(seq_of(r) == seq_of(c), jnp.where(c < r, 1.0, 0.0), 0.0)
    before = jnp.dot(earlier, tot, precision=HIGHEST, preferred_element_type=F32)
    o_ref[...] = (within + before) * LOG2E


def _forget_cumsum(logf):
    b, length, h = logf.shape
    n_chunks = -(-length // LANES)
    rows = jnp.swapaxes(logf, 1, 2).reshape(b * h, length)
    rows = jnp.pad(rows, ((0, 0), (0, n_chunks * LANES - length)))
    out = pl.pallas_call(
        functools.partial(_cumsum_kernel, n_chunks=n_chunks),
        out_shape=jax.ShapeDtypeStruct((b * h * n_chunks, LANES), F32),
        compiler_params=_params(),
        name="forget_cumsum",
    )(rows.reshape(b * h * n_chunks, LANES))
    return jnp.swapaxes(out.reshape(b, h, n_chunks * LANES)[:, :, :length], 1, 2)


def _bias_columns(f_rows, head, key_side):
    n = f_rows.shape[0]
    lane_h = lax.broadcasted_iota(jnp.int32, f_rows.shape, 1)
    col = jnp.sum(jnp.where(lane_h == head, f_rows, 0.0), axis=-1, keepdims=True)
    hi = col.astype(BF16).astype(F32)
    rest = col - hi
    mid = rest.astype(BF16).astype(F32)
    lo = rest - mid
    lane = lax.broadcasted_iota(jnp.int32, (n, LANES), 1)
    if key_side:
        split = jnp.where(lane == 3, -hi, jnp.where(lane == 4, -mid, jnp.where(lane == 5, -lo, 0.0)))
        vals = jnp.where(lane < 3, 1.0, split)
    else:
        split = jnp.where(lane == 0, hi, jnp.where(lane == 1, mid, jnp.where(lane == 2, lo, 0.0)))
        vals = jnp.where((lane >= 3) & (lane < 6), 1.0, split)
    return vals.astype(BF16)


def _attn_kernel(q_ref, k_ref, v_ref, f_ref, o_ref, kaug_ref, vaug_ref, qa_ref, s0_ref, s1_ref, m_ref, acc_ref,
                 *, tq, tk, build_rows):
    head = pl.program_id(1)
    qi = pl.program_id(2)
    t = k_ref.shape[1]
    dh = HEAD_DIM

    @pl.when(qi == 0)
    def _build_keys():
        kaug_ref[:, 0:dh] = k_ref[0]
        vaug_ref[:, 0:dh] = v_ref[0]
        vaug_ref[:, dh:2 * dh] = jnp.ones((t, dh), BF16)

        def piece(i, carry):
            r0 = pl.multiple_of(i * build_rows, build_rows)
            kaug_ref[pl.ds(r0, build_rows), dh:2 * dh] = _bias_columns(
                f_ref[0, pl.ds(r0, build_rows), :], head, True)
            return carry

        lax.fori_loop(0, t // build_rows, piece, 0)

    q0 = pl.multiple_of(qi * tq, tq)
    qa_ref[:, 0:dh] = q_ref[0]
    qa_ref[:, dh:2 * dh] = _bias_columns(f_ref[0, pl.ds(q0, tq), :], head, False)
    m_ref[...] = jnp.full(m_ref.shape, -jnp.inf, F32)
    acc_ref[...] = jnp.zeros(acc_ref.shape, F32)

    def scores(c, s_ref):
        off = pl.multiple_of(c * tk, tk)
        s_ref[...] = _dot_nt(qa_ref[...], kaug_ref[pl.ds(off, tk), :])

    def absorb(c, s_ref, masked):
        off = pl.multiple_of(c * tk, tk)
        s = s_ref[...]
        if masked:
            ahead = (lax.broadcasted_iota(jnp.int32, s.shape, 1)
                     - lax.broadcasted_iota(jnp.int32, s.shape, 0))
            s = jnp.where(ahead <= q0 - off, s, -jnp.inf)
        m_old = m_ref[...]
        m_new = jnp.maximum(m_old, jnp.max(s, axis=-1, keepdims=True))
        alpha = jnp.exp2(m_old - m_new)
        p = jnp.exp2(s - jnp.tile(m_new, (1, tk // LANES)))
        pv = jnp.dot(p.astype(BF16), vaug_ref[pl.ds(off, tk), :], preferred_element_type=F32)
        acc_ref[...] = jnp.tile(alpha, (1, 2)) * acc_ref[...] + pv
        m_ref[...] = m_new

    n_full = 2 * qi
    scores(0, s0_ref)

    def pair(p, carry):
        c = 2 * p
        scores(c + 1, s1_ref)
        absorb(c, s0_ref, False)
        scores(c + 2, s0_ref)
        absorb(c + 1, s1_ref, False)
        return carry

    lax.fori_loop(0, qi, pair, 0)
    scores(n_full + 1, s1_ref)
    absorb(n_full, s0_ref, True)
    absorb(n_full + 1, s1_ref, True)

    acc = acc_ref[...]
    o_ref[0] = (acc[:, 0:dh] * (1.0 / acc[:, dh:2 * dh])).astype(o_ref.dtype)


def _prompt_attention(q, k, v, f_cum2, *, n_heads, tq, tk):
    b, t, _ = q.shape
    assert t % tq == 0 and tq == 2 * tk
    per_head = lambda rows: pl.BlockSpec((1, rows, HEAD_DIM), (lambda bi, h, i: (bi, i, h)) if rows == tq
                                         else (lambda bi, h, i: (bi, 0, h)))
    return pl.pallas_call(
        functools.partial(_attn_kernel, tq=tq, tk=tk, build_rows=1024),
        out_shape=jax.ShapeDtypeStruct(q.shape, BF16),
        grid=(b, n_heads, t // tq),
        in_specs=[per_head(tq), per_head(t), per_head(t),
                  pl.BlockSpec((1, t, n_heads), lambda bi, h, i: (bi, 0, 0))],
        out_specs=per_head(tq),
        scratch_shapes=[pltpu.VMEM((t, 2 * HEAD_DIM), BF16), pltpu.VMEM((t, 2 * HEAD_DIM), BF16),
                        pltpu.VMEM((tq, 2 * HEAD_DIM), BF16), pltpu.VMEM((tq, tk), F32), pltpu.VMEM((tq, tk), F32),
                        pltpu.VMEM((tq, LANES), F32), pltpu.VMEM((tq, 2 * HEAD_DIM), F32)],
        compiler_params=_params("parallel", "arbitrary", "arbitrary"),
        name="prompt_attention",
    )(q, k, v, f_cum2)


def _sample_attn_kernel(q_ref, kc_ref, vc_ref, kn_ref, vn_ref, f_ref, o_ref, *, past, n_heads):
    s_len = q_ref.shape[1]
    f_old = f_ref[0, 0:past, :]
    f_new = f_ref[0, past:past + s_len, :]
    row = lax.broadcasted_iota(jnp.int32, (s_len, s_len), 0)
    col = lax.broadcasted_iota(jnp.int32, (s_len, s_len), 1)
    for hh in range(n_heads):
        sl = slice(hh * HEAD_DIM, (hh + 1) * HEAD_DIM)
        qa = jnp.concatenate([q_ref[0, :, sl], _bias_columns(f_new, hh, False)], axis=-1)
        head_rows = pl.ds(hh, past, stride=n_heads)
        kca = jnp.concatenate([kc_ref[0, head_rows, :].astype(BF16), _bias_columns(f_old, hh, True)], axis=-1)
        kna = jnp.concatenate([kn_ref[0, :, sl], _bias_columns(f_new, hh, True)], axis=-1)
        s_c = _dot_nt(qa, kca)
        s_n = jnp.where(col <= row, _dot_nt(qa, kna), -jnp.inf)
        m = jnp.maximum(jnp.max(s_c, axis=-1, keepdims=True), jnp.max(s_n, axis=-1, keepdims=True))
        p_c = jnp.exp2(s_c - m)
        p_n = jnp.exp2(s_n - m)
        l = jnp.sum(p_c, axis=-1, keepdims=True) + jnp.sum(p_n, axis=-1, keepdims=True)
        o = (jnp.dot(p_c.astype(BF16), vc_ref[0, head_rows, :].astype(BF16), preferred_element_type=F32)
             + jnp.dot(p_n.astype(BF16), vn_ref[0, :, sl], preferred_element_type=F32))
        o_ref[0, :, sl] = (o * (1.0 / l)).astype(o_ref.dtype)


def _sample_attention(q, k_new, v_new, cache_k, cache_v, f_cum2, *, n_heads):
    b, s_len, d_att = q.shape
    past = cache_k.shape[1] // n_heads
    new = pl.BlockSpec((1, s_len, d_att), lambda bi: (bi, 0, 0))
    old = pl.BlockSpec((1, past * n_heads, HEAD_DIM), lambda bi: (bi, 0, 0))
    return pl.pallas_call(
        functools.partial(_sample_attn_kernel, past=past, n_heads=n_heads),
        out_shape=jax.ShapeDtypeStruct(q.shape, BF16),
        grid=(b,),
        in_specs=[new, old, old, new, new, pl.BlockSpec((1, past + s_len, n_heads), lambda bi: (bi, 0, 0))],
        out_specs=new,
        compiler_params=_params("parallel"),
        name="sample_attention",
    )(q, cache_k, cache_v, k_new, v_new, f_cum2)


def _s5_kernel(u_ref, h0re_ref, h0im_ref, wb_ref, wc_ref, pwre_ref, pwim_ref, d_ref, wglu_ref, bglu_ref,
               gout_ref, o_ref, hre_ref, him_ref, xre_ref, xim_ref, y_ref):
    tc = u_ref.shape[1]
    seg = tc // SUBLANES
    n_blk, ch = wb_ref.shape[0], wb_ref.shape[1]
    ns = wb_ref.shape[2] // 2

    @pl.when(pl.program_id(1) == 0)
    def _():
        hre_ref[...] = h0re_ref[...]
        him_ref[...] = h0im_ref[...]

    sub = lax.broadcasted_iota(jnp.int32, (SUBLANES, ns), 0)
    rows = lambda r: slice(r * SUBLANES, (r + 1) * SUBLANES)
    for blk in range(n_blk):
        cols = slice(blk * ns, (blk + 1) * ns)
        chs = slice(blk * ch, (blk + 1) * ch)
        ub = u_ref[0, :, chs]
        bu = jnp.dot(ub, wb_ref[blk], preferred_element_type=F32)

        a_re, a_im = pwre_ref[rows(0), cols], pwim_ref[rows(0), cols]
        h_re = bu[rows(0), 0:ns]
        h_im = bu[rows(0), ns:2 * ns]
        xre_ref[rows(0), :] = h_re
        xim_ref[rows(0), :] = h_im
        for r in range(1, seg):
            h_re, h_im = _cmul_add(bu[rows(r), 0:ns], bu[rows(r), ns:2 * ns], a_re, a_im, h_re, h_im)
            xre_ref[rows(r), :] = h_re
            xim_ref[rows(r), :] = h_im

        s_re = jnp.where(sub == 0, hre_ref[0, :, cols], pltpu.roll(h_re, 1, 0))
        s_im = jnp.where(sub == 0, him_ref[0, :, cols], pltpu.roll(h_im, 1, 0))
        for shift in (1, 2, 4):
            row = _pow_row(seg * shift)
            m_re = jnp.where(sub >= shift, pwre_ref[rows(row), cols], 0.0)
            m_im = jnp.where(sub >= shift, pwim_ref[rows(row), cols], 0.0)
            s_re, s_im = _cmul_add(s_re, s_im, m_re, m_im, pltpu.roll(s_re, shift, 0), pltpu.roll(s_im, shift, 0))

        for r in range(seg):
            t_re, t_im = _cmul_add(xre_ref[rows(r), :], xim_ref[rows(r), :],
                                   pwre_ref[rows(r), cols], pwim_ref[rows(r), cols], s_re, s_im)
            xre_ref[rows(r), :] = t_re
            xim_ref[rows(r), :] = t_im
        hre_ref[0, :, cols] = t_re[SUBLANES - 1:SUBLANES, :]
        him_ref[0, :, cols] = t_im[SUBLANES - 1:SUBLANES, :]

        y = (jnp.dot(xre_ref[...].astype(BF16), wc_ref[blk, 0:ns, :], preferred_element_type=F32)
             + jnp.dot(xim_ref[...].astype(BF16), wc_ref[blk, ns:2 * ns, :], preferred_element_type=F32))
        y_ref[:, chs] = y + d_ref[:, chs] * ub.astype(F32)

    y = y_ref[...]
    g = y * (0.5 * (1.0 + jnp.tanh(math.sqrt(2.0 / math.pi) * (y + 0.044715 * (y * y * y)))))
    gate = _sigmoid(jnp.dot(g.astype(BF16), wglu_ref[...], preferred_element_type=F32) + bglu_ref[...])
    out = _rms_rows(g * gate, gout_ref[...]).astype(BF16)
    o_ref[0] = jnp.dot(_stream_permutation(tc, inverse=True), out, preferred_element_type=F32).astype(o_ref.dtype)


def _s5_mixer(u, h0_re, h0_im, wb, wc, pw_re, pw_im, d_skip, w_glu, b_glu, g_out, *, tc):
    b, t, d_ssm = u.shape
    n_state = h0_re.shape[-1]
    assert tc // SUBLANES <= MAX_SEG
    state = pl.BlockSpec((1, 1, n_state), lambda bi, j: (bi, 0, 0))
    act = pl.BlockSpec((1, tc, d_ssm), lambda bi, j: (bi, j, 0))
    ns = wb.shape[2] // 2
    return pl.pallas_call(
        _s5_kernel,
        out_shape=(jax.ShapeDtypeStruct((b, t, d_ssm), BF16),
                   jax.ShapeDtypeStruct((b, 1, n_state), F32), jax.ShapeDtypeStruct((b, 1, n_state), F32)),
        grid=(b, t // tc),
        in_specs=[act, state, state, _resident(wb.shape), _resident(wc.shape), _resident(pw_re.shape),
                  _resident(pw_im.shape), _resident(d_skip.shape), _resident(w_glu.shape),
                  _resident(b_glu.shape), _resident(g_out.shape)],
        out_specs=(act, state, state),
        scratch_shapes=[pltpu.VMEM((tc, ns), F32), pltpu.VMEM((tc, ns), F32), pltpu.VMEM((tc, d_ssm), F32)],
        compiler_params=_params("parallel", "arbitrary"),
        name="s5_mixer",
    )(u, h0_re, h0_im, wb, wc, pw_re, pw_im, d_skip, w_glu, b_glu, g_out)


def _out_proj_kernel(att_ref, ssm_ref, x_ref, gt1_ref, sh2_ref, sc2_ref, gatt_ref, wout_ref, g2_ref,
                     x1_ref, h2_ref):
    nb, t, d = x_ref.shape
    rows = nb * t
    d_att = att_ref.shape[-1]
    a = att_ref[...].astype(F32).reshape(rows, d_att)
    an = _rms_rows(a, gatt_ref[...]).astype(BF16)
    mix = (jnp.dot(an, wout_ref[0:d_att, :], preferred_element_type=F32)
           + jnp.dot(ssm_ref[...].reshape(rows, ssm_ref.shape[-1]), wout_ref[d_att:, :],
                     preferred_element_type=F32))
    x1 = x_ref[...] + gt1_ref[:, 0] * mix.reshape(nb, t, d)
    x1_ref[...] = x1
    h2 = _rms_rows(x1, g2_ref[...]) * (1.0 + sc2_ref[:, 0]) + sh2_ref[:, 0]
    h2_ref[...] = h2.astype(BF16)


def _out_proj(att, ssm, x, mod4, mod_row, g_att, w_out, g2, *, nb, t):
    n_seq, seq, d = x.shape
    row_blk = mod_row // nb
    act = lambda width: pl.BlockSpec((nb, t, width), lambda i, j: (i, j, 0))
    mod = lambda m: pl.BlockSpec((nb, 1, 1, d), lambda i, j, m=m: (row_blk + i, m, 0, 0))
    return pl.pallas_call(
        _out_proj_kernel,
        out_shape=(jax.ShapeDtypeStruct(x.shape, F32), jax.ShapeDtypeStruct(x.shape, BF16)),
        grid=(n_seq // nb, seq // t),
        in_specs=[act(att.shape[-1]), act(ssm.shape[-1]), act(d), mod(2), mod(3), mod(4),
                  _resident(g_att.shape), _resident(w_out.shape), _resident(g2.shape)],
        out_specs=(act(d), act(d)),
        compiler_params=_params("parallel", "parallel"),
        name="out_proj",
    )(att, ssm, x, mod4, mod4, mod4, g_att, w_out, g2)


def _mlp_kernel(h2_ref, x1_ref, gt2_ref, w1_ref, w2_ref, o_ref, acc_ref):
    nb, t, d = x1_ref.shape
    j = pl.program_id(2)

    @pl.when(j == 0)
    def _():
        acc_ref[...] = jnp.zeros(acc_ref.shape, F32)

    a = jnp.dot(h2_ref[...].reshape(nb * t, d), w1_ref[...], preferred_element_type=F32)
    r = jnp.maximum(a, 0.0)
    acc_ref[...] += jnp.dot((r * r).astype(BF16), w2_ref[...], preferred_element_type=F32)

    @pl.when(j == pl.num_programs(2) - 1)
    def _():
        o_ref[...] = x1_ref[...] + gt2_ref[:, 0] * acc_ref[...].reshape(nb, t, d)


def _mlp(h2, x1, mod4, mod_row, w1, w2, *, nb, t, tf):
    n_seq, seq, d = x1.shape
    d_ff = w1.shape[1]
    row_blk = mod_row // nb
    act = pl.BlockSpec((nb, t, d), lambda i, j, f: (i, j, 0))
    return pl.pallas_call(
        _mlp_kernel,
        out_shape=jax.ShapeDtypeStruct(x1.shape, F32),
        grid=(n_seq // nb, seq // t, d_ff // tf),
        in_specs=[act, act,
                  pl.BlockSpec((nb, 1, 1, d), lambda i, j, f: (row_blk + i, 5, 0, 0)),
                  pl.BlockSpec((d, tf), lambda i, j, f: (0, f)),
                  pl.BlockSpec((tf, d), lambda i, j, f: (f, 0))],
        out_specs=act,
        scratch_shapes=[pltpu.VMEM((nb * t, d), F32)],
        compiler_params=_params("parallel", "parallel", "arbitrary"),
        name="mlp",
    )(h2, x1, mod4, w1, w2)


def kernel(x_prompt, x_sample, c_prompt, c_sample, cache_k, cache_v, cache_logf, state_ssm_re, state_ssm_im,
           w_ada, b_ada, g_norm1, w_in, g_q, g_k, b_f, log_dt, a_re, a_im, b_re, b_im, c_re, c_im, d_skip,
           w_glu, b_glu, g_att_out, g_ssm_out, w_out, g_norm2, w_ff1, w_ff2):
    depth = w_ada.shape[0]
    assert depth == 1, "single-layer step"
    bsz, seq, d = x_prompt.shape
    dec_b, dec_s, _ = x_sample.shape
    past = cache_k.shape[2]
    n_heads = cache_k.shape[3]
    d_att = n_heads * HEAD_DIM
    n_groups, n_state = a_re.shape[1], a_re.shape[2]
    n_blk = n_groups // GROUPS_PER_BLOCK
    mod_rows = 16
    assert dec_b + bsz <= mod_rows
    l = 0

    c_rows = jnp.concatenate([c_sample, c_prompt, jnp.zeros((mod_rows - dec_b - bsz, d), F32)], axis=0)
    mod4 = _modulation(c_rows, w_ada[l], b_ada[l][None]).reshape(mod_rows, N_MOD, 1, d)
    row_sample, row_prompt = 0, dec_b

    w = w_in[l]
    w_qkv = w[:, 0:3 * d_att].astype(BF16)
    w_f = jnp.pad(w[:, 3 * d_att:3 * d_att + n_heads], ((0, 0), (0, LANES - n_heads))).astype(BF16)
    w_u = w[:, 3 * d_att + n_heads:].astype(BF16)
    b_f_pad = jnp.pad(b_f[l], (0, LANES - n_heads))[None]
    g1 = g_norm1[l][None]
    gq, gk = g_q[l][None], g_k[l][None]

    zb_re, zb_im, pw_re, pw_im = _s5_params(log_dt[l], a_re[l], a_im[l], b_re[l], b_im[l])
    pw_re = jnp.repeat(pw_re.reshape(N_POW, n_groups * n_state), SUBLANES, axis=0)
    pw_im = jnp.repeat(pw_im.reshape(N_POW, n_groups * n_state), SUBLANES, axis=0)
    blk4 = lambda m: m.reshape(n_blk, GROUPS_PER_BLOCK, m.shape[1], m.shape[2])
    wb = jnp.concatenate([_block_diag(blk4(zb_re)), _block_diag(blk4(zb_im))], axis=-1).astype(BF16)
    ct_re = jnp.swapaxes(c_re[l], 1, 2)
    ct_im = jnp.swapaxes(c_im[l], 1, 2)
    wc = jnp.concatenate([_block_diag(blk4(ct_re)), _block_diag(blk4(-ct_im))], axis=1).astype(BF16)
    d_row = d_skip[l][None]
    w_glu_b = w_glu[l].astype(BF16)
    b_glu_row = b_glu[l][None]
    g_ssm_row = g_ssm_out[l][None]
    g_att_row = g_att_out[l][None]
    w_out_b = w_out[l].astype(BF16)
    g2 = g_norm2[l][None]
    w1 = w_ff1[l].astype(BF16)
    w2 = w_ff2[l].astype(BF16)

    tm = 512
    tc_prompt = SUBLANES * MAX_SEG
    tc_sample = dec_s

    q_p, k32_p, kbf_p, v32_p, vbf_p, logf_p, u_p = _in_proj(
        x_prompt, mod4, row_prompt, g1, w_qkv, w_u, w_f, gq, gk, b_f_pad,
        nb=1, t=tm, n_heads=n_heads, s5_chunk=tc_prompt)
    fcum_p = _forget_cumsum(logf_p)
    att_p = _prompt_attention(q_p, kbf_p, vbf_p, fcum_p, n_heads=n_heads, tq=1024, tk=512)
    zeros_state = jnp.zeros((bsz, 1, n_groups * n_state), F32)
    ssm_p, hre_p, him_p = _s5_mixer(u_p, zeros_state, zeros_state, wb, wc, pw_re, pw_im, d_row, w_glu_b,
                                    b_glu_row, g_ssm_row, tc=tc_prompt)
    x1_p, h2_p = _out_proj(att_p, ssm_p, x_prompt, mod4, row_prompt, g_att_row, w_out_b, g2, nb=1, t=tm)
    y_p = _mlp(h2_p, x1_p, mod4, row_prompt, w1, w2, nb=1, t=tm, tf=1024)

    q_s, k32_s, kbf_s, v32_s, vbf_s, logf_s, u_s = _in_proj(
        x_sample, mod4, row_sample, g1, w_qkv, w_u, w_f, gq, gk, b_f_pad,
        nb=dec_b, t=dec_s, n_heads=n_heads, s5_chunk=tc_sample)
    logf_all = jnp.concatenate([cache_logf[l], logf_s], axis=1)
    fcum_s = _forget_cumsum(logf_all)
    att_s = _sample_attention(q_s, kbf_s, vbf_s, cache_k[l].reshape(dec_b, past * n_heads, HEAD_DIM),
                              cache_v[l].reshape(dec_b, past * n_heads, HEAD_DIM), fcum_s, n_heads=n_heads)
    ssm_s, hre_s, him_s = _s5_mixer(u_s, state_ssm_re[l].reshape(dec_b, 1, -1),
                                    state_ssm_im[l].reshape(dec_b, 1, -1), wb, wc, pw_re, pw_im, d_row,
                                    w_glu_b, b_glu_row, g_ssm_row, tc=tc_sample)
    x1_s, h2_s = _out_proj(att_s, ssm_s, x_sample, mod4, row_sample, g_att_row, w_out_b, g2,
                           nb=dec_b, t=dec_s)
    y_s = _mlp(h2_s, x1_s, mod4, row_sample, w1, w2, nb=dec_b, t=dec_s, tf=1024)

    states = lambda a, n: a.reshape(1, n, n_groups, n_state)
    heads = lambda a, n, s: a.reshape(1, n, s, n_heads, HEAD_DIM)
    return (y_p, y_s,
            heads(k32_p, bsz, seq), heads(v32_p, bsz, seq), logf_p[None],
            states(hre_p, bsz), states(him_p, bsz),
            heads(k32_s, dec_b, dec_s), heads(v32_s, dec_b, dec_s), logf_s[None],
            states(hre_s, dec_b), states(him_s, dec_b))
```

```python
import functools
import math

import jax
import jax.numpy as jnp
from jax import lax
from jax.experimental import pallas as pl
from jax.experimental.pallas import tpu as pltpu

F32 = jnp.float32
BF16 = jnp.bfloat16
HIGHEST = lax.Precision.HIGHEST

HEAD_DIM = 128
SSM_GROUP = 16
N_MOD = 6
EPS = 1e-6
LOG2E = 1.4426950408889634

LANES = 128
SUBLANES = 8
GROUPS_PER_BLOCK = 16
MAX_SEG = 32
N_POW = MAX_SEG + 2
VMEM_LIMIT = 56 * 1024 * 1024


def _params(*sem):
    return pltpu.CompilerParams(dimension_semantics=sem, vmem_limit_bytes=VMEM_LIMIT)


def _resident(shape):
    nd = len(shape)
    return pl.BlockSpec(shape, lambda *_: (0,) * nd, pipeline_mode=pl.Buffered(1))


def _sigmoid(x):
    return 1.0 / (1.0 + jnp.exp(-x))


def _rms_rows(x, g):
    return x * lax.rsqrt(jnp.mean(x * x, axis=-1, keepdims=True) + EPS) * g


def _dot_nt(a, b):
    return lax.dot_general(a, b, (((1,), (1,)), ((), ())), preferred_element_type=F32)


def _cmul_add(x_re, x_im, a_re, a_im, h_re, h_im):
    return x_re + (a_re * h_re - a_im * h_im), x_im + (a_re * h_im + a_im * h_re)


def _stream_permutation(n, inverse):
    i = lax.broadcasted_iota(jnp.int32, (n, n), 0)
    j = lax.broadcasted_iota(jnp.int32, (n, n), 1)
    if inverse:
        i, j = j, i
    src = (i & (SUBLANES - 1)) * (n // SUBLANES) + (i >> 3)
    return jnp.where(j == src, 1.0, 0.0).astype(BF16)


def _mod_kernel(c_ref, w_ref, b_ref, o_ref):
    c = c_ref[...]
    s = c * _sigmoid(c)
    o_ref[...] = jnp.dot(s, w_ref[...], precision=HIGHEST, preferred_element_type=F32) + b_ref[...]


def _modulation(c_rows, w_ada, b_ada):
    rows, d = c_rows.shape
    n = w_ada.shape[1]
    tn = 512
    return pl.pallas_call(
        _mod_kernel,
        out_shape=jax.ShapeDtypeStruct((rows, n), F32),
        grid=(n // tn,),
        in_specs=[pl.BlockSpec((rows, d), lambda j: (0, 0)),
                  pl.BlockSpec((d, tn), lambda j: (0, j)),
                  pl.BlockSpec((1, tn), lambda j: (0, j))],
        out_specs=pl.BlockSpec((rows, tn), lambda j: (0, j)),
        compiler_params=_params("arbitrary"),
        name="modulation",
    )(c_rows, w_ada, b_ada)


def _s5_param_kernel(logdt_ref, are_ref, aim_ref, bre_ref, bim_ref, zbre_ref, zbim_ref, pwre_ref, pwim_ref):
    dt = jnp.exp(logdt_ref[...])
    lam_re = jnp.minimum(are_ref[...], -1e-4)
    lam_im = aim_ref[...]
    mag = jnp.exp(lam_re * dt)
    ang = lam_im * dt
    ab_re = mag * jnp.cos(ang)
    ab_im = mag * jnp.sin(ang)
    den = lam_re * lam_re + lam_im * lam_im
    z_re = ((ab_re - 1.0) * lam_re + ab_im * lam_im) / den
    z_im = (ab_im * lam_re - (ab_re - 1.0) * lam_im) / den
    b_re = bre_ref[...]
    b_im = bim_ref[...]
    zbre_ref[...] = z_re * b_re - z_im * b_im
    zbim_ref[...] = z_re * b_im + z_im * b_re
    p_re, p_im = ab_re, ab_im
    for k in range(MAX_SEG):
        pwre_ref[k] = p_re
        pwim_ref[k] = p_im
        if k + 1 < MAX_SEG:
            p_re, p_im = p_re * ab_re - p_im * ab_im, p_re * ab_im + p_im * ab_re
    for k in range(MAX_SEG, N_POW):
        p_re, p_im = p_re * p_re - p_im * p_im, 2.0 * (p_re * p_im)
        pwre_ref[k] = p_re
        pwim_ref[k] = p_im


def _s5_params(log_dt, a_re, a_im, b_re, b_im):
    g, n, c = b_re.shape
    f = lambda shape: jax.ShapeDtypeStruct(shape, F32)
    return pl.pallas_call(
        _s5_param_kernel,
        out_shape=(f((g, c, n)), f((g, c, n)), f((N_POW, g, 1, n)), f((N_POW, g, 1, n))),
        name="s5_params",
    )(log_dt.reshape(g, 1, 1), a_re.reshape(g, 1, n), a_im.reshape(g, 1, n),
      jnp.swapaxes(b_re, 1, 2), jnp.swapaxes(b_im, 1, 2))


def _pow_row(seg_rows):
    if seg_rows <= MAX_SEG:
        return seg_rows - 1
    return MAX_SEG - 1 + int(math.log2(seg_rows // MAX_SEG))


def _block_diag(w):
    nb, g, r, c = w.shape
    eye = jnp.eye(g, dtype=w.dtype)
    return (w[:, :, :, None, :] * eye[None, :, None, :, None]).reshape(nb, g * r, g * c)


def _in_proj_kernel(x_ref, sh_ref, sc_ref, g1_ref, wqkv_ref, wu_ref, wf_ref, gq_ref, gk_ref, bf_ref,
                    q_ref, k32_ref, kbf_ref, v32_ref, vbf_ref, logf_ref, u_ref,
                    *, n_heads, q_scale, s5_chunk):
    nb, t, d = x_ref.shape
    rows = nb * t
    d_att = n_heads * HEAD_DIM
    x = x_ref[...]
    y = _rms_rows(x, g1_ref[...])
    h = (y * (1.0 + sc_ref[:, 0]) + sh_ref[:, 0]).reshape(rows, d)
    hb = h.astype(BF16)

    q = jnp.dot(hb, wqkv_ref[:, 0:d_att], preferred_element_type=F32)
    for hh in range(n_heads):
        sl = slice(hh * HEAD_DIM, (hh + 1) * HEAD_DIM)
        qn = _rms_rows(q[:, sl], gq_ref[...]) * q_scale
        q_ref[:, :, sl] = qn.reshape(nb, t, HEAD_DIM).astype(BF16)

    k = jnp.dot(hb, wqkv_ref[:, d_att:2 * d_att], preferred_element_type=F32)
    for hh in range(n_heads):
        sl = slice(hh * HEAD_DIM, (hh + 1) * HEAD_DIM)
        kn = _rms_rows(k[:, sl], gk_ref[...])
        kbf_ref[:, :, sl] = kn.reshape(nb, t, HEAD_DIM).astype(BF16)
        for bi in range(nb):
            k32_ref[bi, pl.ds(hh, t, stride=n_heads), :] = kn[bi * t:(bi + 1) * t]

    v = jnp.dot(hb, wqkv_ref[:, 2 * d_att:3 * d_att], preferred_element_type=F32)
    for hh in range(n_heads):
        sl = slice(hh * HEAD_DIM, (hh + 1) * HEAD_DIM)
        for bi in range(nb):
            v32_ref[bi, pl.ds(hh, t, stride=n_heads), :] = v[bi * t:(bi + 1) * t, sl]
    vbf_ref[...] = v.reshape(nb, t, d_att).astype(BF16)

    f = jnp.dot(hb, wf_ref[...], preferred_element_type=F32) + bf_ref[...]
    logf = jnp.minimum(f, 0.0) - jnp.log1p(jnp.exp(-jnp.abs(f)))
    logf_ref[...] = logf[:, 0:n_heads].reshape(nb, t, n_heads)

    u = jnp.dot(hb, wu_ref[...], preferred_element_type=F32).astype(BF16)
    perm = _stream_permutation(s5_chunk, inverse=False)
    for c in range(rows // s5_chunk):
        uc = jnp.dot(perm, u[c * s5_chunk:(c + 1) * s5_chunk], preferred_element_type=F32).astype(BF16)
        if nb == 1:
            u_ref[0, c * s5_chunk:(c + 1) * s5_chunk, :] = uc
        else:
            per = t // s5_chunk
            u_ref[c // per, (c % per) * s5_chunk:(c % per + 1) * s5_chunk, :] = uc


def _in_proj(x, mod4, mod_row, g1, w_qkv, w_u, w_f, g_q, g_k, b_f_pad, *, nb, t, n_heads, s5_chunk):
    n_seq, seq, d = x.shape
    d_att = n_heads * HEAD_DIM
    d_ssm = w_u.shape[1]
    assert t % s5_chunk == 0
    grid = (n_seq // nb, seq // t)
    row_blk = mod_row // nb
    act = lambda width: pl.BlockSpec((nb, t, width), lambda i, j: (i, j, 0))
    act4 = pl.BlockSpec((nb, t * n_heads, HEAD_DIM), lambda i, j: (i, j, 0))
    mod = lambda m: pl.BlockSpec((nb, 1, 1, d), lambda i, j, m=m: (row_blk + i, m, 0, 0))
    sds = lambda width, dt: jax.ShapeDtypeStruct((n_seq, seq, width), dt)
    sds4 = jax.ShapeDtypeStruct((n_seq, seq * n_heads, HEAD_DIM), F32)
    kern = functools.partial(_in_proj_kernel, n_heads=n_heads, q_scale=HEAD_DIM ** -0.5 * LOG2E,
                             s5_chunk=s5_chunk)
    return pl.pallas_call(
        kern,
        out_shape=(sds(d_att, BF16), sds4, sds(d_att, BF16), sds4, sds(d_att, BF16),
                   sds(n_heads, F32), sds(d_ssm, BF16)),
        grid=grid,
        in_specs=[act(d), mod(0), mod(1), _resident(g1.shape), _resident(w_qkv.shape), _resident(w_u.shape),
                  _resident(w_f.shape), _resident(g_q.shape), _resident(g_k.shape), _resident(b_f_pad.shape)],
        out_specs=(act(d_att), act4, act(d_att), act4, act(d_att), act(n_heads), act(d_ssm)),
        compiler_params=_params("parallel", "parallel"),
        name="in_proj",
    )(x, mod4, mod4, g1, w_qkv, w_u, w_f, g_q, g_k, b_f_pad)


def _cumsum_kernel(x_ref, o_ref, *, n_chunks):
    x = x_ref[...]
    n = x.shape[0]
    li = lax.broadcasted_iota(jnp.int32, (LANES, LANES), 0)
    lj = lax.broadcasted_iota(jnp.int32, (LANES, LANES), 1)
    tri = jnp.where(li <= lj, 1.0, 0.0)
    within = jnp.dot(x, tri, precision=HIGHEST, preferred_element_type=F32)
    tot = jnp.dot(x, jnp.ones((LANES, LANES), F32), precision=HIGHEST, preferred_element_type=F32)
    r = lax.broadcasted_iota(jnp.int32, (n, n), 0)
    c = lax.broadcasted_iota(jnp.int32, (n, n), 1)
    seq_of = lambda i: jnp.floor((i.astype(F32) + 0.5) * (1.0 / n_chunks))
    earlier = jnp.where(seq_of(r) == seq_of(c), jnp.where(c < r, 1.0, 0.0), 0.0)
    before = jnp.dot(earlier, tot, precision=HIGHEST, preferred_element_type=F32)
    o_ref[...] = (within + before) * LOG2E


def _forget_cumsum(logf):
    b, length, h = logf.shape
    n_chunks = -(-length // LANES)
    rows = jnp.swapaxes(logf, 1, 2).reshape(b * h, length)
    rows = jnp.pad(rows, ((0, 0), (0, n_chunks * LANES - length)))
    out = pl.pallas_call(
        functools.partial(_cumsum_kernel, n_chunks=n_chunks),
        out_shape=jax.ShapeDtypeStruct((b * h * n_chunks, LANES), F32),
        compiler_params=_params(),
        name="forget_cumsum",
    )(rows.reshape(b * h * n_chunks, LANES))
    return jnp.swapaxes(out.reshape(b, h, n_chunks * LANES)[:, :, :length], 1, 2)


def _bias_columns(f_rows, head, key_side):
    n = f_rows.shape[0]
    lane_h = lax.broadcasted_iota(jnp.int32, f_rows.shape, 1)
    col = jnp.sum(jnp.where(lane_h == head, f_rows, 0.0), axis=-1, keepdims=True)
    hi = col.astype(BF16).astype(F32)
    rest = col - hi
    mid = rest.astype(BF16).astype(F32)
    lo = rest - mid
    lane = lax.broadcasted_iota(jnp.int32, (n, LANES), 1)
    if key_side:
        split = jnp.where(lane == 3, -hi, jnp.where(lane == 4, -mid, jnp.where(lane == 5, -lo, 0.0)))
        vals = jnp.where(lane < 3, 1.0, split)
    else:
        split = jnp.where(lane == 0, hi, jnp.where(lane == 1, mid, jnp.where(lane == 2, lo, 0.0)))
        vals = jnp.where((lane >= 3) & (lane < 6), 1.0, split)
    return vals.astype(BF16)


def _attn_kernel(q_ref, k_ref, v_ref, f_ref, o_ref, kaug_ref, vaug_ref, qa_ref, s0_ref, s1_ref, m_ref, acc_ref,
                 *, tq, tk, build_rows, heads_per_step):
    qi = pl.program_id(2)
    t = k_ref.shape[1]
    dh = HEAD_DIM
    hps = heads_per_step
    head0 = pl.program_id(1) * hps

    @pl.when(qi == 0)
    def _build_keys():
        for g in range(hps):
            kaug_ref[g, :, 0:dh] = k_ref[0, :, g * dh:(g + 1) * dh]
            vaug_ref[g, :, 0:dh] = v_ref[0, :, g * dh:(g + 1) * dh]
            vaug_ref[g, :, dh:2 * dh] = jnp.ones((t, dh), BF16)

        def piece(i, carry):
            r0 = pl.multiple_of(i * build_rows, build_rows)
            f_rows = f_ref[0, pl.ds(r0, build_rows), :]
            for g in range(hps):
                kaug_ref[g, pl.ds(r0, build_rows), dh:2 * dh] = _bias_columns(f_rows, head0 + g, True)
            return carry

        lax.fori_loop(0, t // build_rows, piece, 0)

    q0 = pl.multiple_of(qi * tq, tq)
    fq_rows = f_ref[0, pl.ds(q0, tq), :]
    for g in range(hps):
        qa_ref[g, :, 0:dh] = q_ref[0, :, g * dh:(g + 1) * dh]
        qa_ref[g, :, dh:2 * dh] = _bias_columns(fq_rows, head0 + g, False)
    m_ref[...] = jnp.full(m_ref.shape, -jnp.inf, F32)
    acc_ref[...] = jnp.zeros(acc_ref.shape, F32)

    def scores(c, s_ref, r0=0):
        off = pl.multiple_of(c * tk, tk)
        for g in range(hps):
            s_ref[g, r0:, :] = _dot_nt(qa_ref[g, r0:, :], kaug_ref[g, pl.ds(off, tk), :])

    def absorb(c, s_ref, masked, r0=0):
        off = pl.multiple_of(c * tk, tk)
        for g in range(hps):
            s = s_ref[g, r0:, :]
            if masked:
                ahead = (lax.broadcasted_iota(jnp.int32, s.shape, 1)
                         - lax.broadcasted_iota(jnp.int32, s.shape, 0))
                s = jnp.where(ahead <= q0 + r0 - off, s, -jnp.inf)
            m_old = m_ref[g, r0:, :]
            m_new = jnp.maximum(m_old, jnp.max(s, axis=-1, keepdims=True))
            alpha = jnp.exp2(m_old - m_new)
            p = jnp.exp2(s - jnp.tile(m_new, (1, tk // LANES)))
            pv = jnp.dot(p.astype(BF16), vaug_ref[g, pl.ds(off, tk), :], preferred_element_type=F32)
            acc_ref[g, r0:, :] = jnp.tile(alpha, (1, 2)) * acc_ref[g, r0:, :] + pv
            m_ref[g, r0:, :] = m_new

    n_full = 2 * qi
    scores(0, s0_ref)

    def pair(p, carry):
        c = 2 * p
        scores(c + 1, s1_ref)
        absorb(c, s0_ref, False)
        scores(c + 2, s0_ref)
        absorb(c + 1, s1_ref, False)
        return carry

    lax.fori_loop(0, qi, pair, 0)
    scores(n_full + 1, s1_ref, r0=tk)
    absorb(n_full, s0_ref, True)
    absorb(n_full + 1, s1_ref, True, r0=tk)

    for g in range(hps):
        acc = acc_ref[g]
        o_ref[0, :, g * dh:(g + 1) * dh] = (acc[:, 0:dh] * (1.0 / acc[:, dh:2 * dh])).astype(o_ref.dtype)


def _prompt_attention(q, k, v, f_cum2, *, n_heads, tq, tk, heads_per_step):
    b, t, _ = q.shape
    assert t % tq == 0 and tq == 2 * tk and n_heads % heads_per_step == 0
    hps = heads_per_step
    width = hps * HEAD_DIM
    q_blk = pl.BlockSpec((1, tq, width), lambda bi, h, i: (bi, i, h))
    kv_blk = pl.BlockSpec((1, t, width), lambda bi, h, i: (bi, 0, h))
    f_blk = pl.BlockSpec((1, t, n_heads), lambda bi, h, i: (bi, 0, 0), pipeline_mode=pl.Buffered(1))
    return pl.pallas_call(
        functools.partial(_attn_kernel, tq=tq, tk=tk, build_rows=1024, heads_per_step=hps),
        out_shape=jax.ShapeDtypeStruct(q.shape, BF16),
        grid=(b, n_heads // hps, t // tq),
        in_specs=[q_blk, kv_blk, kv_blk, f_blk],
        out_specs=q_blk,
        scratch_shapes=[pltpu.VMEM((hps, t, 2 * HEAD_DIM), BF16), pltpu.VMEM((hps, t, 2 * HEAD_DIM), BF16),
                        pltpu.VMEM((hps, tq, 2 * HEAD_DIM), BF16),
                        pltpu.VMEM((hps, tq, tk), F32), pltpu.VMEM((hps, tq, tk), F32),
                        pltpu.VMEM((hps, tq, LANES), F32), pltpu.VMEM((hps, tq, 2 * HEAD_DIM), F32)],
        compiler_params=_params("parallel", "arbitrary", "arbitrary"),
        name="prompt_attention",
    )(q, k, v, f_cum2)


def _sample_attn_kernel(q_ref, kc_ref, vc_ref, kn_ref, vn_ref, f_ref, o_ref, *, past, n_heads):
    s_len = q_ref.shape[1]
    f_old = f_ref[0, 0:past, :]
    f_new = f_ref[0, past:past + s_len, :]
    row = lax.broadcasted_iota(jnp.int32, (s_len, s_len), 0)
    col = lax.broadcasted_iota(jnp.int32, (s_len, s_len), 1)
    for hh in range(n_heads):
        sl = slice(hh * HEAD_DIM, (hh + 1) * HEAD_DIM)
        qa = jnp.concatenate([q_ref[0, :, sl], _bias_columns(f_new, hh, False)], axis=-1)
        head_rows = pl.ds(hh, past, stride=n_heads)
        kca = jnp.concatenate([kc_ref[0, head_rows, :].astype(BF16), _bias_columns(f_old, hh, True)], axis=-1)
        kna = jnp.concatenate([kn_ref[0, :, sl], _bias_columns(f_new, hh, True)], axis=-1)
        s_c = _dot_nt(qa, kca)
        s_n = jnp.where(col <= row, _dot_nt(qa, kna), -jnp.inf)
        m = jnp.maximum(jnp.max(s_c, axis=-1, keepdims=True), jnp.max(s_n, axis=-1, keepdims=True))
        p_c = jnp.exp2(s_c - m)
        p_n = jnp.exp2(s_n - m)
        l = jnp.sum(p_c, axis=-1, keepdims=True) + jnp.sum(p_n, axis=-1, keepdims=True)
        o = (jnp.dot(p_c.astype(BF16), vc_ref[0, head_rows, :].astype(BF16), preferred_element_type=F32)
             + jnp.dot(p_n.astype(BF16), vn_ref[0, :, sl], preferred_element_type=F32))
        o_ref[0, :, sl] = (o * (1.0 / l)).astype(o_ref.dtype)


def _sample_attention(q, k_new, v_new, cache_k, cache_v, f_cum2, *, n_heads):
    b, s_len, d_att = q.shape
    past = cache_k.shape[1] // n_heads
    new = pl.BlockSpec((1, s_len, d_att), lambda bi: (bi, 0, 0))
    old = pl.BlockSpec((1, past * n_heads, HEAD_DIM), lambda bi: (bi, 0, 0))
    return pl.pallas_call(
        functools.partial(_sample_attn_kernel, past=past, n_heads=n_heads),
        out_shape=jax.ShapeDtypeStruct(q.shape, BF16),
        grid=(b,),
        in_specs=[new, old, old, new, new, pl.BlockSpec((1, past + s_len, n_heads), lambda bi: (bi, 0, 0))],
        out_specs=new,
        compiler_params=_params("parallel"),
        name="sample_attention",
    )(q, cache_k, cache_v, k_new, v_new, f_cum2)


def _s5_kernel(u_ref, h0re_ref, h0im_ref, wb_ref, wc_ref, pwre_ref, pwim_ref, d_ref, wglu_ref, bglu_ref,
               gout_ref, o_ref, hre_ref, him_ref, bu_ref, xre_ref, xim_ref, y_ref, *, tc):
    n_rows = u_ref.shape[1]
    n_chunks = n_rows // tc
    seg = tc // SUBLANES
    n_blk, ch = wb_ref.shape[0], wb_ref.shape[1]
    ns = wb_ref.shape[2] // 2

    @pl.when(pl.program_id(1) == 0)
    def _():
        hre_ref[...] = h0re_ref[...]
        him_ref[...] = h0im_ref[...]

    def b_proj(blk):
        bu_ref[blk % 2] = jnp.dot(u_ref[0, :, blk * ch:(blk + 1) * ch], wb_ref[blk], preferred_element_type=F32)

    sub = lax.broadcasted_iota(jnp.int32, (SUBLANES, ns), 0)
    tab = lambda r: slice(r * SUBLANES, (r + 1) * SUBLANES)
    b_proj(0)
    for blk in range(n_blk):
        if blk + 1 < n_blk:
            b_proj(blk + 1)
        buf = blk % 2
        cols = slice(blk * ns, (blk + 1) * ns)
        chs = slice(blk * ch, (blk + 1) * ch)
        for c in range(n_chunks):
            _s5_scan_chunk(c, tc, seg, ns, buf, cols, sub, tab, bu_ref, xre_ref, xim_ref, pwre_ref, pwim_ref,
                           hre_ref, him_ref)
        y = (jnp.dot(xre_ref[...].astype(BF16), wc_ref[blk, 0:ns, :], preferred_element_type=F32)
             + jnp.dot(xim_ref[...].astype(BF16), wc_ref[blk, ns:2 * ns, :], preferred_element_type=F32))
        y_ref[:, chs] = y + d_ref[:, chs] * u_ref[0, :, chs].astype(F32)

    y = y_ref[...]
    g = y * (0.5 * (1.0 + jnp.tanh(math.sqrt(2.0 / math.pi) * (y + 0.044715 * (y * y * y)))))
    gate = _sigmoid(jnp.dot(g.astype(BF16), wglu_ref[...], preferred_element_type=F32) + bglu_ref[...])
    out = _rms_rows(g * gate, gout_ref[...]).astype(BF16)
    unperm = _stream_permutation(tc, inverse=True)
    for c in range(n_chunks):
        o_ref[0, c * tc:(c + 1) * tc, :] = jnp.dot(unperm, out[c * tc:(c + 1) * tc],
                                                    preferred_element_type=F32).astype(o_ref.dtype)


def _s5_scan_chunk(c, tc, seg, ns, buf, cols, sub, tab, bu_ref, xre_ref, xim_ref, pwre_ref, pwim_ref,
                   hre_ref, him_ref):
    rows = lambda r: slice(c * tc + r * SUBLANES, c * tc + (r + 1) * SUBLANES)

    a_re, a_im = pwre_ref[tab(0), cols], pwim_ref[tab(0), cols]
    h_re = bu_ref[buf, rows(0), 0:ns]
    h_im = bu_ref[buf, rows(0), ns:2 * ns]
    xre_ref[rows(0), :] = h_re
    xim_ref[rows(0), :] = h_im
    for r in range(1, seg):
        h_re, h_im = _cmul_add(bu_ref[buf, rows(r), 0:ns], bu_ref[buf, rows(r), ns:2 * ns], a_re, a_im, h_re, h_im)
        xre_ref[rows(r), :] = h_re
        xim_ref[rows(r), :] = h_im

    s_re = jnp.where(sub == 0, hre_ref[0, :, cols], pltpu.roll(h_re, 1, 0))
    s_im = jnp.where(sub == 0, him_ref[0, :, cols], pltpu.roll(h_im, 1, 0))
    for shift in (1, 2, 4):
        row = _pow_row(seg * shift)
        m_re = jnp.where(sub >= shift, pwre_ref[tab(row), cols], 0.0)
        m_im = jnp.where(sub >= shift, pwim_ref[tab(row), cols], 0.0)
        s_re, s_im = _cmul_add(s_re, s_im, m_re, m_im, pltpu.roll(s_re, shift, 0), pltpu.roll(s_im, shift, 0))

    for r in range(seg):
        t_re, t_im = _cmul_add(xre_ref[rows(r), :], xim_ref[rows(r), :],
                               pwre_ref[tab(r), cols], pwim_ref[tab(r), cols], s_re, s_im)
        xre_ref[rows(r), :] = t_re
        xim_ref[rows(r), :] = t_im
    hre_ref[0, :, cols] = t_re[SUBLANES - 1:SUBLANES, :]
    him_ref[0, :, cols] = t_im[SUBLANES - 1:SUBLANES, :]


def _s5_mixer(u, h0_re, h0_im, wb, wc, pw_re, pw_im, d_skip, w_glu, b_glu, g_out, *, tc, chunks_per_step):
    b, t, d_ssm = u.shape
    n_state = h0_re.shape[-1]
    rows = tc * chunks_per_step
    assert tc // SUBLANES <= MAX_SEG and t % rows == 0
    state = pl.BlockSpec((1, 1, n_state), lambda bi, j: (bi, 0, 0))
    act = pl.BlockSpec((1, rows, d_ssm), lambda bi, j: (bi, j, 0))
    ns = wb.shape[2] // 2
    return pl.pallas_call(
        functools.partial(_s5_kernel, tc=tc),
        out_shape=(jax.ShapeDtypeStruct((b, t, d_ssm), BF16),
                   jax.ShapeDtypeStruct((b, 1, n_state), F32), jax.ShapeDtypeStruct((b, 1, n_state), F32)),
        grid=(b, t // rows),
        in_specs=[act, state, state, _resident(wb.shape), _resident(wc.shape), _resident(pw_re.shape),
                  _resident(pw_im.shape), _resident(d_skip.shape), _resident(w_glu.shape),
                  _resident(b_glu.shape), _resident(g_out.shape)],
        out_specs=(act, state, state),
        scratch_shapes=[pltpu.VMEM((2, rows, 2 * ns), F32), pltpu.VMEM((rows, ns), F32),
                        pltpu.VMEM((rows, ns), F32), pltpu.VMEM((rows, d_ssm), F32)],
        compiler_params=_params("parallel", "arbitrary"),
        name="s5_mixer",
    )(u, h0_re, h0_im, wb, wc, pw_re, pw_im, d_skip, w_glu, b_glu, g_out)


def _out_proj_kernel(att_ref, ssm_ref, x_ref, gt1_ref, sh2_ref, sc2_ref, gatt_ref, wout_ref, g2_ref,
                     x1_ref, h2_ref):
    nb, t, d = x_ref.shape
    rows = nb * t
    d_att = att_ref.shape[-1]
    a = att_ref[...].astype(F32).reshape(rows, d_att)
    an = _rms_rows(a, gatt_ref[...]).astype(BF16)
    mix = (jnp.dot(an, wout_ref[0:d_att, :], preferred_element_type=F32)
           + jnp.dot(ssm_ref[...].reshape(rows, ssm_ref.shape[-1]), wout_ref[d_att:, :],
                     preferred_element_type=F32))
    x1 = x_ref[...] + gt1_ref[:, 0] * mix.reshape(nb, t, d)
    x1_ref[...] = x1
    h2 = _rms_rows(x1, g2_ref[...]) * (1.0 + sc2_ref[:, 0]) + sh2_ref[:, 0]
    h2_ref[...] = h2.astype(BF16)


def _out_proj(att, ssm, x, mod4, mod_row, g_att, w_out, g2, *, nb, t):
    n_seq, seq, d = x.shape
    row_blk = mod_row // nb
    act = lambda width: pl.BlockSpec((nb, t, width), lambda i, j: (i, j, 0))
    mod = lambda m: pl.BlockSpec((nb, 1, 1, d), lambda i, j, m=m: (row_blk + i, m, 0, 0))
    return pl.pallas_call(
        _out_proj_kernel,
        out_shape=(jax.ShapeDtypeStruct(x.shape, F32), jax.ShapeDtypeStruct(x.shape, BF16)),
        grid=(n_seq // nb, seq // t),
        in_specs=[act(att.shape[-1]), act(ssm.shape[-1]), act(d), mod(2), mod(3), mod(4),
                  _resident(g_att.shape), _resident(w_out.shape), _resident(g2.shape)],
        out_specs=(act(d), act(d)),
        compiler_params=_params("parallel", "parallel"),
        name="out_proj",
    )(att, ssm, x, mod4, mod4, mod4, g_att, w_out, g2)


def _mlp_kernel(h2_ref, x1_ref, gt2_ref, w1_ref, w2_ref, o_ref, acc_ref):
    nb, t, d = x1_ref.shape
    j = pl.program_id(2)

    @pl.when(j == 0)
    def _():
        acc_ref[...] = jnp.zeros(acc_ref.shape, F32)

    a = jnp.dot(h2_ref[...].reshape(nb * t, d), w1_ref[...], preferred_element_type=F32)
    r = jnp.maximum(a, 0.0)
    acc_ref[...] += jnp.dot((r * r).astype(BF16), w2_ref[...], preferred_element_type=F32)

    @pl.when(j == pl.num_programs(2) - 1)
    def _():
        o_ref[...] = x1_ref[...] + gt2_ref[:, 0] * acc_ref[...].reshape(nb, t, d)


def _mlp(h2, x1, mod4, mod_row, w1, w2, *, nb, t, tf):
    n_seq, seq, d = x1.shape
    d_ff = w1.shape[1]
    row_blk = mod_row // nb
    act = pl.BlockSpec((nb, t, d), lambda i, j, f: (i, j, 0))
    return pl.pallas_call(
        _mlp_kernel,
        out_shape=jax.ShapeDtypeStruct(x1.shape, F32),
        grid=(n_seq // nb, seq // t, d_ff // tf),
        in_specs=[act, act,
                  pl.BlockSpec((nb, 1, 1, d), lambda i, j, f: (row_blk + i, 5, 0, 0)),
                  pl.BlockSpec((d, tf), lambda i, j, f: (0, f)),
                  pl.BlockSpec((tf, d), lambda i, j, f: (f, 0))],
        out_specs=act,
        scratch_shapes=[pltpu.VMEM((nb * t, d), F32)],
        compiler_params=_params("parallel", "parallel", "arbitrary"),
        name="mlp",
    )(h2, x1, mod4, w1, w2)


def kernel(x_prompt, x_sample, c_prompt, c_sample, cache_k, cache_v, cache_logf, state_ssm_re, state_ssm_im,
           w_ada, b_ada, g_norm1, w_in, g_q, g_k, b_f, log_dt, a_re, a_im, b_re, b_im, c_re, c_im, d_skip,
           w_glu, b_glu, g_att_out, g_ssm_out, w_out, g_norm2, w_ff1, w_ff2):
    depth = w_ada.shape[0]
    assert depth == 1, "single-layer step"
    bsz, seq, d = x_prompt.shape
    dec_b, dec_s, _ = x_sample.shape
    past = cache_k.shape[2]
    n_heads = cache_k.shape[3]
    d_att = n_heads * HEAD_DIM
    n_groups, n_state = a_re.shape[1], a_re.shape[2]
    n_blk = n_groups // GROUPS_PER_BLOCK
    mod_rows = 16
    assert dec_b + bsz <= mod_rows
    l = 0

    c_rows = jnp.concatenate([c_sample, c_prompt, jnp.zeros((mod_rows - dec_b - bsz, d), F32)], axis=0)
    mod4 = _modulation(c_rows, w_ada[l], b_ada[l][None]).reshape(mod_rows, N_MOD, 1, d)
    row_sample, row_prompt = 0, dec_b

    w = w_in[l]
    w_qkv = w[:, 0:3 * d_att].astype(BF16)
    w_f = jnp.pad(w[:, 3 * d_att:3 * d_att + n_heads], ((0, 0), (0, LANES - n_heads))).astype(BF16)
    w_u = w[:, 3 * d_att + n_heads:].astype(BF16)
    b_f_pad = jnp.pad(b_f[l], (0, LANES - n_heads))[None]
    g1 = g_norm1[l][None]
    gq, gk = g_q[l][None], g_k[l][None]

    zb_re, zb_im, pw_re, pw_im = _s5_params(log_dt[l], a_re[l], a_im[l], b_re[l], b_im[l])
    pw_re = jnp.repeat(pw_re.reshape(N_POW, n_groups * n_state), SUBLANES, axis=0)
    pw_im = jnp.repeat(pw_im.reshape(N_POW, n_groups * n_state), SUBLANES, axis=0)
    blk4 = lambda m: m.reshape(n_blk, GROUPS_PER_BLOCK, m.shape[1], m.shape[2])
    wb = jnp.concatenate([_block_diag(blk4(zb_re)), _block_diag(blk4(zb_im))], axis=-1).astype(BF16)
    ct_re = jnp.swapaxes(c_re[l], 1, 2)
    ct_im = jnp.swapaxes(c_im[l], 1, 2)
    wc = jnp.concatenate([_block_diag(blk4(ct_re)), _block_diag(blk4(-ct_im))], axis=1).astype(BF16)
    d_row = d_skip[l][None]
    w_glu_b = w_glu[l].astype(BF16)
    b_glu_row = b_glu[l][None]
    g_ssm_row = g_ssm_out[l][None]
    g_att_row = g_att_out[l][None]
    w_out_b = w_out[l].astype(BF16)
    g2 = g_norm2[l][None]
    w1 = w_ff1[l].astype(BF16)
    w2 = w_ff2[l].astype(BF16)

    tm = 512
    tc_prompt = SUBLANES * MAX_SEG
    tc_sample = dec_s

    q_p, k32_p, kbf_p, v32_p, vbf_p, logf_p, u_p = _in_proj(
        x_prompt, mod4, row_prompt, g1, w_qkv, w_u, w_f, gq, gk, b_f_pad,
        nb=1, t=tm, n_heads=n_heads, s5_chunk=tc_prompt)
    fcum_p = _forget_cumsum(logf_p)
    att_p = _prompt_attention(q_p, kbf_p, vbf_p, fcum_p, n_heads=n_heads, tq=1024, tk=512, heads_per_step=1)
    zeros_state = jnp.zeros((bsz, 1, n_groups * n_state), F32)
    ssm_p, hre_p, him_p = _s5_mixer(u_p, zeros_state, zeros_state, wb, wc, pw_re, pw_im, d_row, w_glu_b,
                                    b_glu_row, g_ssm_row, tc=tc_prompt, chunks_per_step=tm // tc_prompt)
    x1_p, h2_p = _out_proj(att_p, ssm_p, x_prompt, mod4, row_prompt, g_att_row, w_out_b, g2, nb=1, t=tm)
    y_p = _mlp(h2_p, x1_p, mod4, row_prompt, w1, w2, nb=1, t=tm, tf=1024)

    q_s, k32_s, kbf_s, v32_s, vbf_s, logf_s, u_s = _in_proj(
        x_sample, mod4, row_sample, g1, w_qkv, w_u, w_f, gq, gk, b_f_pad,
        nb=dec_b, t=dec_s, n_heads=n_heads, s5_chunk=tc_sample)
    logf_all = jnp.concatenate([cache_logf[l], logf_s], axis=1)
    fcum_s = _forget_cumsum(logf_all)
    att_s = _sample_attention(q_s, kbf_s, vbf_s, cache_k[l].reshape(dec_b, past * n_heads, HEAD_DIM),
                              cache_v[l].reshape(dec_b, past * n_heads, HEAD_DIM), fcum_s, n_heads=n_heads)
    ssm_s, hre_s, him_s = _s5_mixer(u_s, state_ssm_re[l].reshape(dec_b, 1, -1),
                                    state_ssm_im[l].reshape(dec_b, 1, -1), wb, wc, pw_re, pw_im, d_row,
                                    w_glu_b, b_glu_row, g_ssm_row, tc=tc_sample, chunks_per_step=1)
    x1_s, h2_s = _out_proj(att_s, ssm_s, x_sample, mod4, row_sample, g_att_row, w_out_b, g2,
                           nb=dec_b, t=dec_s)
    y_s = _mlp(h2_s, x1_s, mod4, row_sample, w1, w2, nb=dec_b, t=dec_s, tf=1024)

    states = lambda a, n: a.reshape(1, n, n_groups, n_state)
    heads = lambda a, n, s: a.reshape(1, n, s, n_heads, HEAD_DIM)
    return (y_p, y_s,
            heads(k32_p, bsz, seq), heads(v32_p, bsz, seq), logf_p[None],
            states(hre_p, bsz), states(him_p, bsz),
            heads(k32_s, dec_b, dec_s), heads(v32_s, dec_b, dec_s), logf_s[None],
            states(hre_s, dec_b), states(him_s, dec_b))
```

```python
import functools
import math

import jax
import jax.numpy as jnp
from jax import lax
from jax.experimental import pallas as pl
from jax.experimental.pallas import tpu as pltpu

F32 = jnp.float32
BF16 = jnp.bfloat16
HIGHEST = lax.Precision.HIGHEST

HEAD_DIM = 128
SSM_GROUP = 16
N_MOD = 6
EPS = 1e-6
LOG2E = 1.4426950408889634

LANES = 128
SUBLANES = 8
GROUPS_PER_BLOCK = 16
MAX_SEG = 32
N_POW = MAX_SEG + 2
VMEM_LIMIT = 56 * 1024 * 1024


def _params(*sem):
    return pltpu.CompilerParams(dimension_semantics=sem, vmem_limit_bytes=VMEM_LIMIT)


def _resident(shape):
    nd = len(shape)
    return pl.BlockSpec(shape, lambda *_: (0,) * nd, pipeline_mode=pl.Buffered(1))


def _sigmoid(x):
    return 1.0 / (1.0 + jnp.exp(-x))


def _rms_rows(x, g):
    return x * lax.rsqrt(jnp.mean(x * x, axis=-1, keepdims=True) + EPS) * g


def _dot_nt(a, b):
    return lax.dot_general(a, b, (((1,), (1,)), ((), ())), preferred_element_type=F32)


def _cmul_add(x_re, x_im, a_re, a_im, h_re, h_im):
    return x_re + (a_re * h_re - a_im * h_im), x_im + (a_re * h_im + a_im * h_re)


def _stream_permutation(n, inverse):
    i = lax.broadcasted_iota(jnp.int32, (n, n), 0)
    j = lax.broadcasted_iota(jnp.int32, (n, n), 1)
    if inverse:
        i, j = j, i
    src = (i & (SUBLANES - 1)) * (n // SUBLANES) + (i >> 3)
    return jnp.where(j == src, 1.0, 0.0).astype(BF16)


def _split_bf16(x):
    hi = x.astype(BF16)
    return hi, (x - hi.astype(F32)).astype(BF16)


def _mod_kernel(c_ref, w_ref, b_ref, o_ref):
    c = c_ref[...]
    rows = c.shape[0]
    s_hi, s_lo = _split_bf16(c * _sigmoid(c))
    w_hi, w_lo = _split_bf16(w_ref[...])
    first = jnp.dot(jnp.concatenate([s_hi, s_lo], axis=0), w_hi, preferred_element_type=F32)
    second = jnp.dot(s_hi, w_lo, preferred_element_type=F32)
    o_ref[...] = first[0:rows] + (first[rows:2 * rows] + second) + b_ref[...]


def _modulation(c_rows, w_ada, b_ada):
    rows, d = c_rows.shape
    n = w_ada.shape[1]
    tn = 1024
    return pl.pallas_call(
        _mod_kernel,
        out_shape=jax.ShapeDtypeStruct((rows, n), F32),
        grid=(n // tn,),
        in_specs=[pl.BlockSpec((rows, d), lambda j: (0, 0)),
                  pl.BlockSpec((d, tn), lambda j: (0, j)),
                  pl.BlockSpec((1, tn), lambda j: (0, j))],
        out_specs=pl.BlockSpec((rows, tn), lambda j: (0, j)),
        compiler_params=_params("arbitrary"),
        name="modulation",
    )(c_rows, w_ada, b_ada)


def _s5_param_kernel(logdt_ref, are_ref, aim_ref, bre_ref, bim_ref, cre_ref, cim_ref,
                     wb_ref, wc_ref, pwre_ref, pwim_ref):
    n_groups, n_ch, n_st = bre_ref.shape
    gpb = GROUPS_PER_BLOCK
    ns = gpb * n_st
    dt = jnp.exp(logdt_ref[...])
    lam_re = jnp.minimum(are_ref[...], -1e-4)
    lam_im = aim_ref[...]
    mag = jnp.exp(lam_re * dt)
    ang = lam_im * dt
    ab_re = mag * jnp.cos(ang)
    ab_im = mag * jnp.sin(ang)
    den = lam_re * lam_re + lam_im * lam_im
    z_re = ((ab_re - 1.0) * lam_re + ab_im * lam_im) / den
    z_im = (ab_im * lam_re - (ab_re - 1.0) * lam_im) / den
    b_re = bre_ref[...]
    b_im = bim_ref[...]
    zb_re = z_re * b_re - z_im * b_im
    zb_im = z_re * b_im + z_im * b_re
    wb_ref[...] = jnp.zeros(wb_ref.shape, F32)
    wc_ref[...] = jnp.zeros(wc_ref.shape, F32)
    for g in range(n_groups):
        blk, gi = divmod(g, gpb)
        chs = slice(gi * n_ch, (gi + 1) * n_ch)
        sts = slice(gi * n_st, (gi + 1) * n_st)
        sts_im = slice(ns + gi * n_st, ns + (gi + 1) * n_st)
        wb_ref[blk, chs, sts] = zb_re[g]
        wb_ref[blk, chs, sts_im] = zb_im[g]
        wc_ref[blk, sts, chs] = cre_ref[g]
        wc_ref[blk, sts_im, chs] = -cim_ref[g]
    p_re, p_im = ab_re, ab_im
    for k in range(MAX_SEG):
        pwre_ref[k] = p_re
        pwim_ref[k] = p_im
        if k + 1 < MAX_SEG:
            p_re, p_im = p_re * ab_re - p_im * ab_im, p_re * ab_im + p_im * ab_re
    for k in range(MAX_SEG, N_POW):
        p_re, p_im = p_re * p_re - p_im * p_im, 2.0 * (p_re * p_im)
        pwre_ref[k] = p_re
        pwim_ref[k] = p_im


def _s5_params(log_dt, a_re, a_im, b_re, b_im, c_re, c_im):
    g, n, c = b_re.shape
    n_blk = g // GROUPS_PER_BLOCK
    ch, ns = GROUPS_PER_BLOCK * c, GROUPS_PER_BLOCK * n
    f = lambda shape: jax.ShapeDtypeStruct(shape, F32)
    return pl.pallas_call(
        _s5_param_kernel,
        out_shape=(f((n_blk, ch, 2 * ns)), f((n_blk, 2 * ns, ch)), f((N_POW, g, 1, n)), f((N_POW, g, 1, n))),
        compiler_params=_params(),
        name="s5_params",
    )(log_dt.reshape(g, 1, 1), a_re.reshape(g, 1, n), a_im.reshape(g, 1, n),
      jnp.swapaxes(b_re, 1, 2), jnp.swapaxes(b_im, 1, 2), jnp.swapaxes(c_re, 1, 2), jnp.swapaxes(c_im, 1, 2))


def _pow_row(seg_rows):
    if seg_rows <= MAX_SEG:
        return seg_rows - 1
    return MAX_SEG - 1 + int(math.log2(seg_rows // MAX_SEG))


def _in_proj_kernel(x_ref, sh_ref, sc_ref, g1_ref, wqkv_ref, wu_ref, wf_ref, gq_ref, gk_ref, bf_ref,
                    q_ref, k32_ref, kbf_ref, v32_ref, vbf_ref, logf_ref, u_ref,
                    *, n_heads, q_scale, s5_chunk):
    nb, t, d = x_ref.shape
    rows = nb * t
    d_att = n_heads * HEAD_DIM
    x = x_ref[...]
    y = _rms_rows(x, g1_ref[...])
    h = (y * (1.0 + sc_ref[:, 0]) + sh_ref[:, 0]).reshape(rows, d)
    hb = h.astype(BF16)

    q = jnp.dot(hb, wqkv_ref[:, 0:d_att], preferred_element_type=F32)
    for hh in range(n_heads):
        sl = slice(hh * HEAD_DIM, (hh + 1) * HEAD_DIM)
        qn = _rms_rows(q[:, sl], gq_ref[...]) * q_scale
        q_ref[:, :, sl] = qn.reshape(nb, t, HEAD_DIM).astype(BF16)

    k = jnp.dot(hb, wqkv_ref[:, d_att:2 * d_att], preferred_element_type=F32)
    for hh in range(n_heads):
        sl = slice(hh * HEAD_DIM, (hh + 1) * HEAD_DIM)
        kn = _rms_rows(k[:, sl], gk_ref[...])
        kbf_ref[:, :, sl] = kn.reshape(nb, t, HEAD_DIM).astype(BF16)
        for bi in range(nb):
            k32_ref[bi, pl.ds(hh, t, stride=n_heads), :] = kn[bi * t:(bi + 1) * t]

    v = jnp.dot(hb, wqkv_ref[:, 2 * d_att:3 * d_att], preferred_element_type=F32)
    for hh in range(n_heads):
        sl = slice(hh * HEAD_DIM, (hh + 1) * HEAD_DIM)
        for bi in range(nb):
            v32_ref[bi, pl.ds(hh, t, stride=n_heads), :] = v[bi * t:(bi + 1) * t, sl]
    vbf_ref[...] = v.reshape(nb, t, d_att).astype(BF16)

    f = jnp.dot(hb, wf_ref[...], preferred_element_type=F32) + bf_ref[...]
    logf = jnp.minimum(f, 0.0) - jnp.log1p(jnp.exp(-jnp.abs(f)))
    logf_ref[...] = logf[:, 0:n_heads].reshape(nb, t, n_heads)

    u = jnp.dot(hb, wu_ref[...], preferred_element_type=F32).astype(BF16)
    perm = _stream_permutation(s5_chunk, inverse=False)
    for c in range(rows // s5_chunk):
        uc = jnp.dot(perm, u[c * s5_chunk:(c + 1) * s5_chunk], preferred_element_type=F32).astype(BF16)
        if nb == 1:
            u_ref[0, c * s5_chunk:(c + 1) * s5_chunk, :] = uc
        else:
            per = t // s5_chunk
            u_ref[c // per, (c % per) * s5_chunk:(c % per + 1) * s5_chunk, :] = uc


def _in_proj(x, mod4, mod_row, g1, w_qkv, w_u, w_f, g_q, g_k, b_f_pad, *, nb, t, n_heads, s5_chunk):
    n_seq, seq, d = x.shape
    d_att = n_heads * HEAD_DIM
    d_ssm = w_u.shape[1]
    assert t % s5_chunk == 0
    grid = (n_seq // nb, seq // t)
    row_blk = mod_row // nb
    act = lambda width: pl.BlockSpec((nb, t, width), lambda i, j: (i, j, 0))
    act4 = pl.BlockSpec((nb, t * n_heads, HEAD_DIM), lambda i, j: (i, j, 0))
    mod = lambda m: pl.BlockSpec((nb, 1, 1, d), lambda i, j, m=m: (row_blk + i, m, 0, 0))
    sds = lambda width, dt: jax.ShapeDtypeStruct((n_seq, seq, width), dt)
    sds4 = jax.ShapeDtypeStruct((n_seq, seq * n_heads, HEAD_DIM), F32)
    kern = functools.partial(_in_proj_kernel, n_heads=n_heads, q_scale=HEAD_DIM ** -0.5 * LOG2E,
                             s5_chunk=s5_chunk)
    return pl.pallas_call(
        kern,
        out_shape=(sds(d_att, BF16), sds4, sds(d_att, BF16), sds4, sds(d_att, BF16),
                   sds(n_heads, F32), sds(d_ssm, BF16)),
        grid=grid,
        in_specs=[act(d), mod(0), mod(1), _resident(g1.shape), _resident(w_qkv.shape), _resident(w_u.shape),
                  _resident(w_f.shape), _resident(g_q.shape), _resident(g_k.shape), _resident(b_f_pad.shape)],
        out_specs=(act(d_att), act4, act(d_att), act4, act(d_att), act(n_heads), act(d_ssm)),
        compiler_params=_params("parallel", "parallel"),
        name="in_proj",
    )(x, mod4, mod4, g1, w_qkv, w_u, w_f, g_q, g_k, b_f_pad)


def _cumsum_kernel(x_ref, o_ref, *, n_chunks):
    x = x_ref[...]
    n = x.shape[0]
    li = lax.broadcasted_iota(jnp.int32, (LANES, LANES), 0)
    lj = lax.broadcasted_iota(jnp.int32, (LANES, LANES), 1)
    tri = jnp.where(li <= lj, 1.0, 0.0)
    within = jnp.dot(x, tri, precision=HIGHEST, preferred_element_type=F32)
    tot = jnp.dot(x, jnp.ones((LANES, LANES), F32), precision=HIGHEST, preferred_element_type=F32)
    r = lax.broadcasted_iota(jnp.int32, (n, n), 0)
    c = lax.broadcasted_iota(jnp.int32, (n, n), 1)
    seq_of = lambda i: jnp.floor((i.astype(F32) + 0.5) * (1.0 / n_chunks))
    earlier = jnp.where(seq_of(r) == seq_of(c), jnp.where(c < r, 1.0, 0.0), 0.0)
    before = jnp.dot(earlier, tot, precision=HIGHEST, preferred_element_type=F32)
    o_ref[...] = (within + before) * LOG2E


def _forget_cumsum(logf):
    b, length, h = logf.shape
    n_chunks = -(-length // LANES)
    rows = jnp.swapaxes(logf, 1, 2).reshape(b * h, length)
    rows = jnp.pad(rows, ((0, 0), (0, n_chunks * LANES - length)))
    out = pl.pallas_call(
        functools.partial(_cumsum_kernel, n_chunks=n_chunks),
        out_shape=jax.ShapeDtypeStruct((b * h * n_chunks, LANES), F32),
        compiler_params=_params(),
        name="forget_cumsum",
    )(rows.reshape(b * h * n_chunks, LANES))
    return jnp.swapaxes(out.reshape(b, h, n_chunks * LANES)[:, :, :length], 1, 2)


def _bias_columns(f_rows, head, key_side):
    n = f_rows.shape[0]
    lane_h = lax.broadcasted_iota(jnp.int32, f_rows.shape, 1)
    col = jnp.sum(jnp.where(lane_h == head, f_rows, 0.0), axis=-1, keepdims=True)
    hi = col.astype(BF16).astype(F32)
    rest = col - hi
    mid = rest.astype(BF16).astype(F32)
    lo = rest - mid
    lane = lax.broadcasted_iota(jnp.int32, (n, LANES), 1)
    if key_side:
        split = jnp.where(lane == 3, -hi, jnp.where(lane == 4, -mid, jnp.where(lane == 5, -lo, 0.0)))
        vals = jnp.where(lane < 3, 1.0, split)
    else:
        split = jnp.where(lane == 0, hi, jnp.where(lane == 1, mid, jnp.where(lane == 2, lo, 0.0)))
        vals = jnp.where((lane >= 3) & (lane < 6), 1.0, split)
    return vals.astype(BF16)


def _attn_kernel(q_ref, qn_ref, k_ref, v_ref, f_ref, o_ref, kaug_ref, vaug_ref, qa_ref, qnext_ref, s0_ref, s1_ref,
                 m_ref, acc_ref, *, tq, tk, build_rows, heads_per_step):
    qi = pl.program_id(2)
    t = k_ref.shape[1]
    dh = HEAD_DIM
    hps = heads_per_step
    head0 = pl.program_id(1) * hps

    def first_scores(src_ref, row0, dst_ref):
        f_rows = f_ref[0, pl.ds(row0, tq), :]
        for g in range(hps):
            qa = jnp.concatenate([src_ref[0, :, g * dh:(g + 1) * dh], _bias_columns(f_rows, head0 + g, False)],
                                 axis=-1)
            dst_ref[g] = qa
            s0_ref[g] = _dot_nt(qa, kaug_ref[g, 0:tk, :])

    def scores(c, s_ref, r0=0):
        off = pl.multiple_of(c * tk, tk)
        for g in range(hps):
            s_ref[g, r0:, :] = _dot_nt(qa_ref[g, r0:, :], kaug_ref[g, pl.ds(off, tk), :])

    @pl.when(qi == 0)
    def _first_block():
        for g in range(hps):
            kaug_ref[g, :, 0:dh] = k_ref[0, :, g * dh:(g + 1) * dh]
            vaug_ref[g, :, 0:dh] = v_ref[0, :, g * dh:(g + 1) * dh]
            vaug_ref[g, :, dh:2 * dh] = jnp.ones((t, dh), BF16)

        def piece(i, carry):
            r0 = pl.multiple_of(i * build_rows, build_rows)
            f_rows = f_ref[0, pl.ds(r0, build_rows), :]
            for g in range(hps):
                kaug_ref[g, pl.ds(r0, build_rows), dh:2 * dh] = _bias_columns(f_rows, head0 + g, True)
            return carry

        lax.fori_loop(0, t // build_rows, piece, 0)
        first_scores(q_ref, 0, qa_ref)

    @pl.when(qi > 0)
    def _take_prepared_queries():
        qa_ref[...] = qnext_ref[...]

    q0 = pl.multiple_of(qi * tq, tq)
    m_ref[...] = jnp.full(m_ref.shape, -jnp.inf, F32)
    acc_ref[...] = jnp.zeros(acc_ref.shape, F32)

    def absorb(c, s_ref, masked, r0=0):
        off = pl.multiple_of(c * tk, tk)
        for g in range(hps):
            s = s_ref[g, r0:, :]
            if masked:
                ahead = (lax.broadcasted_iota(jnp.int32, s.shape, 1)
                         - lax.broadcasted_iota(jnp.int32, s.shape, 0))
                s = jnp.where(ahead <= q0 + r0 - off, s, -jnp.inf)
            m_old = m_ref[g, r0:, :]
            m_new = jnp.maximum(m_old, jnp.max(s, axis=-1, keepdims=True))
            alpha = jnp.exp2(m_old - m_new)
            p = jnp.exp2(s - jnp.tile(m_new, (1, tk // LANES)))
            pv = jnp.dot(p.astype(BF16), vaug_ref[g, pl.ds(off, tk), :], preferred_element_type=F32)
            acc_ref[g, r0:, :] = jnp.tile(alpha, (1, 2)) * acc_ref[g, r0:, :] + pv
            m_ref[g, r0:, :] = m_new

    n_full = 2 * qi

    def pair(c):
        scores(c + 1, s1_ref)
        absorb(c, s0_ref, False)
        scores(c + 2, s0_ref)
        absorb(c + 1, s1_ref, False)

    def two_pairs(p, carry):
        pair(4 * p)
        pair(4 * p + 2)
        return carry

    lax.fori_loop(0, qi // 2, two_pairs, 0)

    @pl.when(qi % 2 == 1)
    def _odd_pair():
        pair(n_full - 2)

    scores(n_full + 1, s1_ref, r0=tk)
    absorb(n_full, s0_ref, True)
    first_scores(qn_ref, pl.multiple_of(jnp.minimum(qi + 1, pl.num_programs(2) - 1) * tq, tq), qnext_ref)
    absorb(n_full + 1, s1_ref, True, r0=tk)

    for g in range(hps):
        acc = acc_ref[g]
        o_ref[0, :, g * dh:(g + 1) * dh] = (acc[:, 0:dh] * (1.0 / acc[:, dh:2 * dh])).astype(o_ref.dtype)


def _prompt_attention(q, k, v, f_cum2, *, n_heads, tq, tk, heads_per_step):
    b, t, _ = q.shape
    assert t % tq == 0 and tq == 2 * tk and n_heads % heads_per_step == 0
    hps = heads_per_step
    width = hps * HEAD_DIM
    nq = t // tq
    q_blk = pl.BlockSpec((1, tq, width), lambda bi, h, i: (bi, i, h))
    qn_blk = pl.BlockSpec((1, tq, width), lambda bi, h, i: (bi, jnp.minimum(i + 1, nq - 1), h))
    kv_blk = pl.BlockSpec((1, t, width), lambda bi, h, i: (bi, 0, h))
    f_blk = pl.BlockSpec((1, t, n_heads), lambda bi, h, i: (bi, 0, 0), pipeline_mode=pl.Buffered(1))
    return pl.pallas_call(
        functools.partial(_attn_kernel, tq=tq, tk=tk, build_rows=1024, heads_per_step=hps),
        out_shape=jax.ShapeDtypeStruct(q.shape, BF16),
        grid=(b, n_heads // hps, nq),
        in_specs=[q_blk, qn_blk, kv_blk, kv_blk, f_blk],
        out_specs=q_blk,
        scratch_shapes=[pltpu.VMEM((hps, t, 2 * HEAD_DIM), BF16), pltpu.VMEM((hps, t, 2 * HEAD_DIM), BF16),
                        pltpu.VMEM((hps, tq, 2 * HEAD_DIM), BF16), pltpu.VMEM((hps, tq, 2 * HEAD_DIM), BF16),
                        pltpu.VMEM((hps, tq, tk), F32), pltpu.VMEM((hps, tq, tk), F32),
                        pltpu.VMEM((hps, tq, LANES), F32), pltpu.VMEM((hps, tq, 2 * HEAD_DIM), F32)],
        compiler_params=_params("parallel", "arbitrary", "arbitrary"),
        name="prompt_attention",
    )(q, q, k, v, f_cum2)


def _sample_attn_kernel(q_ref, kc_ref, vc_ref, kn_ref, vn_ref, f_ref, o_ref, *, past, n_heads):
    s_len = q_ref.shape[1]
    f_old = f_ref[0, 0:past, :]
    f_new = f_ref[0, past:past + s_len, :]
    row = lax.broadcasted_iota(jnp.int32, (s_len, s_len), 0)
    col = lax.broadcasted_iota(jnp.int32, (s_len, s_len), 1)
    for hh in range(n_heads):
        sl = slice(hh * HEAD_DIM, (hh + 1) * HEAD_DIM)
        qa = jnp.concatenate([q_ref[0, :, sl], _bias_columns(f_new, hh, False)], axis=-1)
        head_rows = pl.ds(hh, past, stride=n_heads)
        kca = jnp.concatenate([kc_ref[0, head_rows, :].astype(BF16), _bias_columns(f_old, hh, True)], axis=-1)
        kna = jnp.concatenate([kn_ref[0, :, sl], _bias_columns(f_new, hh, True)], axis=-1)
        s_c = _dot_nt(qa, kca)
        s_n = jnp.where(col <= row, _dot_nt(qa, kna), -jnp.inf)
        m = jnp.maximum(jnp.max(s_c, axis=-1, keepdims=True), jnp.max(s_n, axis=-1, keepdims=True))
        p_c = jnp.exp2(s_c - m)
        p_n = jnp.exp2(s_n - m)
        l = jnp.sum(p_c, axis=-1, keepdims=True) + jnp.sum(p_n, axis=-1, keepdims=True)
        o = (jnp.dot(p_c.astype(BF16), vc_ref[0, head_rows, :].astype(BF16), preferred_element_type=F32)
             + jnp.dot(p_n.astype(BF16), vn_ref[0, :, sl], preferred_element_type=F32))
        o_ref[0, :, sl] = (o * (1.0 / l)).astype(o_ref.dtype)


def _sample_attention(q, k_new, v_new, cache_k, cache_v, f_cum2, *, n_heads):
    b, s_len, d_att = q.shape
    past = cache_k.shape[1] // n_heads
    new = pl.BlockSpec((1, s_len, d_att), lambda bi: (bi, 0, 0))
    old = pl.BlockSpec((1, past * n_heads, HEAD_DIM), lambda bi: (bi, 0, 0))
    return pl.pallas_call(
        functools.partial(_sample_attn_kernel, past=past, n_heads=n_heads),
        out_shape=jax.ShapeDtypeStruct(q.shape, BF16),
        grid=(b,),
        in_specs=[new, old, old, new, new, pl.BlockSpec((1, past + s_len, n_heads), lambda bi: (bi, 0, 0))],
        out_specs=new,
        compiler_params=_params("parallel"),
        name="sample_attention",
    )(q, cache_k, cache_v, k_new, v_new, f_cum2)


def _s5_kernel(u_ref, h0re_ref, h0im_ref, wb_ref, wc_ref, pwre_ref, pwim_ref, d_ref, wglu_ref, bglu_ref,
               gout_ref, o_ref, hre_ref, him_ref, bu_ref, xre_ref, xim_ref, y_ref, *, tc):
    n_rows = u_ref.shape[1]
    n_chunks = n_rows // tc
    seg = tc // SUBLANES
    n_blk, ch = wb_ref.shape[0], wb_ref.shape[1]
    ns = wb_ref.shape[2] // 2

    @pl.when(pl.program_id(1) == 0)
    def _():
        hre_ref[...] = h0re_ref[...]
        him_ref[...] = h0im_ref[...]

    def b_proj(blk):
        bu_ref[blk % 2] = jnp.dot(u_ref[0, :, blk * ch:(blk + 1) * ch], wb_ref[blk], preferred_element_type=F32)

    sub = lax.broadcasted_iota(jnp.int32, (SUBLANES, ns), 0)
    tab = lambda r: slice(r * SUBLANES, (r + 1) * SUBLANES)
    b_proj(0)
    for blk in range(n_blk):
        if blk + 1 < n_blk:
            b_proj(blk + 1)
        buf = blk % 2
        cols = slice(blk * ns, (blk + 1) * ns)
        chs = slice(blk * ch, (blk + 1) * ch)
        for c in range(n_chunks):
            _s5_scan_chunk(c, tc, seg, ns, buf, cols, sub, tab, bu_ref, xre_ref, xim_ref, pwre_ref, pwim_ref,
                           hre_ref, him_ref)
        y = (jnp.dot(xre_ref[...].astype(BF16), wc_ref[blk, 0:ns, :], preferred_element_type=F32)
             + jnp.dot(xim_ref[...].astype(BF16), wc_ref[blk, ns:2 * ns, :], preferred_element_type=F32))
        y_ref[:, chs] = y + d_ref[:, chs] * u_ref[0, :, chs].astype(F32)

    y = y_ref[...]
    g = y * (0.5 * (1.0 + jnp.tanh(math.sqrt(2.0 / math.pi) * (y + 0.044715 * (y * y * y)))))
    gate = _sigmoid(jnp.dot(g.astype(BF16), wglu_ref[...], preferred_element_type=F32) + bglu_ref[...])
    out = _rms_rows(g * gate, gout_ref[...]).astype(BF16)
    unperm = _stream_permutation(tc, inverse=True)
    for c in range(n_chunks):
        o_ref[0, c * tc:(c + 1) * tc, :] = jnp.dot(unperm, out[c * tc:(c + 1) * tc],
                                                    preferred_element_type=F32).astype(o_ref.dtype)


def _s5_scan_chunk(c, tc, seg, ns, buf, cols, sub, tab, bu_ref, xre_ref, xim_ref, pwre_ref, pwim_ref,
                   hre_ref, him_ref):
    rows = lambda r: slice(c * tc + r * SUBLANES, c * tc + (r + 1) * SUBLANES)

    a_re, a_im = pwre_ref[tab(0), cols], pwim_ref[tab(0), cols]
    h_re = bu_ref[buf, rows(0), 0:ns]
    h_im = bu_ref[buf, rows(0), ns:2 * ns]
    xre_ref[rows(0), :] = h_re
    xim_ref[rows(0), :] = h_im
    for r in range(1, seg):
        h_re, h_im = _cmul_add(bu_ref[buf, rows(r), 0:ns], bu_ref[buf, rows(r), ns:2 * ns], a_re, a_im, h_re, h_im)
        xre_ref[rows(r), :] = h_re
        xim_ref[rows(r), :] = h_im

    s_re = jnp.where(sub == 0, hre_ref[0, :, cols], pltpu.roll(h_re, 1, 0))
    s_im = jnp.where(sub == 0, him_ref[0, :, cols], pltpu.roll(h_im, 1, 0))
    for shift in (1, 2, 4):
        row = _pow_row(seg * shift)
        m_re = jnp.where(sub >= shift, pwre_ref[tab(row), cols], 0.0)
        m_im = jnp.where(sub >= shift, pwim_ref[tab(row), cols], 0.0)
        s_re, s_im = _cmul_add(s_re, s_im, m_re, m_im, pltpu.roll(s_re, shift, 0), pltpu.roll(s_im, shift, 0))

    for r in range(seg):
        t_re, t_im = _cmul_add(xre_ref[rows(r), :], xim_ref[rows(r), :],
                               pwre_ref[tab(r), cols], pwim_ref[tab(r), cols], s_re, s_im)
        xre_ref[rows(r), :] = t_re
        xim_ref[rows(r), :] = t_im
    hre_ref[0, :, cols] = t_re[SUBLANES - 1:SUBLANES, :]
    him_ref[0, :, cols] = t_im[SUBLANES - 1:SUBLANES, :]


def _s5_mixer(u, h0_re, h0_im, wb, wc, pw_re, pw_im, d_skip, w_glu, b_glu, g_out, *, tc, chunks_per_step):
    b, t, d_ssm = u.shape
    n_state = h0_re.shape[-1]
    rows = tc * chunks_per_step
    assert tc // SUBLANES <= MAX_SEG and t % rows == 0
    state = pl.BlockSpec((1, 1, n_state), lambda bi, j: (bi, 0, 0))
    act = pl.BlockSpec((1, rows, d_ssm), lambda bi, j: (bi, j, 0))
    ns = wb.shape[2] // 2
    return pl.pallas_call(
        functools.partial(_s5_kernel, tc=tc),
        out_shape=(jax.ShapeDtypeStruct((b, t, d_ssm), BF16),
                   jax.ShapeDtypeStruct((b, 1, n_state), F32), jax.ShapeDtypeStruct((b, 1, n_state), F32)),
        grid=(b, t // rows),
        in_specs=[act, state, state, _resident(wb.shape), _resident(wc.shape), _resident(pw_re.shape),
                  _resident(pw_im.shape), _resident(d_skip.shape), _resident(w_glu.shape),
                  _resident(b_glu.shape), _resident(g_out.shape)],
        out_specs=(act, state, state),
        scratch_shapes=[pltpu.VMEM((2, rows, 2 * ns), F32), pltpu.VMEM((rows, ns), F32),
                        pltpu.VMEM((rows, ns), F32), pltpu.VMEM((rows, d_ssm), F32)],
        compiler_params=_params("parallel", "arbitrary"),
        name="s5_mixer",
    )(u, h0_re, h0_im, wb, wc, pw_re, pw_im, d_skip, w_glu, b_glu, g_out)


def _out_proj_kernel(att_ref, ssm_ref, x_ref, gt1_ref, sh2_ref, sc2_ref, gatt_ref, wout_ref, g2_ref,
                     x1_ref, h2_ref):
    nb, t, d = x_ref.shape
    rows = nb * t
    d_att = att_ref.shape[-1]
    a = att_ref[...].astype(F32).reshape(rows, d_att)
    an = _rms_rows(a, gatt_ref[...]).astype(BF16)
    mix = (jnp.dot(an, wout_ref[0:d_att, :], preferred_element_type=F32)
           + jnp.dot(ssm_ref[...].reshape(rows, ssm_ref.shape[-1]), wout_ref[d_att:, :],
                     preferred_element_type=F32))
    x1 = x_ref[...] + gt1_ref[:, 0] * mix.reshape(nb, t, d)
    x1_ref[...] = x1
    h2 = _rms_rows(x1, g2_ref[...]) * (1.0 + sc2_ref[:, 0]) + sh2_ref[:, 0]
    h2_ref[...] = h2.astype(BF16)


def _out_proj(att, ssm, x, mod4, mod_row, g_att, w_out, g2, *, nb, t):
    n_seq, seq, d = x.shape
    row_blk = mod_row // nb
    act = lambda width: pl.BlockSpec((nb, t, width), lambda i, j: (i, j, 0))
    mod = lambda m: pl.BlockSpec((nb, 1, 1, d), lambda i, j, m=m: (row_blk + i, m, 0, 0))
    return pl.pallas_call(
        _out_proj_kernel,
        out_shape=(jax.ShapeDtypeStruct(x.shape, F32), jax.ShapeDtypeStruct(x.shape, BF16)),
        grid=(n_seq // nb, seq // t),
        in_specs=[act(att.shape[-1]), act(ssm.shape[-1]), act(d), mod(2), mod(3), mod(4),
                  _resident(g_att.shape), _resident(w_out.shape), _resident(g2.shape)],
        out_specs=(act(d), act(d)),
        compiler_params=_params("parallel", "parallel"),
        name="out_proj",
    )(att, ssm, x, mod4, mod4, mod4, g_att, w_out, g2)


def _mlp_kernel(h2_ref, x1_ref, gt2_ref, w1_ref, w2_ref, o_ref, acc_ref):
    nb, t, d = x1_ref.shape
    j = pl.program_id(2)

    @pl.when(j == 0)
    def _():
        acc_ref[...] = jnp.zeros(acc_ref.shape, F32)

    a = jnp.dot(h2_ref[...].reshape(nb * t, d), w1_ref[...], preferred_element_type=F32)
    r = jnp.maximum(a, 0.0)
    acc_ref[...] += jnp.dot((r * r).astype(BF16), w2_ref[...], preferred_element_type=F32)

    @pl.when(j == pl.num_programs(2) - 1)
    def _():
        o_ref[...] = x1_ref[...] + gt2_ref[:, 0] * acc_ref[...].reshape(nb, t, d)


def _mlp(h2, x1, mod4, mod_row, w1, w2, *, nb, t, tf):
    n_seq, seq, d = x1.shape
    d_ff = w1.shape[1]
    row_blk = mod_row // nb
    act = pl.BlockSpec((nb, t, d), lambda i, j, f: (i, j, 0))
    return pl.pallas_call(
        _mlp_kernel,
        out_shape=jax.ShapeDtypeStruct(x1.shape, F32),
        grid=(n_seq // nb, seq // t, d_ff // tf),
        in_specs=[act, act,
                  pl.BlockSpec((nb, 1, 1, d), lambda i, j, f: (row_blk + i, 5, 0, 0)),
                  pl.BlockSpec((d, tf), lambda i, j, f: (0, f)),
                  pl.BlockSpec((tf, d), lambda i, j, f: (f, 0))],
        out_specs=act,
        scratch_shapes=[pltpu.VMEM((nb * t, d), F32)],
        compiler_params=_params("parallel", "parallel", "arbitrary"),
        name="mlp",
    )(h2, x1, mod4, w1, w2)


def kernel(x_prompt, x_sample, c_prompt, c_sample, cache_k, cache_v, cache_logf, state_ssm_re, state_ssm_im,
           w_ada, b_ada, g_norm1, w_in, g_q, g_k, b_f, log_dt, a_re, a_im, b_re, b_im, c_re, c_im, d_skip,
           w_glu, b_glu, g_att_out, g_ssm_out, w_out, g_norm2, w_ff1, w_ff2):
    depth = w_ada.shape[0]
    assert depth == 1, "single-layer step"
    bsz, seq, d = x_prompt.shape
    dec_b, dec_s, _ = x_sample.shape
    past = cache_k.shape[2]
    n_heads = cache_k.shape[3]
    d_att = n_heads * HEAD_DIM
    n_groups, n_state = a_re.shape[1], a_re.shape[2]
    n_blk = n_groups // GROUPS_PER_BLOCK
    mod_rows = 16
    assert dec_b + bsz <= mod_rows
    l = 0

    c_rows = jnp.concatenate([c_sample, c_prompt, jnp.zeros((mod_rows - dec_b - bsz, d), F32)], axis=0)
    mod4 = _modulation(c_rows, w_ada[l], b_ada[l][None]).reshape(mod_rows, N_MOD, 1, d)
    row_sample, row_prompt = 0, dec_b

    w_qkv = w_in[l, :, 0:3 * d_att].astype(BF16)
    w_f = jnp.pad(w_in[l, :, 3 * d_att:3 * d_att + n_heads], ((0, 0), (0, LANES - n_heads))).astype(BF16)
    w_u = w_in[l, :, 3 * d_att + n_heads:].astype(BF16)
    b_f_pad = jnp.pad(b_f[l], (0, LANES - n_heads))[None]
    g1 = g_norm1[l][None]
    gq, gk = g_q[l][None], g_k[l][None]

    wb32, wc32, pw_re, pw_im = _s5_params(log_dt[l], a_re[l], a_im[l], b_re[l], b_im[l], c_re[l], c_im[l])
    wb, wc = wb32.astype(BF16), wc32.astype(BF16)
    pw_re = jnp.repeat(pw_re.reshape(N_POW, n_groups * n_state), SUBLANES, axis=0)
    pw_im = jnp.repeat(pw_im.reshape(N_POW, n_groups * n_state), SUBLANES, axis=0)
    d_row = d_skip[l][None]
    w_glu_b = w_glu[l].astype(BF16)
    b_glu_row = b_glu[l][None]
    g_ssm_row = g_ssm_out[l][None]
    g_att_row = g_att_out[l][None]
    w_out_b = w_out[l].astype(BF16)
    g2 = g_norm2[l][None]
    w1 = w_ff1[l].astype(BF16)
    w2 = w_ff2[l].astype(BF16)

    tm = 512
    tc_prompt = SUBLANES * MAX_SEG
    tc_sample = dec_s

    q_p, k32_p, kbf_p, v32_p, vbf_p, logf_p, u_p = _in_proj(
        x_prompt, mod4, row_prompt, g1, w_qkv, w_u, w_f, gq, gk, b_f_pad,
        nb=1, t=tm, n_heads=n_heads, s5_chunk=tc_prompt)
    fcum_p = _forget_cumsum(logf_p)
    att_p = _prompt_attention(q_p, kbf_p, vbf_p, fcum_p, n_heads=n_heads, tq=1024, tk=512, heads_per_step=1)
    zeros_state = jnp.zeros((bsz, 1, n_groups * n_state), F32)
    ssm_p, hre_p, him_p = _s5_mixer(u_p, zeros_state, zeros_state, wb, wc, pw_re, pw_im, d_row, w_glu_b,
                                    b_glu_row, g_ssm_row, tc=tc_prompt, chunks_per_step=tm // tc_prompt)
    x1_p, h2_p = _out_proj(att_p, ssm_p, x_prompt, mod4, row_prompt, g_att_row, w_out_b, g2, nb=1, t=tm)
    y_p = _mlp(h2_p, x1_p, mod4, row_prompt, w1, w2, nb=1, t=tm, tf=1024)

    q_s, k32_s, kbf_s, v32_s, vbf_s, logf_s, u_s = _in_proj(
        x_sample, mod4, row_sample, g1, w_qkv, w_u, w_f, gq, gk, b_f_pad,
        nb=dec_b, t=dec_s, n_heads=n_heads, s5_chunk=tc_sample)
    logf_all = jnp.concatenate([cache_logf[l], logf_s], axis=1)
    fcum_s = _forget_cumsum(logf_all)
    att_s = _sample_attention(q_s, kbf_s, vbf_s, cache_k[l].reshape(dec_b, past * n_heads, HEAD_DIM),
                              cache_v[l].reshape(dec_b, past * n_heads, HEAD_DIM), fcum_s, n_heads=n_heads)
    ssm_s, hre_s, him_s = _s5_mixer(u_s, state_ssm_re[l].reshape(dec_b, 1, -1),
                                    state_ssm_im[l].reshape(dec_b, 1, -1), wb, wc, pw_re, pw_im, d_row,
                                    w_glu_b, b_glu_row, g_ssm_row, tc=tc_sample, chunks_per_step=1)
    x1_s, h2_s = _out_proj(att_s, ssm_s, x_sample, mod4, row_sample, g_att_row, w_out_b, g2,
                           nb=dec_b, t=dec_s)
    y_s = _mlp(h2_s, x1_s, mod4, row_sample, w1, w2, nb=dec_b, t=dec_s, tf=1024)

    states = lambda a, n: a.reshape(1, n, n_groups, n_state)
    heads = lambda a, n, s: a.reshape(1, n, s, n_heads, HEAD_DIM)
    return (y_p, y_s,
            heads(k32_p, bsz, seq), heads(v32_p, bsz, seq), logf_p[None],
            states(hre_p, bsz), states(him_p, bsz),
            heads(k32_s, dec_b, dec_s), heads(v32_s, dec_b, dec_s), logf_s[None],
            states(hre_s, dec_b), states(him_s, dec_b))
```

```python
import functools
import math

import jax
import jax.numpy as jnp
from jax import lax
from jax.experimental import pallas as pl
from jax.experimental.pallas import tpu as pltpu

F32 = jnp.float32
BF16 = jnp.bfloat16
HIGHEST = lax.Precision.HIGHEST

HEAD_DIM = 128
SSM_GROUP = 16
N_MOD = 6
EPS = 1e-6
LOG2E = 1.4426950408889634

LANES = 128
SUBLANES = 8
GROUPS_PER_BLOCK = 16
MAX_SEG = 32
N_POW = MAX_SEG + 2
VMEM_LIMIT = 56 * 1024 * 1024


def _params(*sem):
    return pltpu.CompilerParams(dimension_semantics=sem, vmem_limit_bytes=VMEM_LIMIT)


def _resident(shape):
    nd = len(shape)
    return pl.BlockSpec(shape, lambda *_: (0,) * nd, pipeline_mode=pl.Buffered(1))


def _sigmoid(x):
    return 1.0 / (1.0 + jnp.exp(-x))


def _rms_rows(x, g):
    return x * lax.rsqrt(jnp.mean(x * x, axis=-1, keepdims=True) + EPS) * g


def _dot_nt(a, b):
    return lax.dot_general(a, b, (((1,), (1,)), ((), ())), preferred_element_type=F32)


def _cmul_add(x_re, x_im, a_re, a_im, h_re, h_im):
    return x_re + (a_re * h_re - a_im * h_im), x_im + (a_re * h_im + a_im * h_re)


def _stream_permutation(n, inverse):
    i = lax.broadcasted_iota(jnp.int32, (n, n), 0)
    j = lax.broadcasted_iota(jnp.int32, (n, n), 1)
    if inverse:
        i, j = j, i
    src = (i & (SUBLANES - 1)) * (n // SUBLANES) + (i >> 3)
    return jnp.where(j == src, 1.0, 0.0).astype(BF16)


def _split_bf16(x):
    hi = x.astype(BF16)
    return hi, (x - hi.astype(F32)).astype(BF16)


def _mod_kernel(c_ref, w_ref, b_ref, o_ref):
    c = c_ref[...]
    rows = c.shape[0]
    s_hi, s_lo = _split_bf16(c * _sigmoid(c))
    w_hi, w_lo = _split_bf16(w_ref[...])
    first = jnp.dot(jnp.concatenate([s_hi, s_lo], axis=0), w_hi, preferred_element_type=F32)
    second = jnp.dot(s_hi, w_lo, preferred_element_type=F32)
    o_ref[...] = first[0:rows] + (first[rows:2 * rows] + second) + b_ref[...]


def _modulation(c_rows, w_ada, b_ada):
    rows, d = c_rows.shape
    n = w_ada.shape[1]
    tn = 1024
    return pl.pallas_call(
        _mod_kernel,
        out_shape=jax.ShapeDtypeStruct((rows, n), F32),
        grid=(n // tn,),
        in_specs=[pl.BlockSpec((rows, d), lambda j: (0, 0)),
                  pl.BlockSpec((d, tn), lambda j: (0, j)),
                  pl.BlockSpec((1, tn), lambda j: (0, j))],
        out_specs=pl.BlockSpec((rows, tn), lambda j: (0, j)),
        compiler_params=_params("arbitrary"),
        name="modulation",
    )(c_rows, w_ada, b_ada)


def _zoh(log_dt, a_re, a_im):
    dt = jnp.exp(log_dt)
    lam_re = jnp.minimum(a_re, -1e-4)
    lam_im = a_im
    mag = jnp.exp(lam_re * dt)
    ang = lam_im * dt
    ab_re = mag * jnp.cos(ang)
    ab_im = mag * jnp.sin(ang)
    den = lam_re * lam_re + lam_im * lam_im
    z_re = ((ab_re - 1.0) * lam_re + ab_im * lam_im) / den
    z_im = (ab_im * lam_re - (ab_re - 1.0) * lam_im) / den
    return ab_re, ab_im, z_re, z_im


def _s5_param_kernel(logdt_ref, are_ref, aim_ref, logdt2_ref, are2_ref, aim2_ref, bre_ref, bim_ref, cre_ref,
                     cim_ref, wb_ref, wc_ref, pwre_ref, pwim_ref):
    n_groups, n_ch, n_st = bre_ref.shape
    gpb = GROUPS_PER_BLOCK
    ns = gpb * n_st
    _, _, z_re, z_im = _zoh(logdt_ref[...], are_ref[...], aim_ref[...])
    ab_re, ab_im, _, _ = _zoh(logdt2_ref[...], are2_ref[...], aim2_ref[...])
    b_re = bre_ref[...]
    b_im = bim_ref[...]
    zb_re = z_re * b_re - z_im * b_im
    zb_im = z_re * b_im + z_im * b_re
    wb_ref[...] = jnp.zeros(wb_ref.shape, F32)
    wc_ref[...] = jnp.zeros(wc_ref.shape, F32)
    for g in range(n_groups):
        blk, gi = divmod(g, gpb)
        chs = slice(gi * n_ch, (gi + 1) * n_ch)
        sts = slice(gi * n_st, (gi + 1) * n_st)
        sts_im = slice(ns + gi * n_st, ns + (gi + 1) * n_st)
        wb_ref[blk, chs, sts] = zb_re[g]
        wb_ref[blk, chs, sts_im] = zb_im[g]
        wc_ref[blk, sts, chs] = cre_ref[g]
        wc_ref[blk, sts_im, chs] = -cim_ref[g]
    p_re, p_im = ab_re, ab_im
    for k in range(MAX_SEG):
        pwre_ref[k] = p_re
        pwim_ref[k] = p_im
        if k + 1 < MAX_SEG:
            p_re, p_im = p_re * ab_re - p_im * ab_im, p_re * ab_im + p_im * ab_re
    for k in range(MAX_SEG, N_POW):
        p_re, p_im = p_re * p_re - p_im * p_im, 2.0 * (p_re * p_im)
        pwre_ref[k] = p_re
        pwim_ref[k] = p_im


def _s5_params(log_dt, a_re, a_im, b_re, b_im, c_re, c_im):
    g, n, c = b_re.shape
    n_blk = g // GROUPS_PER_BLOCK
    ch, ns = GROUPS_PER_BLOCK * c, GROUPS_PER_BLOCK * n
    f = lambda shape: jax.ShapeDtypeStruct(shape, F32)
    return pl.pallas_call(
        _s5_param_kernel,
        out_shape=(f((n_blk, ch, 2 * ns)), f((n_blk, 2 * ns, ch)), f((N_POW, g, n)), f((N_POW, g, n))),
        compiler_params=_params(),
        name="s5_params",
    )(log_dt.reshape(g, 1, 1), a_re.reshape(g, 1, n), a_im.reshape(g, 1, n), log_dt.reshape(g, 1), a_re, a_im,
      jnp.swapaxes(b_re, 1, 2), jnp.swapaxes(b_im, 1, 2), jnp.swapaxes(c_re, 1, 2), jnp.swapaxes(c_im, 1, 2))


def _pow_row(seg_rows):
    if seg_rows <= MAX_SEG:
        return seg_rows - 1
    return MAX_SEG - 1 + int(math.log2(seg_rows // MAX_SEG))


def _in_proj_kernel(x_ref, sh_ref, sc_ref, g1_ref, wqkv_ref, wu_ref, wf_ref, gq_ref, gk_ref, bf_ref,
                    q_ref, k32_ref, kbf_ref, v32_ref, vbf_ref, logf_ref, u_ref,
                    *, n_heads, q_scale, s5_chunk):
    nb, t, d = x_ref.shape
    rows = nb * t
    d_att = n_heads * HEAD_DIM
    x = x_ref[...]
    y = _rms_rows(x, g1_ref[...])
    h = (y * (1.0 + sc_ref[:, 0]) + sh_ref[:, 0]).reshape(rows, d)
    hb = h.astype(BF16)

    q = jnp.dot(hb, wqkv_ref[:, 0:d_att], preferred_element_type=F32)
    for hh in range(n_heads):
        sl = slice(hh * HEAD_DIM, (hh + 1) * HEAD_DIM)
        qn = _rms_rows(q[:, sl], gq_ref[...]) * q_scale
        q_ref[:, :, sl] = qn.reshape(nb, t, HEAD_DIM).astype(BF16)

    k = jnp.dot(hb, wqkv_ref[:, d_att:2 * d_att], preferred_element_type=F32)
    for hh in range(n_heads):
        sl = slice(hh * HEAD_DIM, (hh + 1) * HEAD_DIM)
        kn = _rms_rows(k[:, sl], gk_ref[...])
        kbf_ref[:, :, sl] = kn.reshape(nb, t, HEAD_DIM).astype(BF16)
        for bi in range(nb):
            k32_ref[bi, pl.ds(hh, t, stride=n_heads), :] = kn[bi * t:(bi + 1) * t]

    v = jnp.dot(hb, wqkv_ref[:, 2 * d_att:3 * d_att], preferred_element_type=F32)
    for hh in range(n_heads):
        sl = slice(hh * HEAD_DIM, (hh + 1) * HEAD_DIM)
        for bi in range(nb):
            v32_ref[bi, pl.ds(hh, t, stride=n_heads), :] = v[bi * t:(bi + 1) * t, sl]
    vbf_ref[...] = v.reshape(nb, t, d_att).astype(BF16)

    f = jnp.dot(hb, wf_ref[...], preferred_element_type=F32) + bf_ref[...]
    logf = jnp.minimum(f, 0.0) - jnp.log1p(jnp.exp(-jnp.abs(f)))
    logf_ref[...] = logf[:, 0:n_heads].reshape(nb, t, n_heads)

    u = jnp.dot(hb, wu_ref[...], preferred_element_type=F32).astype(BF16)
    perm = _stream_permutation(s5_chunk, inverse=False)
    for c in range(rows // s5_chunk):
        uc = jnp.dot(perm, u[c * s5_chunk:(c + 1) * s5_chunk], preferred_element_type=F32).astype(BF16)
        if nb == 1:
            u_ref[0, c * s5_chunk:(c + 1) * s5_chunk, :] = uc
        else:
            per = t // s5_chunk
            u_ref[c // per, (c % per) * s5_chunk:(c % per + 1) * s5_chunk, :] = uc


def _in_proj(x, mod4, mod_row, g1, w_qkv, w_u, w_f, g_q, g_k, b_f_pad, *, nb, t, n_heads, s5_chunk):
    n_seq, seq, d = x.shape
    d_att = n_heads * HEAD_DIM
    d_ssm = w_u.shape[1]
    assert t % s5_chunk == 0
    grid = (n_seq // nb, seq // t)
    row_blk = mod_row // nb
    act = lambda width: pl.BlockSpec((nb, t, width), lambda i, j: (i, j, 0))
    act4 = pl.BlockSpec((nb, t * n_heads, HEAD_DIM), lambda i, j: (i, j, 0))
    mod = lambda m: pl.BlockSpec((nb, 1, 1, d), lambda i, j, m=m: (row_blk + i, m, 0, 0))
    sds = lambda width, dt: jax.ShapeDtypeStruct((n_seq, seq, width), dt)
    sds4 = jax.ShapeDtypeStruct((n_seq, seq * n_heads, HEAD_DIM), F32)
    kern = functools.partial(_in_proj_kernel, n_heads=n_heads, q_scale=HEAD_DIM ** -0.5 * LOG2E,
                             s5_chunk=s5_chunk)
    return pl.pallas_call(
        kern,
        out_shape=(sds(d_att, BF16), sds4, sds(d_att, BF16), sds4, sds(d_att, BF16),
                   sds(n_heads, F32), sds(d_ssm, BF16)),
        grid=grid,
        in_specs=[act(d), mod(0), mod(1), _resident(g1.shape), _resident(w_qkv.shape), _resident(w_u.shape),
                  _resident(w_f.shape), _resident(g_q.shape), _resident(g_k.shape), _resident(b_f_pad.shape)],
        out_specs=(act(d_att), act4, act(d_att), act4, act(d_att), act(n_heads), act(d_ssm)),
        compiler_params=_params("parallel", "parallel"),
        name="in_proj",
    )(x, mod4, mod4, g1, w_qkv, w_u, w_f, g_q, g_k, b_f_pad)


def _cumsum_kernel(x_ref, o_ref, *, n_chunks):
    x = x_ref[...]
    n = x.shape[0]
    li = lax.broadcasted_iota(jnp.int32, (LANES, LANES), 0)
    lj = lax.broadcasted_iota(jnp.int32, (LANES, LANES), 1)
    tri = jnp.where(li <= lj, 1.0, 0.0)
    within = jnp.dot(x, tri, precision=HIGHEST, preferred_element_type=F32)
    tot = jnp.dot(x, jnp.ones((LANES, LANES), F32), precision=HIGHEST, preferred_element_type=F32)
    r = lax.broadcasted_iota(jnp.int32, (n, n), 0)
    c = lax.broadcasted_iota(jnp.int32, (n, n), 1)
    seq_of = lambda i: jnp.floor((i.astype(F32) + 0.5) * (1.0 / n_chunks))
    earlier = jnp.where(seq_of(r) == seq_of(c), jnp.where(c < r, 1.0, 0.0), 0.0)
    before = jnp.dot(earlier, tot, precision=HIGHEST, preferred_element_type=F32)
    o_ref[...] = (within + before) * LOG2E


def _forget_cumsum(logf):
    b, length, h = logf.shape
    n_chunks = -(-length // LANES)
    rows = jnp.swapaxes(logf, 1, 2).reshape(b * h, length)
    rows = jnp.pad(rows, ((0, 0), (0, n_chunks * LANES - length)))
    out = pl.pallas_call(
        functools.partial(_cumsum_kernel, n_chunks=n_chunks),
        out_shape=jax.ShapeDtypeStruct((b * h * n_chunks, LANES), F32),
        compiler_params=_params(),
        name="forget_cumsum",
    )(rows.reshape(b * h * n_chunks, LANES))
    return jnp.swapaxes(out.reshape(b, h, n_chunks * LANES)[:, :, :length], 1, 2)


def _bias_columns(f_rows, head, key_side):
    n = f_rows.shape[0]
    lane_h = lax.broadcasted_iota(jnp.int32, f_rows.shape, 1)
    col = jnp.sum(jnp.where(lane_h == head, f_rows, 0.0), axis=-1, keepdims=True)
    hi = col.astype(BF16).astype(F32)
    rest = col - hi
    mid = rest.astype(BF16).astype(F32)
    lo = rest - mid
    lane = lax.broadcasted_iota(jnp.int32, (n, LANES), 1)
    if key_side:
        split = jnp.where(lane == 3, -hi, jnp.where(lane == 4, -mid, jnp.where(lane == 5, -lo, 0.0)))
        vals = jnp.where(lane < 3, 1.0, split)
    else:
        split = jnp.where(lane == 0, hi, jnp.where(lane == 1, mid, jnp.where(lane == 2, lo, 0.0)))
        vals = jnp.where((lane >= 3) & (lane < 6), 1.0, split)
    return vals.astype(BF16)


def _attn_kernel(q_ref, qn_ref, k_ref, v_ref, f_ref, o_ref, kaug_ref, vaug_ref, qa_ref, qnext_ref, s0_ref, s1_ref,
                 m_ref, acc_ref, *, tq, tk, build_rows, heads_per_step):
    qi = pl.program_id(2)
    t = k_ref.shape[1]
    dh = HEAD_DIM
    hps = heads_per_step
    head0 = pl.program_id(1) * hps

    def first_scores(src_ref, row0, dst_ref):
        f_rows = f_ref[0, pl.ds(row0, tq), :]
        for g in range(hps):
            qa = jnp.concatenate([src_ref[0, :, g * dh:(g + 1) * dh], _bias_columns(f_rows, head0 + g, False)],
                                 axis=-1)
            dst_ref[g] = qa
            s0_ref[g] = _dot_nt(qa, kaug_ref[g, 0:tk, :])

    def scores(c, s_ref, r0=0):
        off = pl.multiple_of(c * tk, tk)
        for g in range(hps):
            s_ref[g, r0:, :] = _dot_nt(qa_ref[g, r0:, :], kaug_ref[g, pl.ds(off, tk), :])

    @pl.when(qi == 0)
    def _first_block():
        for g in range(hps):
            kaug_ref[g, :, 0:dh] = k_ref[0, :, g * dh:(g + 1) * dh]
            vaug_ref[g, :, 0:dh] = v_ref[0, :, g * dh:(g + 1) * dh]
            vaug_ref[g, :, dh:2 * dh] = jnp.ones((t, dh), BF16)

        def piece(i, carry):
            r0 = pl.multiple_of(i * build_rows, build_rows)
            f_rows = f_ref[0, pl.ds(r0, build_rows), :]
            for g in range(hps):
                kaug_ref[g, pl.ds(r0, build_rows), dh:2 * dh] = _bias_columns(f_rows, head0 + g, True)
            return carry

        lax.fori_loop(0, t // build_rows, piece, 0)
        first_scores(q_ref, 0, qa_ref)

    @pl.when(qi > 0)
    def _take_prepared_queries():
        qa_ref[...] = qnext_ref[...]

    q0 = pl.multiple_of(qi * tq, tq)
    m_ref[...] = jnp.full(m_ref.shape, -jnp.inf, F32)
    acc_ref[...] = jnp.zeros(acc_ref.shape, F32)

    def absorb(c, s_ref, masked, r0=0):
        off = pl.multiple_of(c * tk, tk)
        for g in range(hps):
            s = s_ref[g, r0:, :]
            if masked:
                ahead = (lax.broadcasted_iota(jnp.int32, s.shape, 1)
                         - lax.broadcasted_iota(jnp.int32, s.shape, 0))
                s = jnp.where(ahead <= q0 + r0 - off, s, -jnp.inf)
            m_old = m_ref[g, r0:, :]
            m_new = jnp.maximum(m_old, jnp.max(s, axis=-1, keepdims=True))
            alpha = jnp.exp2(m_old - m_new)
            p = jnp.exp2(s - jnp.tile(m_new, (1, tk // LANES)))
            pv = jnp.dot(p.astype(BF16), vaug_ref[g, pl.ds(off, tk), :], preferred_element_type=F32)
            acc_ref[g, r0:, :] = jnp.tile(alpha, (1, 2)) * acc_ref[g, r0:, :] + pv
            m_ref[g, r0:, :] = m_new

    n_full = 2 * qi

    def pair(c):
        scores(c + 1, s1_ref)
        absorb(c, s0_ref, False)
        scores(c + 2, s0_ref)
        absorb(c + 1, s1_ref, False)

    def two_pairs(p, carry):
        pair(4 * p)
        pair(4 * p + 2)
        return carry

    lax.fori_loop(0, qi // 2, two_pairs, 0)

    @pl.when(qi % 2 == 1)
    def _odd_pair():
        pair(n_full - 2)

    scores(n_full + 1, s1_ref, r0=tk)
    absorb(n_full, s0_ref, True)
    first_scores(qn_ref, pl.multiple_of(jnp.minimum(qi + 1, pl.num_programs(2) - 1) * tq, tq), qnext_ref)
    absorb(n_full + 1, s1_ref, True, r0=tk)

    for g in range(hps):
        acc = acc_ref[g]
        o_ref[0, :, g * dh:(g + 1) * dh] = (acc[:, 0:dh] * (1.0 / acc[:, dh:2 * dh])).astype(o_ref.dtype)


def _prompt_attention(q, k, v, f_cum2, *, n_heads, tq, tk, heads_per_step):
    b, t, _ = q.shape
    assert t % tq == 0 and tq == 2 * tk and n_heads % heads_per_step == 0
    hps = heads_per_step
    width = hps * HEAD_DIM
    nq = t // tq
    q_blk = pl.BlockSpec((1, tq, width), lambda bi, h, i: (bi, i, h))
    qn_blk = pl.BlockSpec((1, tq, width), lambda bi, h, i: (bi, jnp.minimum(i + 1, nq - 1), h))
    kv_blk = pl.BlockSpec((1, t, width), lambda bi, h, i: (bi, 0, h))
    f_blk = pl.BlockSpec((1, t, n_heads), lambda bi, h, i: (bi, 0, 0), pipeline_mode=pl.Buffered(1))
    return pl.pallas_call(
        functools.partial(_attn_kernel, tq=tq, tk=tk, build_rows=1024, heads_per_step=hps),
        out_shape=jax.ShapeDtypeStruct(q.shape, BF16),
        grid=(b, n_heads // hps, nq),
        in_specs=[q_blk, qn_blk, kv_blk, kv_blk, f_blk],
        out_specs=q_blk,
        scratch_shapes=[pltpu.VMEM((hps, t, 2 * HEAD_DIM), BF16), pltpu.VMEM((hps, t, 2 * HEAD_DIM), BF16),
                        pltpu.VMEM((hps, tq, 2 * HEAD_DIM), BF16), pltpu.VMEM((hps, tq, 2 * HEAD_DIM), BF16),
                        pltpu.VMEM((hps, tq, tk), F32), pltpu.VMEM((hps, tq, tk), F32),
                        pltpu.VMEM((hps, tq, LANES), F32), pltpu.VMEM((hps, tq, 2 * HEAD_DIM), F32)],
        compiler_params=_params("parallel", "arbitrary", "arbitrary"),
        name="prompt_attention",
    )(q, q, k, v, f_cum2)


def _sample_attn_kernel(q_ref, kc_ref, vc_ref, kn_ref, vn_ref, f_ref, o_ref, *, past, n_heads):
    s_len = q_ref.shape[1]
    f_old = f_ref[0, 0:past, :]
    f_new = f_ref[0, past:past + s_len, :]
    row = lax.broadcasted_iota(jnp.int32, (s_len, s_len), 0)
    col = lax.broadcasted_iota(jnp.int32, (s_len, s_len), 1)
    for hh in range(n_heads):
        sl = slice(hh * HEAD_DIM, (hh + 1) * HEAD_DIM)
        qa = jnp.concatenate([q_ref[0, :, sl], _bias_columns(f_new, hh, False)], axis=-1)
        head_rows = pl.ds(hh, past, stride=n_heads)
        kca = jnp.concatenate([kc_ref[0, head_rows, :].astype(BF16), _bias_columns(f_old, hh, True)], axis=-1)
        kna = jnp.concatenate([kn_ref[0, :, sl], _bias_columns(f_new, hh, True)], axis=-1)
        s_c = _dot_nt(qa, kca)
        s_n = jnp.where(col <= row, _dot_nt(qa, kna), -jnp.inf)
        m = jnp.maximum(jnp.max(s_c, axis=-1, keepdims=True), jnp.max(s_n, axis=-1, keepdims=True))
        p_c = jnp.exp2(s_c - m)
        p_n = jnp.exp2(s_n - m)
        l = jnp.sum(p_c, axis=-1, keepdims=True) + jnp.sum(p_n, axis=-1, keepdims=True)
        o = (jnp.dot(p_c.astype(BF16), vc_ref[0, head_rows, :].astype(BF16), preferred_element_type=F32)
             + jnp.dot(p_n.astype(BF16), vn_ref[0, :, sl], preferred_element_type=F32))
        o_ref[0, :, sl] = (o * (1.0 / l)).astype(o_ref.dtype)


def _sample_attention(q, k_new, v_new, cache_k, cache_v, f_cum2, *, n_heads):
    b, s_len, d_att = q.shape
    past = cache_k.shape[1] // n_heads
    new = pl.BlockSpec((1, s_len, d_att), lambda bi: (bi, 0, 0))
    old = pl.BlockSpec((1, past * n_heads, HEAD_DIM), lambda bi: (bi, 0, 0))
    return pl.pallas_call(
        functools.partial(_sample_attn_kernel, past=past, n_heads=n_heads),
        out_shape=jax.ShapeDtypeStruct(q.shape, BF16),
        grid=(b,),
        in_specs=[new, old, old, new, new, pl.BlockSpec((1, past + s_len, n_heads), lambda bi: (bi, 0, 0))],
        out_specs=new,
        compiler_params=_params("parallel"),
        name="sample_attention",
    )(q, cache_k, cache_v, k_new, v_new, f_cum2)


def _s5_kernel(u_ref, h0re_ref, h0im_ref, wb_ref, wc_ref, pwre_ref, pwim_ref, d_ref, wglu_ref, bglu_ref,
               gout_ref, o_ref, hre_ref, him_ref, bu_ref, xre_ref, xim_ref, y_ref, *, tc):
    n_rows = u_ref.shape[1]
    n_chunks = n_rows // tc
    seg = tc // SUBLANES
    n_blk, ch = wb_ref.shape[0], wb_ref.shape[1]
    ns = wb_ref.shape[2] // 2

    @pl.when(pl.program_id(1) == 0)
    def _():
        hre_ref[...] = h0re_ref[...]
        him_ref[...] = h0im_ref[...]

    def b_proj(blk):
        bu_ref[blk % 2] = jnp.dot(u_ref[0, :, blk * ch:(blk + 1) * ch], wb_ref[blk], preferred_element_type=F32)

    sub = lax.broadcasted_iota(jnp.int32, (SUBLANES, ns), 0)
    tab = lambda r: slice(r * SUBLANES, (r + 1) * SUBLANES)
    b_proj(0)
    for blk in range(n_blk):
        if blk + 1 < n_blk:
            b_proj(blk + 1)
        buf = blk % 2
        cols = slice(blk * ns, (blk + 1) * ns)
        chs = slice(blk * ch, (blk + 1) * ch)
        for c in range(n_chunks):
            _s5_scan_chunk(c, tc, seg, ns, buf, cols, sub, tab, bu_ref, xre_ref, xim_ref, pwre_ref, pwim_ref,
                           hre_ref, him_ref)
        y = (jnp.dot(xre_ref[...].astype(BF16), wc_ref[blk, 0:ns, :], preferred_element_type=F32)
             + jnp.dot(xim_ref[...].astype(BF16), wc_ref[blk, ns:2 * ns, :], preferred_element_type=F32))
        y_ref[:, chs] = y + d_ref[:, chs] * u_ref[0, :, chs].astype(F32)

    y = y_ref[...]
    g = y * (0.5 * (1.0 + jnp.tanh(math.sqrt(2.0 / math.pi) * (y + 0.044715 * (y * y * y)))))
    gate = _sigmoid(jnp.dot(g.astype(BF16), wglu_ref[...], preferred_element_type=F32) + bglu_ref[...])
    out = _rms_rows(g * gate, gout_ref[...]).astype(BF16)
    unperm = _stream_permutation(tc, inverse=True)
    for c in range(n_chunks):
        o_ref[0, c * tc:(c + 1) * tc, :] = jnp.dot(unperm, out[c * tc:(c + 1) * tc],
                                                    preferred_element_type=F32).astype(o_ref.dtype)


def _s5_scan_chunk(c, tc, seg, ns, buf, cols, sub, tab, bu_ref, xre_ref, xim_ref, pwre_ref, pwim_ref,
                   hre_ref, him_ref):
    rows = lambda r: slice(c * tc + r * SUBLANES, c * tc + (r + 1) * SUBLANES)

    a_re, a_im = pwre_ref[tab(0), cols], pwim_ref[tab(0), cols]
    h_re = bu_ref[buf, rows(0), 0:ns]
    h_im = bu_ref[buf, rows(0), ns:2 * ns]
    xre_ref[rows(0), :] = h_re
    xim_ref[rows(0), :] = h_im
    for r in range(1, seg):
        h_re, h_im = _cmul_add(bu_ref[buf, rows(r), 0:ns], bu_ref[buf, rows(r), ns:2 * ns], a_re, a_im, h_re, h_im)
        xre_ref[rows(r), :] = h_re
        xim_ref[rows(r), :] = h_im

    s_re = jnp.where(sub == 0, hre_ref[0, :, cols], pltpu.roll(h_re, 1, 0))
    s_im = jnp.where(sub == 0, him_ref[0, :, cols], pltpu.roll(h_im, 1, 0))
    for shift in (1, 2, 4):
        row = _pow_row(seg * shift)
        m_re = jnp.where(sub >= shift, pwre_ref[tab(row), cols], 0.0)
        m_im = jnp.where(sub >= shift, pwim_ref[tab(row), cols], 0.0)
        s_re, s_im = _cmul_add(s_re, s_im, m_re, m_im, pltpu.roll(s_re, shift, 0), pltpu.roll(s_im, shift, 0))

    for r in range(seg):
        t_re, t_im = _cmul_add(xre_ref[rows(r), :], xim_ref[rows(r), :],
                               pwre_ref[tab(r), cols], pwim_ref[tab(r), cols], s_re, s_im)
        xre_ref[rows(r), :] = t_re
        xim_ref[rows(r), :] = t_im
    hre_ref[0, :, cols] = t_re[SUBLANES - 1:SUBLANES, :]
    him_ref[0, :, cols] = t_im[SUBLANES - 1:SUBLANES, :]


def _s5_mixer(u, h0_re, h0_im, wb, wc, pw_re, pw_im, d_skip, w_glu, b_glu, g_out, *, tc, chunks_per_step):
    b, t, d_ssm = u.shape
    n_state = h0_re.shape[-1]
    rows = tc * chunks_per_step
    assert tc // SUBLANES <= MAX_SEG and t % rows == 0
    state = pl.BlockSpec((1, 1, n_state), lambda bi, j: (bi, 0, 0))
    act = pl.BlockSpec((1, rows, d_ssm), lambda bi, j: (bi, j, 0))
    ns = wb.shape[2] // 2
    return pl.pallas_call(
        functools.partial(_s5_kernel, tc=tc),
        out_shape=(jax.ShapeDtypeStruct((b, t, d_ssm), BF16),
                   jax.ShapeDtypeStruct((b, 1, n_state), F32), jax.ShapeDtypeStruct((b, 1, n_state), F32)),
        grid=(b, t // rows),
        in_specs=[act, state, state, _resident(wb.shape), _resident(wc.shape), _resident(pw_re.shape),
                  _resident(pw_im.shape), _resident(d_skip.shape), _resident(w_glu.shape),
                  _resident(b_glu.shape), _resident(g_out.shape)],
        out_specs=(act, state, state),
        scratch_shapes=[pltpu.VMEM((2, rows, 2 * ns), F32), pltpu.VMEM((rows, ns), F32),
                        pltpu.VMEM((rows, ns), F32), pltpu.VMEM((rows, d_ssm), F32)],
        compiler_params=_params("parallel", "arbitrary"),
        name="s5_mixer",
    )(u, h0_re, h0_im, wb, wc, pw_re, pw_im, d_skip, w_glu, b_glu, g_out)


def _out_proj_kernel(att_ref, ssm_ref, x_ref, gt1_ref, sh2_ref, sc2_ref, gatt_ref, wout_ref, g2_ref,
                     x1_ref, h2_ref):
    nb, t, d = x_ref.shape
    d_att, d_ssm = att_ref.shape[-1], ssm_ref.shape[-1]
    halves = [(slice(0, nb), slice(h * (t // 2), (h + 1) * (t // 2))) for h in range(2)] if nb == 1 else \
             [(slice(h * (nb // 2), (h + 1) * (nb // 2)), slice(0, t)) for h in range(2)]
    for bs, ts in halves:
        hb, ht = bs.stop - bs.start, ts.stop - ts.start
        rows = hb * ht
        a = att_ref[bs, ts, :].astype(F32).reshape(rows, d_att)
        an = _rms_rows(a, gatt_ref[...]).astype(BF16)
        mix = (jnp.dot(an, wout_ref[0:d_att, :], preferred_element_type=F32)
               + jnp.dot(ssm_ref[bs, ts, :].reshape(rows, d_ssm), wout_ref[d_att:, :],
                         preferred_element_type=F32))
        x1 = x_ref[bs, ts, :] + gt1_ref[bs, 0] * mix.reshape(hb, ht, d)
        x1_ref[bs, ts, :] = x1
        h2 = _rms_rows(x1, g2_ref[...]) * (1.0 + sc2_ref[bs, 0]) + sh2_ref[bs, 0]
        h2_ref[bs, ts, :] = h2.astype(BF16)


def _out_proj(att, ssm, x, mod4, mod_row, g_att, w_out, g2, *, nb, t):
    n_seq, seq, d = x.shape
    row_blk = mod_row // nb
    act = lambda width: pl.BlockSpec((nb, t, width), lambda i, j: (i, j, 0))
    mod = lambda m: pl.BlockSpec((nb, 1, 1, d), lambda i, j, m=m: (row_blk + i, m, 0, 0))
    return pl.pallas_call(
        _out_proj_kernel,
        out_shape=(jax.ShapeDtypeStruct(x.shape, F32), jax.ShapeDtypeStruct(x.shape, BF16)),
        grid=(n_seq // nb, seq // t),
        in_specs=[act(att.shape[-1]), act(ssm.shape[-1]), act(d), mod(2), mod(3), mod(4),
                  _resident(g_att.shape), _resident(w_out.shape), _resident(g2.shape)],
        out_specs=(act(d), act(d)),
        compiler_params=_params("parallel", "parallel"),
        name="out_proj",
    )(att, ssm, x, mod4, mod4, mod4, g_att, w_out, g2)


def _mlp_kernel(h2_ref, x1_ref, gt2_ref, w1_ref, w2_ref, o_ref, *, sub):
    nb, t, d = x1_ref.shape
    j = pl.program_id(2)
    tf = w1_ref.shape[1]

    @pl.when(j == 0)
    def _():
        o_ref[...] = jnp.zeros(o_ref.shape, F32)

    h2 = h2_ref[...].reshape(nb * t, d)
    for s in range(tf // sub):
        a = jnp.dot(h2, w1_ref[:, s * sub:(s + 1) * sub], preferred_element_type=F32)
        r = jnp.maximum(a, 0.0)
        o_ref[...] += jnp.dot((r * r).astype(BF16), w2_ref[s * sub:(s + 1) * sub, :],
                              preferred_element_type=F32).reshape(nb, t, d)

    @pl.when(j == pl.num_programs(2) - 1)
    def _():
        o_ref[...] = x1_ref[...] + gt2_ref[:, 0] * o_ref[...]


def _mlp(h2, x1, mod4, mod_row, w1, w2, *, nb, t, tf, sub):
    n_seq, seq, d = x1.shape
    d_ff = w1.shape[1]
    row_blk = mod_row // nb
    act = lambda **kw: pl.BlockSpec((nb, t, d), lambda i, j, f: (i, j, 0), **kw)
    return pl.pallas_call(
        functools.partial(_mlp_kernel, sub=sub),
        out_shape=jax.ShapeDtypeStruct(x1.shape, F32),
        grid=(n_seq // nb, seq // t, d_ff // tf),
        in_specs=[act(), act(pipeline_mode=pl.Buffered(1)),
                  pl.BlockSpec((nb, 1, 1, d), lambda i, j, f: (row_blk + i, 5, 0, 0)),
                  pl.BlockSpec((d, tf), lambda i, j, f: (0, f)),
                  pl.BlockSpec((tf, d), lambda i, j, f: (f, 0))],
        out_specs=act(),
        compiler_params=_params("parallel", "parallel", "arbitrary"),
        name="mlp",
    )(h2, x1, mod4, w1, w2)


def kernel(x_prompt, x_sample, c_prompt, c_sample, cache_k, cache_v, cache_logf, state_ssm_re, state_ssm_im,
           w_ada, b_ada, g_norm1, w_in, g_q, g_k, b_f, log_dt, a_re, a_im, b_re, b_im, c_re, c_im, d_skip,
           w_glu, b_glu, g_att_out, g_ssm_out, w_out, g_norm2, w_ff1, w_ff2):
    depth = w_ada.shape[0]
    assert depth == 1, "single-layer step"
    bsz, seq, d = x_prompt.shape
    dec_b, dec_s, _ = x_sample.shape
    past = cache_k.shape[2]
    n_heads = cache_k.shape[3]
    d_att = n_heads * HEAD_DIM
    n_groups, n_state = a_re.shape[1], a_re.shape[2]
    mod_rows = 16
    assert dec_b + bsz <= mod_rows
    l = 0

    c_rows = jnp.concatenate([c_sample, c_prompt, jnp.zeros((mod_rows - dec_b - bsz, d), F32)], axis=0)
    mod4 = _modulation(c_rows, w_ada[l], b_ada[l][None]).reshape(mod_rows, N_MOD, 1, d)
    row_sample, row_prompt = 0, dec_b

    w_qkv = w_in[l, :, 0:3 * d_att].astype(BF16)
    w_f = jnp.pad(w_in[l, :, 3 * d_att:3 * d_att + n_heads], ((0, 0), (0, LANES - n_heads))).astype(BF16)
    w_u = w_in[l, :, 3 * d_att + n_heads:].astype(BF16)
    b_f_pad = jnp.pad(b_f[l], (0, LANES - n_heads))[None]
    g1 = g_norm1[l][None]
    gq, gk = g_q[l][None], g_k[l][None]

    wb32, wc32, pw_re, pw_im = _s5_params(log_dt[l], a_re[l], a_im[l], b_re[l], b_im[l], c_re[l], c_im[l])
    wb, wc = wb32.astype(BF16), wc32.astype(BF16)
    pw_re = jnp.repeat(pw_re.reshape(N_POW, n_groups * n_state), SUBLANES, axis=0)
    pw_im = jnp.repeat(pw_im.reshape(N_POW, n_groups * n_state), SUBLANES, axis=0)
    d_row = d_skip[l][None]
    w_glu_b = w_glu[l].astype(BF16)
    b_glu_row = b_glu[l][None]
    g_ssm_row = g_ssm_out[l][None]
    g_att_row = g_att_out[l][None]
    w_out_b = w_out[l].astype(BF16)
    g2 = g_norm2[l][None]
    w1 = w_ff1[l].astype(BF16)
    w2 = w_ff2[l].astype(BF16)

    tm = 512
    tc_prompt = SUBLANES * MAX_SEG
    tc_sample = dec_s

    q_p, k32_p, kbf_p, v32_p, vbf_p, logf_p, u_p = _in_proj(
        x_prompt, mod4, row_prompt, g1, w_qkv, w_u, w_f, gq, gk, b_f_pad,
        nb=1, t=tm, n_heads=n_heads, s5_chunk=tc_prompt)
    fcum_p = _forget_cumsum(logf_p)
    att_p = _prompt_attention(q_p, kbf_p, vbf_p, fcum_p, n_heads=n_heads, tq=1024, tk=512, heads_per_step=1)
    zeros_state = jnp.zeros((bsz, 1, n_groups * n_state), F32)
    ssm_p, hre_p, him_p = _s5_mixer(u_p, zeros_state, zeros_state, wb, wc, pw_re, pw_im, d_row, w_glu_b,
                                    b_glu_row, g_ssm_row, tc=tc_prompt, chunks_per_step=tm // tc_prompt)
    x1_p, h2_p = _out_proj(att_p, ssm_p, x_prompt, mod4, row_prompt, g_att_row, w_out_b, g2, nb=1, t=tm)
    y_p = _mlp(h2_p, x1_p, mod4, row_prompt, w1, w2, nb=1, t=tm, tf=2048, sub=1024)

    q_s, k32_s, kbf_s, v32_s, vbf_s, logf_s, u_s = _in_proj(
        x_sample, mod4, row_sample, g1, w_qkv, w_u, w_f, gq, gk, b_f_pad,
        nb=dec_b, t=dec_s, n_heads=n_heads, s5_chunk=tc_sample)
    logf_all = jnp.concatenate([cache_logf[l], logf_s], axis=1)
    fcum_s = _forget_cumsum(logf_all)
    att_s = _sample_attention(q_s, kbf_s, vbf_s, cache_k[l].reshape(dec_b, past * n_heads, HEAD_DIM),
                              cache_v[l].reshape(dec_b, past * n_heads, HEAD_DIM), fcum_s, n_heads=n_heads)
    ssm_s, hre_s, him_s = _s5_mixer(u_s, state_ssm_re[l].reshape(dec_b, 1, -1),
                                    state_ssm_im[l].reshape(dec_b, 1, -1), wb, wc, pw_re, pw_im, d_row,
                                    w_glu_b, b_glu_row, g_ssm_row, tc=tc_sample, chunks_per_step=1)
    x1_s, h2_s = _out_proj(att_s, ssm_s, x_sample, mod4, row_sample, g_att_row, w_out_b, g2,
                           nb=dec_b, t=dec_s)
    y_s = _mlp(h2_s, x1_s, mod4, row_sample, w1, w2, nb=dec_b, t=dec_s, tf=2048, sub=1024)

    states = lambda a, n: a.reshape(1, n, n_groups, n_state)
    heads = lambda a, n, s: a.reshape(1, n, s, n_heads, HEAD_DIM)
    return (y_p, y_s,
            heads(k32_p, bsz, seq), heads(v32_p, bsz, seq), logf_p[None],
            states(hre_p, bsz), states(him_p, bsz),
            heads(k32_s, dec_b, dec_s), heads(v32_s, dec_b, dec_s), logf_s[None],
            states(hre_s, dec_b), states(him_s, dec_b))
```

```python
import functools
import math

import jax
import jax.numpy as jnp
from jax import lax
from jax.experimental import pallas as pl
from jax.experimental.pallas import tpu as pltpu

F32 = jnp.float32
BF16 = jnp.bfloat16
HIGHEST = lax.Precision.HIGHEST

HEAD_DIM = 128
SSM_GROUP = 16
N_MOD = 6
EPS = 1e-6
LOG2E = 1.4426950408889634

LANES = 128
SUBLANES = 8
GROUPS_PER_BLOCK = 16
MAX_SEG = 32
N_POW = MAX_SEG + 2
VMEM_LIMIT = 56 * 1024 * 1024


def _params(*sem):
    return pltpu.CompilerParams(dimension_semantics=sem, vmem_limit_bytes=VMEM_LIMIT)


def _resident(shape):
    nd = len(shape)
    return pl.BlockSpec(shape, lambda *_: (0,) * nd, pipeline_mode=pl.Buffered(1))


def _sigmoid(x):
    return 1.0 / (1.0 + jnp.exp(-x))


def _rms_rows(x, g):
    return x * lax.rsqrt(jnp.mean(x * x, axis=-1, keepdims=True) + EPS) * g


def _dot_nt(a, b):
    return lax.dot_general(a, b, (((1,), (1,)), ((), ())), preferred_element_type=F32)


def _cmul_add(x_re, x_im, a_re, a_im, h_re, h_im):
    return x_re + (a_re * h_re - a_im * h_im), x_im + (a_re * h_im + a_im * h_re)


def _stream_permutation(n, inverse):
    i = lax.broadcasted_iota(jnp.int32, (n, n), 0)
    j = lax.broadcasted_iota(jnp.int32, (n, n), 1)
    if inverse:
        i, j = j, i
    src = (i & (SUBLANES - 1)) * (n // SUBLANES) + (i >> 3)
    return jnp.where(j == src, 1.0, 0.0).astype(BF16)


def _split_bf16(x):
    hi = x.astype(BF16)
    return hi, (x - hi.astype(F32)).astype(BF16)


def _mod_kernel(c_ref, w_ref, b_ref, o_ref):
    c = c_ref[...]
    rows = c.shape[0]
    s_hi, s_lo = _split_bf16(c * _sigmoid(c))
    w_hi, w_lo = _split_bf16(w_ref[...])
    first = jnp.dot(jnp.concatenate([s_hi, s_lo], axis=0), w_hi, preferred_element_type=F32)
    second = jnp.dot(s_hi, w_lo, preferred_element_type=F32)
    o_ref[...] = first[0:rows] + (first[rows:2 * rows] + second) + b_ref[...]


def _modulation(c_rows, w_ada, b_ada):
    rows, d = c_rows.shape
    n = w_ada.shape[1]
    tn = 1024
    return pl.pallas_call(
        _mod_kernel,
        out_shape=jax.ShapeDtypeStruct((rows, n), F32),
        grid=(n // tn,),
        in_specs=[pl.BlockSpec((rows, d), lambda j: (0, 0)),
                  pl.BlockSpec((d, tn), lambda j: (0, j)),
                  pl.BlockSpec((1, tn), lambda j: (0, j))],
        out_specs=pl.BlockSpec((rows, tn), lambda j: (0, j)),
        compiler_params=_params("arbitrary"),
        name="modulation",
    )(c_rows, w_ada, b_ada)


def _zoh(log_dt, a_re, a_im):
    dt = jnp.exp(log_dt)
    lam_re = jnp.minimum(a_re, -1e-4)
    lam_im = a_im
    mag = jnp.exp(lam_re * dt)
    ang = lam_im * dt
    ab_re = mag * jnp.cos(ang)
    ab_im = mag * jnp.sin(ang)
    den = lam_re * lam_re + lam_im * lam_im
    z_re = ((ab_re - 1.0) * lam_re + ab_im * lam_im) / den
    z_im = (ab_im * lam_re - (ab_re - 1.0) * lam_im) / den
    return ab_re, ab_im, z_re, z_im


def _s5_param_kernel(logdt_ref, are_ref, aim_ref, logdt2_ref, are2_ref, aim2_ref, bre_ref, bim_ref, cre_ref,
                     cim_ref, wb_ref, wc_ref, pwre_ref, pwim_ref):
    n_groups, n_ch, n_st = bre_ref.shape
    gpb = GROUPS_PER_BLOCK
    ns = gpb * n_st
    _, _, z_re, z_im = _zoh(logdt_ref[...], are_ref[...], aim_ref[...])
    ab_re, ab_im, _, _ = _zoh(logdt2_ref[...], are2_ref[...], aim2_ref[...])
    b_re = bre_ref[...]
    b_im = bim_ref[...]
    zb_re = z_re * b_re - z_im * b_im
    zb_im = z_re * b_im + z_im * b_re
    wb_ref[...] = jnp.zeros(wb_ref.shape, F32)
    wc_ref[...] = jnp.zeros(wc_ref.shape, F32)
    for g in range(n_groups):
        blk, gi = divmod(g, gpb)
        chs = slice(gi * n_ch, (gi + 1) * n_ch)
        sts = slice(gi * n_st, (gi + 1) * n_st)
        sts_im = slice(ns + gi * n_st, ns + (gi + 1) * n_st)
        wb_ref[blk, chs, sts] = zb_re[g]
        wb_ref[blk, chs, sts_im] = zb_im[g]
        wc_ref[blk, sts, chs] = cre_ref[g]
        wc_ref[blk, sts_im, chs] = -cim_ref[g]
    p_re, p_im = ab_re, ab_im
    for k in range(MAX_SEG):
        pwre_ref[k] = p_re
        pwim_ref[k] = p_im
        if k + 1 < MAX_SEG:
            p_re, p_im = p_re * ab_re - p_im * ab_im, p_re * ab_im + p_im * ab_re
    for k in range(MAX_SEG, N_POW):
        p_re, p_im = p_re * p_re - p_im * p_im, 2.0 * (p_re * p_im)
        pwre_ref[k] = p_re
        pwim_ref[k] = p_im


def _s5_params(log_dt, a_re, a_im, b_re, b_im, c_re, c_im):
    g, n, c = b_re.shape
    n_blk = g // GROUPS_PER_BLOCK
    ch, ns = GROUPS_PER_BLOCK * c, GROUPS_PER_BLOCK * n
    f = lambda shape: jax.ShapeDtypeStruct(shape, F32)
    return pl.pallas_call(
        _s5_param_kernel,
        out_shape=(f((n_blk, ch, 2 * ns)), f((n_blk, 2 * ns, ch)), f((N_POW, g, n)), f((N_POW, g, n))),
        compiler_params=_params(),
        name="s5_params",
    )(log_dt.reshape(g, 1, 1), a_re.reshape(g, 1, n), a_im.reshape(g, 1, n), log_dt.reshape(g, 1), a_re, a_im,
      jnp.swapaxes(b_re, 1, 2), jnp.swapaxes(b_im, 1, 2), jnp.swapaxes(c_re, 1, 2), jnp.swapaxes(c_im, 1, 2))


def _pow_row(seg_rows):
    if seg_rows <= MAX_SEG:
        return seg_rows - 1
    return MAX_SEG - 1 + int(math.log2(seg_rows // MAX_SEG))


def _in_proj_kernel(x_ref, sh_ref, sc_ref, g1_ref, wqkv_ref, wu_ref, wf_ref, gq_ref, gk_ref, bf_ref,
                    q_ref, k32_ref, kbf_ref, v32_ref, vbf_ref, logf_ref, u_ref,
                    *, n_heads, q_scale, s5_chunk):
    nb, t, d = x_ref.shape
    rows = nb * t
    d_att = n_heads * HEAD_DIM
    x = x_ref[...]
    y = _rms_rows(x, g1_ref[...])
    h = (y * (1.0 + sc_ref[:, 0]) + sh_ref[:, 0]).reshape(rows, d)
    hb = h.astype(BF16)

    q = jnp.dot(hb, wqkv_ref[:, 0:d_att], preferred_element_type=F32)
    for hh in range(n_heads):
        sl = slice(hh * HEAD_DIM, (hh + 1) * HEAD_DIM)
        qn = _rms_rows(q[:, sl], gq_ref[...]) * q_scale
        q_ref[:, :, sl] = qn.reshape(nb, t, HEAD_DIM).astype(BF16)

    k = jnp.dot(hb, wqkv_ref[:, d_att:2 * d_att], preferred_element_type=F32)
    for hh in range(n_heads):
        sl = slice(hh * HEAD_DIM, (hh + 1) * HEAD_DIM)
        kn = _rms_rows(k[:, sl], gk_ref[...])
        kbf_ref[:, :, sl] = kn.reshape(nb, t, HEAD_DIM).astype(BF16)
        for bi in range(nb):
            k32_ref[bi, pl.ds(hh, t, stride=n_heads), :] = kn[bi * t:(bi + 1) * t]

    v = jnp.dot(hb, wqkv_ref[:, 2 * d_att:3 * d_att], preferred_element_type=F32)
    for hh in range(n_heads):
        sl = slice(hh * HEAD_DIM, (hh + 1) * HEAD_DIM)
        for bi in range(nb):
            v32_ref[bi, pl.ds(hh, t, stride=n_heads), :] = v[bi * t:(bi + 1) * t, sl]
    vbf_ref[...] = v.reshape(nb, t, d_att).astype(BF16)

    f = jnp.dot(hb, wf_ref[...], preferred_element_type=F32) + bf_ref[...]
    logf = jnp.minimum(f, 0.0) - jnp.log1p(jnp.exp(-jnp.abs(f)))
    logf_ref[...] = logf[:, 0:n_heads].reshape(nb, t, n_heads)

    u = jnp.dot(hb, wu_ref[...], preferred_element_type=F32).astype(BF16)
    perm = _stream_permutation(s5_chunk, inverse=False)
    for c in range(rows // s5_chunk):
        uc = jnp.dot(perm, u[c * s5_chunk:(c + 1) * s5_chunk], preferred_element_type=F32).astype(BF16)
        if nb == 1:
            u_ref[0, c * s5_chunk:(c + 1) * s5_chunk, :] = uc
        else:
            per = t // s5_chunk
            u_ref[c // per, (c % per) * s5_chunk:(c % per + 1) * s5_chunk, :] = uc


def _in_proj(x, mod4, mod_row, g1, w_qkv, w_u, w_f, g_q, g_k, b_f_pad, *, nb, t, n_heads, s5_chunk):
    n_seq, seq, d = x.shape
    d_att = n_heads * HEAD_DIM
    d_ssm = w_u.shape[1]
    assert t % s5_chunk == 0
    grid = (n_seq // nb, seq // t)
    row_blk = mod_row // nb
    act = lambda width: pl.BlockSpec((nb, t, width), lambda i, j: (i, j, 0))
    act4 = pl.BlockSpec((nb, t * n_heads, HEAD_DIM), lambda i, j: (i, j, 0))
    mod = lambda m: pl.BlockSpec((nb, 1, 1, d), lambda i, j, m=m: (row_blk + i, m, 0, 0))
    sds = lambda width, dt: jax.ShapeDtypeStruct((n_seq, seq, width), dt)
    sds4 = jax.ShapeDtypeStruct((n_seq, seq * n_heads, HEAD_DIM), F32)
    kern = functools.partial(_in_proj_kernel, n_heads=n_heads, q_scale=HEAD_DIM ** -0.5 * LOG2E,
                             s5_chunk=s5_chunk)
    return pl.pallas_call(
        kern,
        out_shape=(sds(d_att, BF16), sds4, sds(d_att, BF16), sds4, sds(d_att, BF16),
                   sds(n_heads, F32), sds(d_ssm, BF16)),
        grid=grid,
        in_specs=[act(d), mod(0), mod(1), _resident(g1.shape), _resident(w_qkv.shape), _resident(w_u.shape),
                  _resident(w_f.shape), _resident(g_q.shape), _resident(g_k.shape), _resident(b_f_pad.shape)],
        out_specs=(act(d_att), act4, act(d_att), act4, act(d_att), act(n_heads), act(d_ssm)),
        compiler_params=_params("parallel", "parallel"),
        name="in_proj",
    )(x, mod4, mod4, g1, w_qkv, w_u, w_f, g_q, g_k, b_f_pad)


def _cumsum_kernel(x_ref, o_ref, *, n_chunks):
    x = x_ref[...]
    n = x.shape[0]
    li = lax.broadcasted_iota(jnp.int32, (LANES, LANES), 0)
    lj = lax.broadcasted_iota(jnp.int32, (LANES, LANES), 1)
    tri = jnp.where(li <= lj, 1.0, 0.0)
    within = jnp.dot(x, tri, precision=HIGHEST, preferred_element_type=F32)
    tot = jnp.dot(x, jnp.ones((LANES, LANES), F32), precision=HIGHEST, preferred_element_type=F32)
    r = lax.broadcasted_iota(jnp.int32, (n, n), 0)
    c = lax.broadcasted_iota(jnp.int32, (n, n), 1)
    seq_of = lambda i: jnp.floor((i.astype(F32) + 0.5) * (1.0 / n_chunks))
    earlier = jnp.where(seq_of(r) == seq_of(c), jnp.where(c < r, 1.0, 0.0), 0.0)
    before = jnp.dot(earlier, tot, precision=HIGHEST, preferred_element_type=F32)
    o_ref[...] = (within + before) * LOG2E


def _forget_cumsum(logf):
    b, length, h = logf.shape
    n_chunks = -(-length // LANES)
    rows = jnp.swapaxes(logf, 1, 2).reshape(b * h, length)
    rows = jnp.pad(rows, ((0, 0), (0, n_chunks * LANES - length)))
    out = pl.pallas_call(
        functools.partial(_cumsum_kernel, n_chunks=n_chunks),
        out_shape=jax.ShapeDtypeStruct((b * h * n_chunks, LANES), F32),
        compiler_params=_params(),
        name="forget_cumsum",
    )(rows.reshape(b * h * n_chunks, LANES))
    by_head = out.reshape(b, h, n_chunks * LANES)
    return jnp.swapaxes(by_head[:, :, :length], 1, 2), by_head


def _head_column(f_rows, head):
    lane_h = lax.broadcasted_iota(jnp.int32, f_rows.shape, 1)
    return jnp.sum(jnp.where(lane_h == head, f_rows, 0.0), axis=-1, keepdims=True)


def _bias_columns(f_rows, head, key_side):
    col = _head_column(f_rows, head)
    hi = col.astype(BF16).astype(F32)
    rest = col - hi
    mid = rest.astype(BF16).astype(F32)
    lo = rest - mid
    lane = lax.broadcasted_iota(jnp.int32, (f_rows.shape[0], LANES), 1)
    if key_side:
        split = jnp.where(lane == 3, -hi, jnp.where(lane == 4, -mid, jnp.where(lane == 5, -lo, 0.0)))
        vals = jnp.where(lane < 3, 1.0, split)
    else:
        split = jnp.where(lane == 0, hi, jnp.where(lane == 1, mid, jnp.where(lane == 2, lo, 0.0)))
        vals = jnp.where((lane >= 3) & (lane < 6), 1.0, split)
    return vals.astype(BF16)


def _attn_kernel(q_ref, qn_ref, k_ref, v_ref, f_ref, o_ref, kaug_ref, vaug_ref, qa_ref, qnext_ref, s0_ref, s1_ref,
                 m_ref, acc_ref, *, tq, tk, build_rows, heads_per_step):
    qi = pl.program_id(2)
    t = k_ref.shape[1]
    dh = HEAD_DIM
    hps = heads_per_step
    head0 = pl.program_id(1) * hps

    def first_scores(src_ref, row0, dst_ref):
        f_rows = f_ref[0, pl.ds(row0, tq), :]
        for g in range(hps):
            qa = jnp.concatenate([src_ref[0, :, g * dh:(g + 1) * dh], _bias_columns(f_rows, head0 + g, False)],
                                 axis=-1)
            dst_ref[g] = qa
            s0_ref[g] = _dot_nt(qa, kaug_ref[g, 0:tk, :])

    def scores(c, s_ref, r0=0):
        off = pl.multiple_of(c * tk, tk)
        for g in range(hps):
            s_ref[g, r0:, :] = _dot_nt(qa_ref[g, r0:, :], kaug_ref[g, pl.ds(off, tk), :])

    @pl.when(qi == 0)
    def _first_block():
        for g in range(hps):
            kaug_ref[g, :, 0:dh] = k_ref[0, :, g * dh:(g + 1) * dh]
            vaug_ref[g, :, 0:dh] = v_ref[0, :, g * dh:(g + 1) * dh]
            vaug_ref[g, :, dh:2 * dh] = jnp.ones((t, dh), BF16)

        def piece(i, carry):
            r0 = pl.multiple_of(i * build_rows, build_rows)
            f_rows = f_ref[0, pl.ds(r0, build_rows), :]
            for g in range(hps):
                kaug_ref[g, pl.ds(r0, build_rows), dh:2 * dh] = _bias_columns(f_rows, head0 + g, True)
            return carry

        lax.fori_loop(0, t // build_rows, piece, 0)
        first_scores(q_ref, 0, qa_ref)

    @pl.when(qi > 0)
    def _take_prepared_queries():
        qa_ref[...] = qnext_ref[...]

    q0 = pl.multiple_of(qi * tq, tq)
    m_ref[...] = jnp.full(m_ref.shape, -jnp.inf, F32)
    acc_ref[...] = jnp.zeros(acc_ref.shape, F32)

    def absorb(c, s_ref, masked, r0=0):
        off = pl.multiple_of(c * tk, tk)
        for g in range(hps):
            s = s_ref[g, r0:, :]
            if masked:
                ahead = (lax.broadcasted_iota(jnp.int32, s.shape, 1)
                         - lax.broadcasted_iota(jnp.int32, s.shape, 0))
                s = jnp.where(ahead <= q0 + r0 - off, s, -jnp.inf)
            m_old = m_ref[g, r0:, :]
            m_new = jnp.maximum(m_old, jnp.max(s, axis=-1, keepdims=True))
            alpha = jnp.exp2(m_old - m_new)
            p = jnp.exp2(s - jnp.tile(m_new, (1, tk // LANES)))
            pv = jnp.dot(p.astype(BF16), vaug_ref[g, pl.ds(off, tk), :], preferred_element_type=F32)
            acc_ref[g, r0:, :] = jnp.tile(alpha, (1, 2)) * acc_ref[g, r0:, :] + pv
            m_ref[g, r0:, :] = m_new

    n_full = 2 * qi

    def pair(c):
        scores(c + 1, s1_ref)
        absorb(c, s0_ref, False)
        scores(c + 2, s0_ref)
        absorb(c + 1, s1_ref, False)

    def two_pairs(p, carry):
        pair(4 * p)
        pair(4 * p + 2)
        return carry

    lax.fori_loop(0, qi // 2, two_pairs, 0)

    @pl.when(qi % 2 == 1)
    def _odd_pair():
        pair(n_full - 2)

    scores(n_full + 1, s1_ref, r0=tk)
    absorb(n_full, s0_ref, True)
    first_scores(qn_ref, pl.multiple_of(jnp.minimum(qi + 1, pl.num_programs(2) - 1) * tq, tq), qnext_ref)
    absorb(n_full + 1, s1_ref, True, r0=tk)

    for g in range(hps):
        acc = acc_ref[g]
        o_ref[0, :, g * dh:(g + 1) * dh] = (acc[:, 0:dh] * (1.0 / acc[:, dh:2 * dh])).astype(o_ref.dtype)


def _prompt_attention(q, k, v, f_cum2, *, n_heads, tq, tk, heads_per_step):
    b, t, _ = q.shape
    assert t % tq == 0 and tq == 2 * tk and n_heads % heads_per_step == 0
    hps = heads_per_step
    width = hps * HEAD_DIM
    nq = t // tq
    q_blk = pl.BlockSpec((1, tq, width), lambda bi, h, i: (bi, i, h))
    qn_blk = pl.BlockSpec((1, tq, width), lambda bi, h, i: (bi, jnp.minimum(i + 1, nq - 1), h))
    kv_blk = pl.BlockSpec((1, t, width), lambda bi, h, i: (bi, 0, h))
    f_blk = pl.BlockSpec((1, t, n_heads), lambda bi, h, i: (bi, 0, 0), pipeline_mode=pl.Buffered(1))
    return pl.pallas_call(
        functools.partial(_attn_kernel, tq=tq, tk=tk, build_rows=1024, heads_per_step=hps),
        out_shape=jax.ShapeDtypeStruct(q.shape, BF16),
        grid=(b, n_heads // hps, nq),
        in_specs=[q_blk, qn_blk, kv_blk, kv_blk, f_blk],
        out_specs=q_blk,
        scratch_shapes=[pltpu.VMEM((hps, t, 2 * HEAD_DIM), BF16), pltpu.VMEM((hps, t, 2 * HEAD_DIM), BF16),
                        pltpu.VMEM((hps, tq, 2 * HEAD_DIM), BF16), pltpu.VMEM((hps, tq, 2 * HEAD_DIM), BF16),
                        pltpu.VMEM((hps, tq, tk), F32), pltpu.VMEM((hps, tq, tk), F32),
                        pltpu.VMEM((hps, tq, LANES), F32), pltpu.VMEM((hps, tq, 2 * HEAD_DIM), F32)],
        compiler_params=_params("parallel", "arbitrary", "arbitrary"),
        name="prompt_attention",
    )(q, q, k, v, f_cum2)


def _sample_attn_kernel(q_ref, kc_ref, vc_ref, kn_ref, vn_ref, fq_ref, fk_ref, o_ref, *, past, n_heads):
    s_len = q_ref.shape[1]
    f_new = fq_ref[0]
    row = lax.broadcasted_iota(jnp.int32, (s_len, s_len), 0)
    col = lax.broadcasted_iota(jnp.int32, (s_len, s_len), 1)
    for hh in range(n_heads):
        sl = slice(hh * HEAD_DIM, (hh + 1) * HEAD_DIM)
        q = q_ref[0, :, sl]
        fq = _head_column(f_new, hh)
        fk = fk_ref[0, hh:hh + 1, :]
        head_rows = pl.ds(hh, past, stride=n_heads)
        s_c = _dot_nt(q, kc_ref[0, head_rows, :].astype(BF16)) + fq - fk[:, 0:past]
        s_n = jnp.where(col <= row, _dot_nt(q, kn_ref[0, :, sl]) + fq - fk[:, past:past + s_len], -jnp.inf)
        m = jnp.maximum(jnp.max(s_c, axis=-1, keepdims=True), jnp.max(s_n, axis=-1, keepdims=True))
        p_c = jnp.exp2(s_c - m)
        p_n = jnp.exp2(s_n - m)
        l = jnp.sum(p_c, axis=-1, keepdims=True) + jnp.sum(p_n, axis=-1, keepdims=True)
        o = (jnp.dot(p_c.astype(BF16), vc_ref[0, head_rows, :].astype(BF16), preferred_element_type=F32)
             + jnp.dot(p_n.astype(BF16), vn_ref[0, :, sl], preferred_element_type=F32))
        o_ref[0, :, sl] = (o * (1.0 / l)).astype(o_ref.dtype)


def _sample_attention(q, k_new, v_new, cache_k, cache_v, f_cum2, f_by_head, *, n_heads):
    b, s_len, d_att = q.shape
    past = cache_k.shape[1] // n_heads
    assert past % s_len == 0
    new = pl.BlockSpec((1, s_len, d_att), lambda bi: (bi, 0, 0))
    old = pl.BlockSpec((1, past * n_heads, HEAD_DIM), lambda bi: (bi, 0, 0))
    return pl.pallas_call(
        functools.partial(_sample_attn_kernel, past=past, n_heads=n_heads),
        out_shape=jax.ShapeDtypeStruct(q.shape, BF16),
        grid=(b,),
        in_specs=[new, old, old, new, new,
                  pl.BlockSpec((1, s_len, n_heads), lambda bi: (bi, past // s_len, 0)),
                  pl.BlockSpec((1, n_heads, f_by_head.shape[-1]), lambda bi: (bi, 0, 0))],
        out_specs=new,
        compiler_params=_params("parallel"),
        name="sample_attention",
    )(q, cache_k, cache_v, k_new, v_new, f_cum2, f_by_head)


def _s5_kernel(u_ref, h0re_ref, h0im_ref, wb_ref, wc_ref, pwre_ref, pwim_ref, d_ref, wglu_ref, bglu_ref,
               gout_ref, o_ref, hre_ref, him_ref, bu_ref, xre_ref, xim_ref, y_ref, *, tc):
    n_rows = u_ref.shape[1]
    n_chunks = n_rows // tc
    seg = tc // SUBLANES
    n_blk, ch = wb_ref.shape[0], wb_ref.shape[1]
    ns = wb_ref.shape[2] // 2

    @pl.when(pl.program_id(1) == 0)
    def _():
        hre_ref[...] = h0re_ref[...]
        him_ref[...] = h0im_ref[...]

    def b_proj(blk):
        bu_ref[blk % 2] = jnp.dot(u_ref[0, :, blk * ch:(blk + 1) * ch], wb_ref[blk], preferred_element_type=F32)

    sub = lax.broadcasted_iota(jnp.int32, (SUBLANES, ns), 0)
    tab = lambda r: slice(r * SUBLANES, (r + 1) * SUBLANES)
    b_proj(0)
    for blk in range(n_blk):
        if blk + 1 < n_blk:
            b_proj(blk + 1)
        buf = blk % 2
        cols = slice(blk * ns, (blk + 1) * ns)
        chs = slice(blk * ch, (blk + 1) * ch)
        for c in range(n_chunks):
            _s5_scan_chunk(c, tc, seg, ns, buf, cols, sub, tab, bu_ref, xre_ref, xim_ref, pwre_ref, pwim_ref,
                           hre_ref, him_ref)
        y = (jnp.dot(xre_ref[...].astype(BF16), wc_ref[blk, 0:ns, :], preferred_element_type=F32)
             + jnp.dot(xim_ref[...].astype(BF16), wc_ref[blk, ns:2 * ns, :], preferred_element_type=F32))
        y_ref[:, chs] = y + d_ref[:, chs] * u_ref[0, :, chs].astype(F32)

    y = y_ref[...]
    g = y * (0.5 * (1.0 + jnp.tanh(math.sqrt(2.0 / math.pi) * (y + 0.044715 * (y * y * y)))))
    gate = _sigmoid(jnp.dot(g.astype(BF16), wglu_ref[...], preferred_element_type=F32) + bglu_ref[...])
    out = _rms_rows(g * gate, gout_ref[...]).astype(BF16)
    unperm = _stream_permutation(tc, inverse=True)
    for c in range(n_chunks):
        o_ref[0, c * tc:(c + 1) * tc, :] = jnp.dot(unperm, out[c * tc:(c + 1) * tc],
                                                    preferred_element_type=F32).astype(o_ref.dtype)


def _s5_scan_chunk(c, tc, seg, ns, buf, cols, sub, tab, bu_ref, xre_ref, xim_ref, pwre_ref, pwim_ref,
                   hre_ref, him_ref):
    rows = lambda r: slice(c * tc + r * SUBLANES, c * tc + (r + 1) * SUBLANES)

    a_re, a_im = pwre_ref[tab(0), cols], pwim_ref[tab(0), cols]
    h_re = bu_ref[buf, rows(0), 0:ns]
    h_im = bu_ref[buf, rows(0), ns:2 * ns]
    xre_ref[rows(0), :] = h_re
    xim_ref[rows(0), :] = h_im
    for r in range(1, seg):
        h_re, h_im = _cmul_add(bu_ref[buf, rows(r), 0:ns], bu_ref[buf, rows(r), ns:2 * ns], a_re, a_im, h_re, h_im)
        xre_ref[rows(r), :] = h_re
        xim_ref[rows(r), :] = h_im

    s_re = jnp.where(sub == 0, hre_ref[0, :, cols], pltpu.roll(h_re, 1, 0))
    s_im = jnp.where(sub == 0, him_ref[0, :, cols], pltpu.roll(h_im, 1, 0))
    for shift in (1, 2, 4):
        row = _pow_row(seg * shift)
        m_re = jnp.where(sub >= shift, pwre_ref[tab(row), cols], 0.0)
        m_im = jnp.where(sub >= shift, pwim_ref[tab(row), cols], 0.0)
        s_re, s_im = _cmul_add(s_re, s_im, m_re, m_im, pltpu.roll(s_re, shift, 0), pltpu.roll(s_im, shift, 0))

    for r in range(seg):
        t_re, t_im = _cmul_add(xre_ref[rows(r), :], xim_ref[rows(r), :],
                               pwre_ref[tab(r), cols], pwim_ref[tab(r), cols], s_re, s_im)
        xre_ref[rows(r), :] = t_re
        xim_ref[rows(r), :] = t_im
    hre_ref[0, :, cols] = t_re[SUBLANES - 1:SUBLANES, :]
    him_ref[0, :, cols] = t_im[SUBLANES - 1:SUBLANES, :]


def _s5_mixer(u, h0_re, h0_im, wb, wc, pw_re, pw_im, d_skip, w_glu, b_glu, g_out, *, tc, chunks_per_step):
    b, t, d_ssm = u.shape
    n_state = h0_re.shape[-1]
    rows = tc * chunks_per_step
    assert tc // SUBLANES <= MAX_SEG and t % rows == 0
    state = pl.BlockSpec((1, 1, n_state), lambda bi, j: (bi, 0, 0))
    act = pl.BlockSpec((1, rows, d_ssm), lambda bi, j: (bi, j, 0))
    ns = wb.shape[2] // 2
    return pl.pallas_call(
        functools.partial(_s5_kernel, tc=tc),
        out_shape=(jax.ShapeDtypeStruct((b, t, d_ssm), BF16),
                   jax.ShapeDtypeStruct((b, 1, n_state), F32), jax.ShapeDtypeStruct((b, 1, n_state), F32)),
        grid=(b, t // rows),
        in_specs=[act, state, state, _resident(wb.shape), _resident(wc.shape), _resident(pw_re.shape),
                  _resident(pw_im.shape), _resident(d_skip.shape), _resident(w_glu.shape),
                  _resident(b_glu.shape), _resident(g_out.shape)],
        out_specs=(act, state, state),
        scratch_shapes=[pltpu.VMEM((2, rows, 2 * ns), F32), pltpu.VMEM((rows, ns), F32),
                        pltpu.VMEM((rows, ns), F32), pltpu.VMEM((rows, d_ssm), F32)],
        compiler_params=_params("parallel", "arbitrary"),
        name="s5_mixer",
    )(u, h0_re, h0_im, wb, wc, pw_re, pw_im, d_skip, w_glu, b_glu, g_out)


def _out_proj_kernel(att_ref, ssm_ref, x_ref, gt1_ref, sh2_ref, sc2_ref, gatt_ref, wout_ref, g2_ref,
                     x1_ref, h2_ref):
    nb, t, d = x_ref.shape
    d_att, d_ssm = att_ref.shape[-1], ssm_ref.shape[-1]
    halves = [(slice(0, nb), slice(h * (t // 2), (h + 1) * (t // 2))) for h in range(2)] if nb == 1 else \
             [(slice(h * (nb // 2), (h + 1) * (nb // 2)), slice(0, t)) for h in range(2)]
    for bs, ts in halves:
        hb, ht = bs.stop - bs.start, ts.stop - ts.start
        rows = hb * ht
        a = att_ref[bs, ts, :].astype(F32).reshape(rows, d_att)
        an = _rms_rows(a, gatt_ref[...]).astype(BF16)
        mix = (jnp.dot(an, wout_ref[0:d_att, :], preferred_element_type=F32)
               + jnp.dot(ssm_ref[bs, ts, :].reshape(rows, d_ssm), wout_ref[d_att:, :],
                         preferred_element_type=F32))
        x1 = x_ref[bs, ts, :] + gt1_ref[bs, 0] * mix.reshape(hb, ht, d)
        x1_ref[bs, ts, :] = x1
        h2 = _rms_rows(x1, g2_ref[...]) * (1.0 + sc2_ref[bs, 0]) + sh2_ref[bs, 0]
        h2_ref[bs, ts, :] = h2.astype(BF16)


def _out_proj(att, ssm, x, mod4, mod_row, g_att, w_out, g2, *, nb, t):
    n_seq, seq, d = x.shape
    row_blk = mod_row // nb
    act = lambda width: pl.BlockSpec((nb, t, width), lambda i, j: (i, j, 0))
    mod = lambda m: pl.BlockSpec((nb, 1, 1, d), lambda i, j, m=m: (row_blk + i, m, 0, 0))
    return pl.pallas_call(
        _out_proj_kernel,
        out_shape=(jax.ShapeDtypeStruct(x.shape, F32), jax.ShapeDtypeStruct(x.shape, BF16)),
        grid=(n_seq // nb, seq // t),
        in_specs=[act(att.shape[-1]), act(ssm.shape[-1]), act(d), mod(2), mod(3), mod(4),
                  _resident(g_att.shape), _resident(w_out.shape), _resident(g2.shape)],
        out_specs=(act(d), act(d)),
        compiler_params=_params("parallel", "parallel"),
        name="out_proj",
    )(att, ssm, x, mod4, mod4, mod4, g_att, w_out, g2)


def _mlp_kernel(h2_ref, x1_ref, gt2_ref, w1_ref, w2_ref, o_ref, acc_ref):
    nb, t, d = x1_ref.shape
    j = pl.program_id(2)

    @pl.when(j == 0)
    def _():
        acc_ref[...] = jnp.zeros(acc_ref.shape, F32)

    a = jnp.dot(h2_ref[...].reshape(nb * t, d), w1_ref[...], preferred_element_type=F32)
    r = jnp.maximum(a, 0.0)
    acc_ref[...] += jnp.dot((r * r).astype(BF16), w2_ref[...], preferred_element_type=F32)

    @pl.when(j == pl.num_programs(2) - 1)
    def _():
        o_ref[...] = x1_ref[...] + gt2_ref[:, 0] * acc_ref[...].reshape(nb, t, d)


def _mlp(h2, x1, mod4, mod_row, w1, w2, *, nb, t, tf):
    n_seq, seq, d = x1.shape
    d_ff = w1.shape[1]
    row_blk = mod_row // nb
    act = pl.BlockSpec((nb, t, d), lambda i, j, f: (i, j, 0))
    return pl.pallas_call(
        _mlp_kernel,
        out_shape=jax.ShapeDtypeStruct(x1.shape, F32),
        grid=(n_seq // nb, seq // t, d_ff // tf),
        in_specs=[act, act,
                  pl.BlockSpec((nb, 1, 1, d), lambda i, j, f: (row_blk + i, 5, 0, 0)),
                  pl.BlockSpec((d, tf), lambda i, j, f: (0, f)),
                  pl.BlockSpec((tf, d), lambda i, j, f: (f, 0))],
        out_specs=act,
        scratch_shapes=[pltpu.VMEM((nb * t, d), F32)],
        compiler_params=_params("parallel", "parallel", "arbitrary"),
        name="mlp",
    )(h2, x1, mod4, w1, w2)


def kernel(x_prompt, x_sample, c_prompt, c_sample, cache_k, cache_v, cache_logf, state_ssm_re, state_ssm_im,
           w_ada, b_ada, g_norm1, w_in, g_q, g_k, b_f, log_dt, a_re, a_im, b_re, b_im, c_re, c_im, d_skip,
           w_glu, b_glu, g_att_out, g_ssm_out, w_out, g_norm2, w_ff1, w_ff2):
    depth = w_ada.shape[0]
    assert depth == 1, "single-layer step"
    bsz, seq, d = x_prompt.shape
    dec_b, dec_s, _ = x_sample.shape
    past = cache_k.shape[2]
    n_heads = cache_k.shape[3]
    d_att = n_heads * HEAD_DIM
    n_groups, n_state = a_re.shape[1], a_re.shape[2]
    mod_rows = 16
    assert dec_b + bsz <= mod_rows
    l = 0

    c_rows = jnp.concatenate([c_sample, c_prompt, jnp.zeros((mod_rows - dec_b - bsz, d), F32)], axis=0)
    mod4 = _modulation(c_rows, w_ada[l], b_ada[l][None]).reshape(mod_rows, N_MOD, 1, d)
    row_sample, row_prompt = 0, dec_b

    w_qkv = w_in[l, :, 0:3 * d_att].astype(BF16)
    w_f = jnp.pad(w_in[l, :, 3 * d_att:3 * d_att + n_heads], ((0, 0), (0, LANES - n_heads))).astype(BF16)
    w_u = w_in[l, :, 3 * d_att + n_heads:].astype(BF16)
    b_f_pad = jnp.pad(b_f[l], (0, LANES - n_heads))[None]
    g1 = g_norm1[l][None]
    gq, gk = g_q[l][None], g_k[l][None]

    wb32, wc32, pw_re, pw_im = _s5_params(log_dt[l], a_re[l], a_im[l], b_re[l], b_im[l], c_re[l], c_im[l])
    wb, wc = wb32.astype(BF16), wc32.astype(BF16)
    pw_re = jnp.repeat(pw_re.reshape(N_POW, n_groups * n_state), SUBLANES, axis=0)
    pw_im = jnp.repeat(pw_im.reshape(N_POW, n_groups * n_state), SUBLANES, axis=0)
    d_row = d_skip[l][None]
    w_glu_b = w_glu[l].astype(BF16)
    b_glu_row = b_glu[l][None]
    g_ssm_row = g_ssm_out[l][None]
    g_att_row = g_att_out[l][None]
    w_out_b = w_out[l].astype(BF16)
    g2 = g_norm2[l][None]
    w1 = w_ff1[l].astype(BF16)
    w2 = w_ff2[l].astype(BF16)

    tm = 512
    tc_prompt = SUBLANES * MAX_SEG
    tc_sample = dec_s

    q_p, k32_p, kbf_p, v32_p, vbf_p, logf_p, u_p = _in_proj(
        x_prompt, mod4, row_prompt, g1, w_qkv, w_u, w_f, gq, gk, b_f_pad,
        nb=1, t=tm, n_heads=n_heads, s5_chunk=tc_prompt)
    fcum_p, _ = _forget_cumsum(logf_p)
    att_p = _prompt_attention(q_p, kbf_p, vbf_p, fcum_p, n_heads=n_heads, tq=1024, tk=512, heads_per_step=1)
    zeros_state = jnp.zeros((bsz, 1, n_groups * n_state), F32)
    ssm_p, hre_p, him_p = _s5_mixer(u_p, zeros_state, zeros_state, wb, wc, pw_re, pw_im, d_row, w_glu_b,
                                    b_glu_row, g_ssm_row, tc=tc_prompt, chunks_per_step=tm // tc_prompt)
    x1_p, h2_p = _out_proj(att_p, ssm_p, x_prompt, mod4, row_prompt, g_att_row, w_out_b, g2, nb=1, t=tm)
    y_p = _mlp(h2_p, x1_p, mod4, row_prompt, w1, w2, nb=1, t=tm, tf=1024)

    q_s, k32_s, kbf_s, v32_s, vbf_s, logf_s, u_s = _in_proj(
        x_sample, mod4, row_sample, g1, w_qkv, w_u, w_f, gq, gk, b_f_pad,
        nb=dec_b, t=dec_s, n_heads=n_heads, s5_chunk=tc_sample)
    logf_all = jnp.concatenate([cache_logf[l], logf_s], axis=1)
    fcum_s, fhead_s = _forget_cumsum(logf_all)
    att_s = _sample_attention(q_s, kbf_s, vbf_s, cache_k[l].reshape(dec_b, past * n_heads, HEAD_DIM),
                              cache_v[l].reshape(dec_b, past * n_heads, HEAD_DIM), fcum_s, fhead_s,
                              n_heads=n_heads)
    ssm_s, hre_s, him_s = _s5_mixer(u_s, state_ssm_re[l].reshape(dec_b, 1, -1),
                                    state_ssm_im[l].reshape(dec_b, 1, -1), wb, wc, pw_re, pw_im, d_row,
                                    w_glu_b, b_glu_row, g_ssm_row, tc=tc_sample, chunks_per_step=1)
    x1_s, h2_s = _out_proj(att_s, ssm_s, x_sample, mod4, row_sample, g_att_row, w_out_b, g2,
                           nb=dec_b, t=dec_s)
    y_s = _mlp(h2_s, x1_s, mod4, row_sample, w1, w2, nb=dec_b, t=dec_s, tf=1024)

    states = lambda a, n: a.reshape(1, n, n_groups, n_state)
    heads = lambda a, n, s: a.reshape(1, n, s, n_heads, HEAD_DIM)
    return (y_p, y_s,
            heads(k32_p, bsz, seq), heads(v32_p, bsz, seq), logf_p[None],
            states(hre_p, bsz), states(him_p, bsz),
            heads(k32_s, dec_b, dec_s), heads(v32_s, dec_b, dec_s), logf_s[None],
            states(hre_s, dec_b), states(him_s, dec_b))
```

```python
import functools
import math

import jax
import jax.numpy as jnp
from jax import lax
from jax.experimental import pallas as pl
from jax.experimental.pallas import tpu as pltpu

F32 = jnp.float32
BF16 = jnp.bfloat16
HIGHEST = lax.Precision.HIGHEST

HEAD_DIM = 128
SSM_GROUP = 16
N_MOD = 6
EPS = 1e-6
LOG2E = 1.4426950408889634

LANES = 128
SUBLANES = 8
GROUPS_PER_BLOCK = 16
MAX_SEG = 32
N_POW = MAX_SEG + 2
VMEM_LIMIT = 56 * 1024 * 1024


def _params(*sem):
    return pltpu.CompilerParams(dimension_semantics=sem, vmem_limit_bytes=VMEM_LIMIT)


def _resident(shape):
    nd = len(shape)
    return pl.BlockSpec(shape, lambda *_: (0,) * nd, pipeline_mode=pl.Buffered(1))


def _sigmoid(x):
    return 1.0 / (1.0 + jnp.exp(-x))


def _rms_rows(x, g):
    return x * lax.rsqrt(jnp.mean(x * x, axis=-1, keepdims=True) + EPS) * g


def _dot_nt(a, b):
    return lax.dot_general(a, b, (((1,), (1,)), ((), ())), preferred_element_type=F32)


def _cmul_add(x_re, x_im, a_re, a_im, h_re, h_im):
    return x_re + (a_re * h_re - a_im * h_im), x_im + (a_re * h_im + a_im * h_re)


def _stream_permutation(n, inverse):
    i = lax.broadcasted_iota(jnp.int32, (n, n), 0)
    j = lax.broadcasted_iota(jnp.int32, (n, n), 1)
    if inverse:
        i, j = j, i
    src = (i & (SUBLANES - 1)) * (n // SUBLANES) + (i >> 3)
    return jnp.where(j == src, 1.0, 0.0).astype(BF16)


def _split_bf16(x):
    hi = x.astype(BF16)
    return hi, (x - hi.astype(F32)).astype(BF16)


def _mod_kernel(c_ref, w_ref, b_ref, o_ref):
    c = c_ref[...]
    rows = c.shape[0]
    s_hi, s_lo = _split_bf16(c * _sigmoid(c))
    w_hi, w_lo = _split_bf16(w_ref[...])
    first = jnp.dot(jnp.concatenate([s_hi, s_lo], axis=0), w_hi, preferred_element_type=F32)
    second = jnp.dot(s_hi, w_lo, preferred_element_type=F32)
    o_ref[...] = first[0:rows] + (first[rows:2 * rows] + second) + b_ref[...]


def _modulation(c_rows, w_ada, b_ada):
    rows, d = c_rows.shape
    n = w_ada.shape[1]
    tn = 1024
    return pl.pallas_call(
        _mod_kernel,
        out_shape=jax.ShapeDtypeStruct((rows, n), F32),
        grid=(n // tn,),
        in_specs=[pl.BlockSpec((rows, d), lambda j: (0, 0)),
                  pl.BlockSpec((d, tn), lambda j: (0, j)),
                  pl.BlockSpec((1, tn), lambda j: (0, j))],
        out_specs=pl.BlockSpec((rows, tn), lambda j: (0, j)),
        compiler_params=_params("arbitrary"),
        name="modulation",
    )(c_rows, w_ada, b_ada)


def _zoh(log_dt, a_re, a_im):
    dt = jnp.exp(log_dt)
    lam_re = jnp.minimum(a_re, -1e-4)
    lam_im = a_im
    mag = jnp.exp(lam_re * dt)
    ang = lam_im * dt
    ab_re = mag * jnp.cos(ang)
    ab_im = mag * jnp.sin(ang)
    den = lam_re * lam_re + lam_im * lam_im
    z_re = ((ab_re - 1.0) * lam_re + ab_im * lam_im) / den
    z_im = (ab_im * lam_re - (ab_re - 1.0) * lam_im) / den
    return ab_re, ab_im, z_re, z_im


def _s5_param_kernel(logdt_ref, are_ref, aim_ref, logdt2_ref, are2_ref, aim2_ref, bre_ref, bim_ref, cre_ref,
                     cim_ref, wb_ref, wc_ref, pwre_ref, pwim_ref):
    n_groups, n_ch, n_st = bre_ref.shape
    gpb = GROUPS_PER_BLOCK
    ns = gpb * n_st
    _, _, z_re, z_im = _zoh(logdt_ref[...], are_ref[...], aim_ref[...])
    ab_re, ab_im, _, _ = _zoh(logdt2_ref[...], are2_ref[...], aim2_ref[...])
    b_re = bre_ref[...]
    b_im = bim_ref[...]
    zb_re = z_re * b_re - z_im * b_im
    zb_im = z_re * b_im + z_im * b_re
    wb_ref[...] = jnp.zeros(wb_ref.shape, F32)
    wc_ref[...] = jnp.zeros(wc_ref.shape, F32)
    for g in range(n_groups):
        blk, gi = divmod(g, gpb)
        chs = slice(gi * n_ch, (gi + 1) * n_ch)
        sts = slice(gi * n_st, (gi + 1) * n_st)
        sts_im = slice(ns + gi * n_st, ns + (gi + 1) * n_st)
        wb_ref[blk, chs, sts] = zb_re[g]
        wb_ref[blk, chs, sts_im] = zb_im[g]
        wc_ref[blk, sts, chs] = cre_ref[g]
        wc_ref[blk, sts_im, chs] = -cim_ref[g]
    p_re, p_im = ab_re, ab_im
    for k in range(MAX_SEG):
        pwre_ref[k] = p_re
        pwim_ref[k] = p_im
        if k + 1 < MAX_SEG:
            p_re, p_im = p_re * ab_re - p_im * ab_im, p_re * ab_im + p_im * ab_re
    for k in range(MAX_SEG, N_POW):
        p_re, p_im = p_re * p_re - p_im * p_im, 2.0 * (p_re * p_im)
        pwre_ref[k] = p_re
        pwim_ref[k] = p_im


def _s5_params(log_dt, a_re, a_im, b_re, b_im, c_re, c_im):
    g, n, c = b_re.shape
    n_blk = g // GROUPS_PER_BLOCK
    ch, ns = GROUPS_PER_BLOCK * c, GROUPS_PER_BLOCK * n
    f = lambda shape: jax.ShapeDtypeStruct(shape, F32)
    return pl.pallas_call(
        _s5_param_kernel,
        out_shape=(f((n_blk, ch, 2 * ns)), f((n_blk, 2 * ns, ch)), f((N_POW, g, n)), f((N_POW, g, n))),
        compiler_params=_params(),
        name="s5_params",
    )(log_dt.reshape(g, 1, 1), a_re.reshape(g, 1, n), a_im.reshape(g, 1, n), log_dt.reshape(g, 1), a_re, a_im,
      jnp.swapaxes(b_re, 1, 2), jnp.swapaxes(b_im, 1, 2), jnp.swapaxes(c_re, 1, 2), jnp.swapaxes(c_im, 1, 2))


def _pow_row(seg_rows):
    if seg_rows <= MAX_SEG:
        return seg_rows - 1
    return MAX_SEG - 1 + int(math.log2(seg_rows // MAX_SEG))


def _in_proj_kernel(x_ref, sh_ref, sc_ref, g1_ref, wqkv_ref, wu_ref, wf_ref, gq_ref, gk_ref, bf_ref,
                    q_ref, k32_ref, kbf_ref, v32_ref, vbf_ref, logf_ref, u_ref,
                    *, n_heads, q_scale, s5_chunk):
    nb, t, d = x_ref.shape
    rows = nb * t
    d_att = n_heads * HEAD_DIM
    x = x_ref[...]
    y = _rms_rows(x, g1_ref[...])
    h = (y * (1.0 + sc_ref[:, 0]) + sh_ref[:, 0]).reshape(rows, d)
    hb = h.astype(BF16)

    q = jnp.dot(hb, wqkv_ref[:, 0:d_att], preferred_element_type=F32)
    for hh in range(n_heads):
        sl = slice(hh * HEAD_DIM, (hh + 1) * HEAD_DIM)
        qn = _rms_rows(q[:, sl], gq_ref[...]) * q_scale
        q_ref[:, :, sl] = qn.reshape(nb, t, HEAD_DIM).astype(BF16)

    k = jnp.dot(hb, wqkv_ref[:, d_att:2 * d_att], preferred_element_type=F32)
    for hh in range(n_heads):
        sl = slice(hh * HEAD_DIM, (hh + 1) * HEAD_DIM)
        kn = _rms_rows(k[:, sl], gk_ref[...])
        kbf_ref[:, :, sl] = kn.reshape(nb, t, HEAD_DIM).astype(BF16)
        for bi in range(nb):
            k32_ref[bi, pl.ds(hh, t, stride=n_heads), :] = kn[bi * t:(bi + 1) * t]

    v = jnp.dot(hb, wqkv_ref[:, 2 * d_att:3 * d_att], preferred_element_type=F32)
    for hh in range(n_heads):
        sl = slice(hh * HEAD_DIM, (hh + 1) * HEAD_DIM)
        for bi in range(nb):
            v32_ref[bi, pl.ds(hh, t, stride=n_heads), :] = v[bi * t:(bi + 1) * t, sl]
    vbf_ref[...] = v.reshape(nb, t, d_att).astype(BF16)

    f = jnp.dot(hb, wf_ref[...], preferred_element_type=F32) + bf_ref[...]
    logf = jnp.minimum(f, 0.0) - jnp.log1p(jnp.exp(-jnp.abs(f)))
    logf_ref[...] = logf[:, 0:n_heads].reshape(nb, t, n_heads)

    u = jnp.dot(hb, wu_ref[...], preferred_element_type=F32).astype(BF16)
    perm = _stream_permutation(s5_chunk, inverse=False)
    for c in range(rows // s5_chunk):
        uc = jnp.dot(perm, u[c * s5_chunk:(c + 1) * s5_chunk], preferred_element_type=F32).astype(BF16)
        if nb == 1:
            u_ref[0, c * s5_chunk:(c + 1) * s5_chunk, :] = uc
        else:
            per = t // s5_chunk
            u_ref[c // per, (c % per) * s5_chunk:(c % per + 1) * s5_chunk, :] = uc


def _in_proj(x, mod4, mod_row, g1, w_qkv, w_u, w_f, g_q, g_k, b_f_pad, *, nb, t, n_heads, s5_chunk):
    n_seq, seq, d = x.shape
    d_att = n_heads * HEAD_DIM
    d_ssm = w_u.shape[1]
    assert t % s5_chunk == 0
    grid = (n_seq // nb, seq // t)
    row_blk = mod_row // nb
    act = lambda width: pl.BlockSpec((nb, t, width), lambda i, j: (i, j, 0))
    act4 = pl.BlockSpec((nb, t * n_heads, HEAD_DIM), lambda i, j: (i, j, 0))
    mod = lambda m: pl.BlockSpec((nb, 1, 1, d), lambda i, j, m=m: (row_blk + i, m, 0, 0))
    sds = lambda width, dt: jax.ShapeDtypeStruct((n_seq, seq, width), dt)
    sds4 = jax.ShapeDtypeStruct((n_seq, seq * n_heads, HEAD_DIM), F32)
    kern = functools.partial(_in_proj_kernel, n_heads=n_heads, q_scale=HEAD_DIM ** -0.5 * LOG2E,
                             s5_chunk=s5_chunk)
    return pl.pallas_call(
        kern,
        out_shape=(sds(d_att, BF16), sds4, sds(d_att, BF16), sds4, sds(d_att, BF16),
                   sds(n_heads, F32), sds(d_ssm, BF16)),
        grid=grid,
        in_specs=[act(d), mod(0), mod(1), _resident(g1.shape), _resident(w_qkv.shape), _resident(w_u.shape),
                  _resident(w_f.shape), _resident(g_q.shape), _resident(g_k.shape), _resident(b_f_pad.shape)],
        out_specs=(act(d_att), act4, act(d_att), act4, act(d_att), act(n_heads), act(d_ssm)),
        compiler_params=_params("parallel", "parallel"),
        name="in_proj",
    )(x, mod4, mod4, g1, w_qkv, w_u, w_f, g_q, g_k, b_f_pad)


def _cumsum_kernel(x_ref, o_ref, *, n_chunks):
    x = x_ref[...]
    n = x.shape[0]
    li = lax.broadcasted_iota(jnp.int32, (LANES, LANES), 0)
    lj = lax.broadcasted_iota(jnp.int32, (LANES, LANES), 1)
    tri = jnp.where(li <= lj, 1.0, 0.0)
    within = jnp.dot(x, tri, precision=HIGHEST, preferred_element_type=F32)
    tot = jnp.dot(x, jnp.ones((LANES, LANES), F32), precision=HIGHEST, preferred_element_type=F32)
    r = lax.broadcasted_iota(jnp.int32, (n, n), 0)
    c = lax.broadcasted_iota(jnp.int32, (n, n), 1)
    seq_of = lambda i: jnp.floor((i.astype(F32) + 0.5) * (1.0 / n_chunks))
    earlier = jnp.where(seq_of(r) == seq_of(c), jnp.where(c < r, 1.0, 0.0), 0.0)
    before = jnp.dot(earlier, tot, precision=HIGHEST, preferred_element_type=F32)
    o_ref[...] = (within + before) * LOG2E


def _forget_cumsum(logf):
    b, length, h = logf.shape
    n_chunks = -(-length // LANES)
    rows = jnp.swapaxes(logf, 1, 2).reshape(b * h, length)
    rows = jnp.pad(rows, ((0, 0), (0, n_chunks * LANES - length)))
    out = pl.pallas_call(
        functools.partial(_cumsum_kernel, n_chunks=n_chunks),
        out_shape=jax.ShapeDtypeStruct((b * h * n_chunks, LANES), F32),
        compiler_params=_params(),
        name="forget_cumsum",
    )(rows.reshape(b * h * n_chunks, LANES))
    by_head = out.reshape(b, h, n_chunks * LANES)
    return jnp.swapaxes(by_head[:, :, :length], 1, 2), by_head


def _head_column(f_rows, head):
    lane_h = lax.broadcasted_iota(jnp.int32, f_rows.shape, 1)
    return jnp.sum(jnp.where(lane_h == head, f_rows, 0.0), axis=-1, keepdims=True)


def _bias_columns(f_rows, head, key_side):
    col = _head_column(f_rows, head)
    hi = col.astype(BF16).astype(F32)
    rest = col - hi
    mid = rest.astype(BF16).astype(F32)
    lo = rest - mid
    lane = lax.broadcasted_iota(jnp.int32, (f_rows.shape[0], LANES), 1)
    if key_side:
        split = jnp.where(lane == 3, -hi, jnp.where(lane == 4, -mid, jnp.where(lane == 5, -lo, 0.0)))
        vals = jnp.where(lane < 3, 1.0, split)
    else:
        split = jnp.where(lane == 0, hi, jnp.where(lane == 1, mid, jnp.where(lane == 2, lo, 0.0)))
        vals = jnp.where((lane >= 3) & (lane < 6), 1.0, split)
    return vals.astype(BF16)


def _attn_kernel(q_ref, qn_ref, k_ref, v_ref, f_ref, o_ref, kaug_ref, vaug_ref, qa_ref, qnext_ref, s0_ref, s1_ref,
                 m_ref, acc_ref, *, tq, tk, build_rows, heads_per_step):
    qi = pl.program_id(2)
    t = k_ref.shape[1]
    dh = HEAD_DIM
    hps = heads_per_step
    head0 = pl.program_id(1) * hps

    def first_scores(src_ref, row0, dst_ref):
        f_rows = f_ref[0, pl.ds(row0, tq), :]
        for g in range(hps):
            qa = jnp.concatenate([src_ref[0, :, g * dh:(g + 1) * dh], _bias_columns(f_rows, head0 + g, False)],
                                 axis=-1)
            dst_ref[g] = qa
            s0_ref[g] = _dot_nt(qa, kaug_ref[g, 0:tk, :])

    def scores(c, s_ref, r0=0):
        off = pl.multiple_of(c * tk, tk)
        for g in range(hps):
            s_ref[g, r0:, :] = _dot_nt(qa_ref[g, r0:, :], kaug_ref[g, pl.ds(off, tk), :])

    @pl.when(qi == 0)
    def _first_block():
        for g in range(hps):
            kaug_ref[g, :, 0:dh] = k_ref[0, :, g * dh:(g + 1) * dh]
            vaug_ref[g, :, 0:dh] = v_ref[0, :, g * dh:(g + 1) * dh]
            vaug_ref[g, :, dh:2 * dh] = jnp.ones((t, dh), BF16)

        def piece(i, carry):
            r0 = pl.multiple_of(i * build_rows, build_rows)
            f_rows = f_ref[0, pl.ds(r0, build_rows), :]
            for g in range(hps):
                kaug_ref[g, pl.ds(r0, build_rows), dh:2 * dh] = _bias_columns(f_rows, head0 + g, True)
            return carry

        lax.fori_loop(0, t // build_rows, piece, 0)
        first_scores(q_ref, 0, qa_ref)

    @pl.when(qi > 0)
    def _take_prepared_queries():
        qa_ref[...] = qnext_ref[...]

    q0 = pl.multiple_of(qi * tq, tq)
    m_ref[...] = jnp.full(m_ref.shape, -jnp.inf, F32)
    acc_ref[...] = jnp.zeros(acc_ref.shape, F32)

    def absorb(c, s_ref, masked, r0=0):
        off = pl.multiple_of(c * tk, tk)
        for g in range(hps):
            s = s_ref[g, r0:, :]
            if masked:
                ahead = (lax.broadcasted_iota(jnp.int32, s.shape, 1)
                         - lax.broadcasted_iota(jnp.int32, s.shape, 0))
                s = jnp.where(ahead <= q0 + r0 - off, s, -jnp.inf)
            m_old = m_ref[g, r0:, :]
            m_new = jnp.maximum(m_old, jnp.max(s, axis=-1, keepdims=True))
            alpha = jnp.exp2(m_old - m_new)
            p = jnp.exp2(s - jnp.tile(m_new, (1, tk // LANES)))
            pv = jnp.dot(p.astype(BF16), vaug_ref[g, pl.ds(off, tk), :], preferred_element_type=F32)
            acc_ref[g, r0:, :] = jnp.tile(alpha, (1, 2)) * acc_ref[g, r0:, :] + pv
            m_ref[g, r0:, :] = m_new

    n_full = 2 * qi

    def pair(c):
        scores(c + 1, s1_ref)
        absorb(c, s0_ref, False)
        scores(c + 2, s0_ref)
        absorb(c + 1, s1_ref, False)

    def two_pairs(p, carry):
        pair(4 * p)
        pair(4 * p + 2)
        return carry

    lax.fori_loop(0, qi // 2, two_pairs, 0)

    @pl.when(qi % 2 == 1)
    def _odd_pair():
        pair(n_full - 2)

    scores(n_full + 1, s1_ref, r0=tk)
    absorb(n_full, s0_ref, True)
    first_scores(qn_ref, pl.multiple_of(jnp.minimum(qi + 1, pl.num_programs(2) - 1) * tq, tq), qnext_ref)
    absorb(n_full + 1, s1_ref, True, r0=tk)

    for g in range(hps):
        acc = acc_ref[g]
        o_ref[0, :, g * dh:(g + 1) * dh] = (acc[:, 0:dh] * (1.0 / acc[:, dh:2 * dh])).astype(o_ref.dtype)


def _prompt_attention(q, k, v, f_cum2, *, n_heads, tq, tk, heads_per_step):
    b, t, _ = q.shape
    assert t % tq == 0 and tq == 2 * tk and n_heads % heads_per_step == 0
    hps = heads_per_step
    width = hps * HEAD_DIM
    nq = t // tq
    q_blk = pl.BlockSpec((1, tq, width), lambda bi, h, i: (bi, i, h))
    qn_blk = pl.BlockSpec((1, tq, width), lambda bi, h, i: (bi, jnp.minimum(i + 1, nq - 1), h))
    kv_blk = pl.BlockSpec((1, t, width), lambda bi, h, i: (bi, 0, h))
    f_blk = pl.BlockSpec((1, t, n_heads), lambda bi, h, i: (bi, 0, 0), pipeline_mode=pl.Buffered(1))
    return pl.pallas_call(
        functools.partial(_attn_kernel, tq=tq, tk=tk, build_rows=1024, heads_per_step=hps),
        out_shape=jax.ShapeDtypeStruct(q.shape, BF16),
        grid=(b, n_heads // hps, nq),
        in_specs=[q_blk, qn_blk, kv_blk, kv_blk, f_blk],
        out_specs=q_blk,
        scratch_shapes=[pltpu.VMEM((hps, t, 2 * HEAD_DIM), BF16), pltpu.VMEM((hps, t, 2 * HEAD_DIM), BF16),
                        pltpu.VMEM((hps, tq, 2 * HEAD_DIM), BF16), pltpu.VMEM((hps, tq, 2 * HEAD_DIM), BF16),
                        pltpu.VMEM((hps, tq, tk), F32), pltpu.VMEM((hps, tq, tk), F32),
                        pltpu.VMEM((hps, tq, LANES), F32), pltpu.VMEM((hps, tq, 2 * HEAD_DIM), F32)],
        compiler_params=_params("parallel", "arbitrary", "arbitrary"),
        name="prompt_attention",
    )(q, q, k, v, f_cum2)


def _sample_attn_kernel(q_ref, kc_ref, vc_ref, kn_ref, vn_ref, fq_ref, fk_ref, o_ref, *, past, n_heads):
    s_len = q_ref.shape[1]
    f_new = fq_ref[0]
    row = lax.broadcasted_iota(jnp.int32, (s_len, s_len), 0)
    col = lax.broadcasted_iota(jnp.int32, (s_len, s_len), 1)
    for hh in range(n_heads):
        sl = slice(hh * HEAD_DIM, (hh + 1) * HEAD_DIM)
        q = q_ref[0, :, sl]
        fq = _head_column(f_new, hh)
        fk = fk_ref[0, hh:hh + 1, :]
        head_rows = pl.ds(hh, past, stride=n_heads)
        s_c = _dot_nt(q, kc_ref[0, head_rows, :].astype(BF16)) + fq - fk[:, 0:past]
        s_n = jnp.where(col <= row, _dot_nt(q, kn_ref[0, :, sl]) + fq - fk[:, past:past + s_len], -jnp.inf)
        m = jnp.maximum(jnp.max(s_c, axis=-1, keepdims=True), jnp.max(s_n, axis=-1, keepdims=True))
        p_c = jnp.exp2(s_c - m)
        p_n = jnp.exp2(s_n - m)
        l = jnp.sum(p_c, axis=-1, keepdims=True) + jnp.sum(p_n, axis=-1, keepdims=True)
        o = (jnp.dot(p_c.astype(BF16), vc_ref[0, head_rows, :].astype(BF16), preferred_element_type=F32)
             + jnp.dot(p_n.astype(BF16), vn_ref[0, :, sl], preferred_element_type=F32))
        o_ref[0, :, sl] = (o * (1.0 / l)).astype(o_ref.dtype)


def _sample_attention(q, k_new, v_new, cache_k, cache_v, f_cum2, f_by_head, *, n_heads):
    b, s_len, d_att = q.shape
    past = cache_k.shape[1] // n_heads
    assert past % s_len == 0
    new = pl.BlockSpec((1, s_len, d_att), lambda bi: (bi, 0, 0))
    old = pl.BlockSpec((1, past * n_heads, HEAD_DIM), lambda bi: (bi, 0, 0))
    return pl.pallas_call(
        functools.partial(_sample_attn_kernel, past=past, n_heads=n_heads),
        out_shape=jax.ShapeDtypeStruct(q.shape, BF16),
        grid=(b,),
        in_specs=[new, old, old, new, new,
                  pl.BlockSpec((1, s_len, n_heads), lambda bi: (bi, past // s_len, 0)),
                  pl.BlockSpec((1, n_heads, f_by_head.shape[-1]), lambda bi: (bi, 0, 0))],
        out_specs=new,
        compiler_params=_params("parallel"),
        name="sample_attention",
    )(q, cache_k, cache_v, k_new, v_new, f_cum2, f_by_head)


def _s5_kernel(u_ref, h0re_ref, h0im_ref, wb_ref, wc_ref, pwre_ref, pwim_ref, pbre_ref, pbim_ref, d_ref,
               wglu_ref, bglu_ref, gout_ref, o_ref, hre_ref, him_ref, bu_ref, xre_ref, xim_ref, hbre_ref, hbim_ref,
               y_ref, *, tc):
    n_rows = u_ref.shape[1]
    n_chunks = n_rows // tc
    seg = tc // SUBLANES
    n_blk, ch = wb_ref.shape[0], wb_ref.shape[1]
    ns = wb_ref.shape[2] // 2

    @pl.when(pl.program_id(1) == 0)
    def _():
        hre_ref[...] = h0re_ref[...]
        him_ref[...] = h0im_ref[...]

    def b_proj(blk):
        bu_ref[blk % 2] = jnp.dot(u_ref[0, :, blk * ch:(blk + 1) * ch], wb_ref[blk], preferred_element_type=F32)

    sub = lax.broadcasted_iota(jnp.int32, (SUBLANES, ns), 0)
    tab = lambda r: slice(r * SUBLANES, (r + 1) * SUBLANES)
    b_proj(0)
    for blk in range(n_blk):
        if blk + 1 < n_blk:
            b_proj(blk + 1)
        buf = blk % 2
        cols = slice(blk * ns, (blk + 1) * ns)
        chs = slice(blk * ch, (blk + 1) * ch)
        for c in range(n_chunks):
            _s5_scan_chunk(c, tc, seg, ns, buf, cols, sub, tab, bu_ref, xre_ref, xim_ref, hbre_ref, hbim_ref,
                           pwre_ref, pwim_ref, pbre_ref, pbim_ref, hre_ref, him_ref)
        y = (jnp.dot(hbre_ref[...], wc_ref[blk, 0:ns, :], preferred_element_type=F32)
             + jnp.dot(hbim_ref[...], wc_ref[blk, ns:2 * ns, :], preferred_element_type=F32))
        y_ref[:, chs] = y + d_ref[:, chs] * u_ref[0, :, chs].astype(F32)

    y = y_ref[...]
    g = y * (0.5 * (1.0 + jnp.tanh(math.sqrt(2.0 / math.pi) * (y + 0.044715 * (y * y * y)))))
    gate = _sigmoid(jnp.dot(g.astype(BF16), wglu_ref[...], preferred_element_type=F32) + bglu_ref[...])
    out = _rms_rows(g * gate, gout_ref[...]).astype(BF16)
    unperm = _stream_permutation(tc, inverse=True)
    for c in range(n_chunks):
        o_ref[0, c * tc:(c + 1) * tc, :] = jnp.dot(unperm, out[c * tc:(c + 1) * tc],
                                                    preferred_element_type=F32).astype(o_ref.dtype)


def _s5_scan_chunk(c, tc, seg, ns, buf, cols, sub, tab, bu_ref, xre_ref, xim_ref, hbre_ref, hbim_ref,
                   pwre_ref, pwim_ref, pbre_ref, pbim_ref, hre_ref, him_ref):
    rows = lambda r: slice(c * tc + r * SUBLANES, c * tc + (r + 1) * SUBLANES)

    a_re, a_im = pwre_ref[tab(0), cols], pwim_ref[tab(0), cols]
    h_re = bu_ref[buf, rows(0), 0:ns]
    h_im = bu_ref[buf, rows(0), ns:2 * ns]
    xre_ref[rows(0), :] = h_re
    xim_ref[rows(0), :] = h_im
    for r in range(1, seg):
        h_re, h_im = _cmul_add(bu_ref[buf, rows(r), 0:ns], bu_ref[buf, rows(r), ns:2 * ns], a_re, a_im, h_re, h_im)
        xre_ref[rows(r), :] = h_re
        xim_ref[rows(r), :] = h_im

    s_re = jnp.where(sub == 0, hre_ref[0, :, cols], pltpu.roll(h_re, 1, 0))
    s_im = jnp.where(sub == 0, him_ref[0, :, cols], pltpu.roll(h_im, 1, 0))
    for shift in (1, 2, 4):
        row = _pow_row(seg * shift)
        m_re = jnp.where(sub >= shift, pwre_ref[tab(row), cols], 0.0)
        m_im = jnp.where(sub >= shift, pwim_ref[tab(row), cols], 0.0)
        s_re, s_im = _cmul_add(s_re, s_im, m_re, m_im, pltpu.roll(s_re, shift, 0), pltpu.roll(s_im, shift, 0))

    s2_re = jnp.concatenate([s_re, s_re], axis=0).astype(BF16)
    s2_im = jnp.concatenate([s_im, s_im], axis=0).astype(BF16)
    for j in range(seg // 2):
        two = slice(c * tc + 2 * j * SUBLANES, c * tc + (2 * j + 2) * SUBLANES)
        tab2 = slice(2 * j * SUBLANES, (2 * j + 2) * SUBLANES)
        t_re, t_im = _cmul_add(xre_ref[two, :].astype(BF16), xim_ref[two, :].astype(BF16),
                               pbre_ref[tab2, cols], pbim_ref[tab2, cols], s2_re, s2_im)
        hbre_ref[two, :] = t_re
        hbim_ref[two, :] = t_im
    last = seg - 1
    c_re, c_im = _cmul_add(xre_ref[rows(last), :], xim_ref[rows(last), :],
                           pwre_ref[tab(last), cols], pwim_ref[tab(last), cols], s_re, s_im)
    hre_ref[0, :, cols] = c_re[SUBLANES - 1:SUBLANES, :]
    him_ref[0, :, cols] = c_im[SUBLANES - 1:SUBLANES, :]


def _s5_mixer(u, h0_re, h0_im, wb, wc, pw_re, pw_im, d_skip, w_glu, b_glu, g_out, *, tc, chunks_per_step):
    b, t, d_ssm = u.shape
    n_state = h0_re.shape[-1]
    rows = tc * chunks_per_step
    assert tc // SUBLANES <= MAX_SEG and t % rows == 0
    state = pl.BlockSpec((1, 1, n_state), lambda bi, j: (bi, 0, 0))
    act = pl.BlockSpec((1, rows, d_ssm), lambda bi, j: (bi, j, 0))
    ns = wb.shape[2] // 2
    pb_re = pw_re[0:MAX_SEG * SUBLANES].astype(BF16)
    pb_im = pw_im[0:MAX_SEG * SUBLANES].astype(BF16)
    return pl.pallas_call(
        functools.partial(_s5_kernel, tc=tc),
        out_shape=(jax.ShapeDtypeStruct((b, t, d_ssm), BF16),
                   jax.ShapeDtypeStruct((b, 1, n_state), F32), jax.ShapeDtypeStruct((b, 1, n_state), F32)),
        grid=(b, t // rows),
        in_specs=[act, state, state, _resident(wb.shape), _resident(wc.shape), _resident(pw_re.shape),
                  _resident(pw_im.shape), _resident(pb_re.shape), _resident(pb_im.shape),
                  _resident(d_skip.shape), _resident(w_glu.shape), _resident(b_glu.shape),
                  _resident(g_out.shape)],
        out_specs=(act, state, state),
        scratch_shapes=[pltpu.VMEM((2, rows, 2 * ns), F32), pltpu.VMEM((rows, ns), F32),
                        pltpu.VMEM((rows, ns), F32), pltpu.VMEM((rows, ns), BF16), pltpu.VMEM((rows, ns), BF16),
                        pltpu.VMEM((rows, d_ssm), F32)],
        compiler_params=_params("parallel", "arbitrary"),
        name="s5_mixer",
    )(u, h0_re, h0_im, wb, wc, pw_re, pw_im, pb_re, pb_im, d_skip, w_glu, b_glu, g_out)


def _out_proj_kernel(att_ref, ssm_ref, x_ref, gt1_ref, sh2_ref, sc2_ref, gatt_ref, wout_ref, g2_ref,
                     x1_ref, h2_ref):
    nb, t, d = x_ref.shape
    d_att, d_ssm = att_ref.shape[-1], ssm_ref.shape[-1]
    halves = [(slice(0, nb), slice(h * (t // 2), (h + 1) * (t // 2))) for h in range(2)] if nb == 1 else \
             [(slice(h * (nb // 2), (h + 1) * (nb // 2)), slice(0, t)) for h in range(2)]
    for bs, ts in halves:
        hb, ht = bs.stop - bs.start, ts.stop - ts.start
        rows = hb * ht
        a = att_ref[bs, ts, :].astype(F32).reshape(rows, d_att)
        an = _rms_rows(a, gatt_ref[...]).astype(BF16)
        mix = (jnp.dot(an, wout_ref[0:d_att, :], preferred_element_type=F32)
               + jnp.dot(ssm_ref[bs, ts, :].reshape(rows, d_ssm), wout_ref[d_att:, :],
                         preferred_element_type=F32))
        x1 = x_ref[bs, ts, :] + gt1_ref[bs, 0] * mix.reshape(hb, ht, d)
        x1_ref[bs, ts, :] = x1
        h2 = _rms_rows(x1, g2_ref[...]) * (1.0 + sc2_ref[bs, 0]) + sh2_ref[bs, 0]
        h2_ref[bs, ts, :] = h2.astype(BF16)


def _out_proj(att, ssm, x, mod4, mod_row, g_att, w_out, g2, *, nb, t):
    n_seq, seq, d = x.shape
    row_blk = mod_row // nb
    act = lambda width: pl.BlockSpec((nb, t, width), lambda i, j: (i, j, 0))
    mod = lambda m: pl.BlockSpec((nb, 1, 1, d), lambda i, j, m=m: (row_blk + i, m, 0, 0))
    return pl.pallas_call(
        _out_proj_kernel,
        out_shape=(jax.ShapeDtypeStruct(x.shape, F32), jax.ShapeDtypeStruct(x.shape, BF16)),
        grid=(n_seq // nb, seq // t),
        in_specs=[act(att.shape[-1]), act(ssm.shape[-1]), act(d), mod(2), mod(3), mod(4),
                  _resident(g_att.shape), _resident(w_out.shape), _resident(g2.shape)],
        out_specs=(act(d), act(d)),
        compiler_params=_params("parallel", "parallel"),
        name="out_proj",
    )(att, ssm, x, mod4, mod4, mod4, g_att, w_out, g2)


def _mlp_kernel(h2_ref, x1_ref, gt2_ref, w1_ref, w2_ref, o_ref, acc_ref):
    nb, t, d = x1_ref.shape
    j = pl.program_id(2)

    @pl.when(j == 0)
    def _():
        acc_ref[...] = jnp.zeros(acc_ref.shape, F32)

    a = jnp.dot(h2_ref[...].reshape(nb * t, d), w1_ref[...], preferred_element_type=F32)
    r = jnp.maximum(a, 0.0)
    acc_ref[...] += jnp.dot((r * r).astype(BF16), w2_ref[...], preferred_element_type=F32)

    @pl.when(j == pl.num_programs(2) - 1)
    def _():
        o_ref[...] = x1_ref[...] + gt2_ref[:, 0] * acc_ref[...].reshape(nb, t, d)


def _mlp(h2, x1, mod4, mod_row, w1, w2, *, nb, t, tf):
    n_seq, seq, d = x1.shape
    d_ff = w1.shape[1]
    row_blk = mod_row // nb
    act = pl.BlockSpec((nb, t, d), lambda i, j, f: (i, j, 0))
    return pl.pallas_call(
        _mlp_kernel,
        out_shape=jax.ShapeDtypeStruct(x1.shape, F32),
        grid=(n_seq // nb, seq // t, d_ff // tf),
        in_specs=[act, act,
                  pl.BlockSpec((nb, 1, 1, d), lambda i, j, f: (row_blk + i, 5, 0, 0)),
                  pl.BlockSpec((d, tf), lambda i, j, f: (0, f)),
                  pl.BlockSpec((tf, d), lambda i, j, f: (f, 0))],
        out_specs=act,
        scratch_shapes=[pltpu.VMEM((nb * t, d), F32)],
        compiler_params=_params("parallel", "parallel", "arbitrary"),
        name="mlp",
    )(h2, x1, mod4, w1, w2)


def kernel(x_prompt, x_sample, c_prompt, c_sample, cache_k, cache_v, cache_logf, state_ssm_re, state_ssm_im,
           w_ada, b_ada, g_norm1, w_in, g_q, g_k, b_f, log_dt, a_re, a_im, b_re, b_im, c_re, c_im, d_skip,
           w_glu, b_glu, g_att_out, g_ssm_out, w_out, g_norm2, w_ff1, w_ff2):
    depth = w_ada.shape[0]
    assert depth == 1, "single-layer step"
    bsz, seq, d = x_prompt.shape
    dec_b, dec_s, _ = x_sample.shape
    past = cache_k.shape[2]
    n_heads = cache_k.shape[3]
    d_att = n_heads * HEAD_DIM
    n_groups, n_state = a_re.shape[1], a_re.shape[2]
    mod_rows = 16
    assert dec_b + bsz <= mod_rows
    l = 0

    c_rows = jnp.concatenate([c_sample, c_prompt, jnp.zeros((mod_rows - dec_b - bsz, d), F32)], axis=0)
    mod4 = _modulation(c_rows, w_ada[l], b_ada[l][None]).reshape(mod_rows, N_MOD, 1, d)
    row_sample, row_prompt = 0, dec_b

    w_qkv = w_in[l, :, 0:3 * d_att].astype(BF16)
    w_f = jnp.pad(w_in[l, :, 3 * d_att:3 * d_att + n_heads], ((0, 0), (0, LANES - n_heads))).astype(BF16)
    w_u = w_in[l, :, 3 * d_att + n_heads:].astype(BF16)
    b_f_pad = jnp.pad(b_f[l], (0, LANES - n_heads))[None]
    g1 = g_norm1[l][None]
    gq, gk = g_q[l][None], g_k[l][None]

    wb32, wc32, pw_re, pw_im = _s5_params(log_dt[l], a_re[l], a_im[l], b_re[l], b_im[l], c_re[l], c_im[l])
    wb, wc = wb32.astype(BF16), wc32.astype(BF16)
    pw_re = jnp.repeat(pw_re.reshape(N_POW, n_groups * n_state), SUBLANES, axis=0)
    pw_im = jnp.repeat(pw_im.reshape(N_POW, n_groups * n_state), SUBLANES, axis=0)
    d_row = d_skip[l][None]
    w_glu_b = w_glu[l].astype(BF16)
    b_glu_row = b_glu[l][None]
    g_ssm_row = g_ssm_out[l][None]
    g_att_row = g_att_out[l][None]
    w_out_b = w_out[l].astype(BF16)
    g2 = g_norm2[l][None]
    w1 = w_ff1[l].astype(BF16)
    w2 = w_ff2[l].astype(BF16)

    tm = 512
    tc_prompt = SUBLANES * MAX_SEG
    tc_sample = dec_s

    q_p, k32_p, kbf_p, v32_p, vbf_p, logf_p, u_p = _in_proj(
        x_prompt, mod4, row_prompt, g1, w_qkv, w_u, w_f, gq, gk, b_f_pad,
        nb=1, t=tm, n_heads=n_heads, s5_chunk=tc_prompt)
    fcum_p, _ = _forget_cumsum(logf_p)
    att_p = _prompt_attention(q_p, kbf_p, vbf_p, fcum_p, n_heads=n_heads, tq=1024, tk=512, heads_per_step=1)
    zeros_state = jnp.zeros((bsz, 1, n_groups * n_state), F32)
    ssm_p, hre_p, him_p = _s5_mixer(u_p, zeros_state, zeros_state, wb, wc, pw_re, pw_im, d_row, w_glu_b,
                                    b_glu_row, g_ssm_row, tc=tc_prompt, chunks_per_step=tm // tc_prompt)
    x1_p, h2_p = _out_proj(att_p, ssm_p, x_prompt, mod4, row_prompt, g_att_row, w_out_b, g2, nb=1, t=tm)
    y_p = _mlp(h2_p, x1_p, mod4, row_prompt, w1, w2, nb=1, t=tm, tf=1024)

    q_s, k32_s, kbf_s, v32_s, vbf_s, logf_s, u_s = _in_proj(
        x_sample, mod4, row_sample, g1, w_qkv, w_u, w_f, gq, gk, b_f_pad,
        nb=dec_b, t=dec_s, n_heads=n_heads, s5_chunk=tc_sample)
    logf_all = jnp.concatenate([cache_logf[l], logf_s], axis=1)
    fcum_s, fhead_s = _forget_cumsum(logf_all)
    att_s = _sample_attention(q_s, kbf_s, vbf_s, cache_k[l].reshape(dec_b, past * n_heads, HEAD_DIM),
                              cache_v[l].reshape(dec_b, past * n_heads, HEAD_DIM), fcum_s, fhead_s,
                              n_heads=n_heads)
    ssm_s, hre_s, him_s = _s5_mixer(u_s, state_ssm_re[l].reshape(dec_b, 1, -1),
                                    state_ssm_im[l].reshape(dec_b, 1, -1), wb, wc, pw_re, pw_im, d_row,
                                    w_glu_b, b_glu_row, g_ssm_row, tc=tc_sample, chunks_per_step=1)
    x1_s, h2_s = _out_proj(att_s, ssm_s, x_sample, mod4, row_sample, g_att_row, w_out_b, g2,
                           nb=dec_b, t=dec_s)
    y_s = _mlp(h2_s, x1_s, mod4, row_sample, w1, w2, nb=dec_b, t=dec_s, tf=1024)

    states = lambda a, n: a.reshape(1, n, n_groups, n_state)
    heads = lambda a, n, s: a.reshape(1, n, s, n_heads, HEAD_DIM)
    return (y_p, y_s,
            heads(k32_p, bsz, seq), heads(v32_p, bsz, seq), logf_p[None],
            states(hre_p, bsz), states(him_p, bsz),
            heads(k32_s, dec_b, dec_s), heads(v32_s, dec_b, dec_s), logf_s[None],
            states(hre_s, dec_b), states(him_s, dec_b))
```

```python
import functools
import math

import jax
import jax.numpy as jnp
from jax import lax
from jax.experimental import pallas as pl
from jax.experimental.pallas import tpu as pltpu

F32 = jnp.float32
BF16 = jnp.bfloat16
HIGHEST = lax.Precision.HIGHEST

HEAD_DIM = 128
SSM_GROUP = 16
N_MOD = 6
EPS = 1e-6
LOG2E = 1.4426950408889634

LANES = 128
SUBLANES = 8
GROUPS_PER_BLOCK = 16
MAX_SEG = 32
N_POW = MAX_SEG + 2
VMEM_LIMIT = 56 * 1024 * 1024
MLP_VMEM_LIMIT = 58 * 1024 * 1024


def _params(*sem):
    return pltpu.CompilerParams(dimension_semantics=sem, vmem_limit_bytes=VMEM_LIMIT)


def _resident(shape):
    nd = len(shape)
    return pl.BlockSpec(shape, lambda *_: (0,) * nd, pipeline_mode=pl.Buffered(1))


def _sigmoid(x):
    return 1.0 / (1.0 + jnp.exp(-x))


def _rms_rows(x, g):
    return x * lax.rsqrt(jnp.mean(x * x, axis=-1, keepdims=True) + EPS) * g


def _dot_nt(a, b):
    return lax.dot_general(a, b, (((1,), (1,)), ((), ())), preferred_element_type=F32)


def _cmul_add(x_re, x_im, a_re, a_im, h_re, h_im):
    return x_re + (a_re * h_re - a_im * h_im), x_im + (a_re * h_im + a_im * h_re)


def _stream_permutation(n, inverse):
    i = lax.broadcasted_iota(jnp.int32, (n, n), 0)
    j = lax.broadcasted_iota(jnp.int32, (n, n), 1)
    if inverse:
        i, j = j, i
    src = (i & (SUBLANES - 1)) * (n // SUBLANES) + (i >> 3)
    return jnp.where(j == src, 1.0, 0.0).astype(BF16)


def _split_bf16(x):
    hi = x.astype(BF16)
    return hi, (x - hi.astype(F32)).astype(BF16)


def _mod_kernel(c_ref, w_ref, b_ref, o_ref):
    c = c_ref[...]
    rows = c.shape[0]
    s_hi, s_lo = _split_bf16(c * _sigmoid(c))
    w_hi, w_lo = _split_bf16(w_ref[...])
    first = jnp.dot(jnp.concatenate([s_hi, s_lo], axis=0), w_hi, preferred_element_type=F32)
    second = jnp.dot(s_hi, w_lo, preferred_element_type=F32)
    o_ref[...] = first[0:rows] + (first[rows:2 * rows] + second) + b_ref[...]


def _modulation(c_rows, w_ada, b_ada):
    rows, d = c_rows.shape
    n = w_ada.shape[1]
    tn = 1024
    return pl.pallas_call(
        _mod_kernel,
        out_shape=jax.ShapeDtypeStruct((rows, n), F32),
        grid=(n // tn,),
        in_specs=[pl.BlockSpec((rows, d), lambda j: (0, 0)),
                  pl.BlockSpec((d, tn), lambda j: (0, j)),
                  pl.BlockSpec((1, tn), lambda j: (0, j))],
        out_specs=pl.BlockSpec((rows, tn), lambda j: (0, j)),
        compiler_params=_params("arbitrary"),
        name="modulation",
    )(c_rows, w_ada, b_ada)


def _zoh(log_dt, a_re, a_im):
    dt = jnp.exp(log_dt)
    lam_re = jnp.minimum(a_re, -1e-4)
    lam_im = a_im
    mag = jnp.exp(lam_re * dt)
    ang = lam_im * dt
    ab_re = mag * jnp.cos(ang)
    ab_im = mag * jnp.sin(ang)
    den = lam_re * lam_re + lam_im * lam_im
    z_re = ((ab_re - 1.0) * lam_re + ab_im * lam_im) / den
    z_im = (ab_im * lam_re - (ab_re - 1.0) * lam_im) / den
    return ab_re, ab_im, z_re, z_im


def _s5_param_kernel(logdt_ref, are_ref, aim_ref, logdt2_ref, are2_ref, aim2_ref, bre_ref, bim_ref, cre_ref,
                     cim_ref, wb_ref, wc_ref, pwre_ref, pwim_ref):
    n_groups, n_ch, n_st = bre_ref.shape
    gpb = GROUPS_PER_BLOCK
    ns = gpb * n_st
    _, _, z_re, z_im = _zoh(logdt_ref[...], are_ref[...], aim_ref[...])
    ab_re, ab_im, _, _ = _zoh(logdt2_ref[...], are2_ref[...], aim2_ref[...])
    b_re = bre_ref[...]
    b_im = bim_ref[...]
    zb_re = z_re * b_re - z_im * b_im
    zb_im = z_re * b_im + z_im * b_re
    wb_ref[...] = jnp.zeros(wb_ref.shape, F32)
    wc_ref[...] = jnp.zeros(wc_ref.shape, F32)
    for g in range(n_groups):
        blk, gi = divmod(g, gpb)
        chs = slice(gi * n_ch, (gi + 1) * n_ch)
        sts = slice(gi * n_st, (gi + 1) * n_st)
        sts_im = slice(ns + gi * n_st, ns + (gi + 1) * n_st)
        wb_ref[blk, chs, sts] = zb_re[g]
        wb_ref[blk, chs, sts_im] = zb_im[g]
        wc_ref[blk, sts, chs] = cre_ref[g]
        wc_ref[blk, sts_im, chs] = -cim_ref[g]
    p_re, p_im = ab_re, ab_im
    for k in range(MAX_SEG):
        pwre_ref[k] = p_re
        pwim_ref[k] = p_im
        if k + 1 < MAX_SEG:
            p_re, p_im = p_re * ab_re - p_im * ab_im, p_re * ab_im + p_im * ab_re
    for k in range(MAX_SEG, N_POW):
        p_re, p_im = p_re * p_re - p_im * p_im, 2.0 * (p_re * p_im)
        pwre_ref[k] = p_re
        pwim_ref[k] = p_im


def _s5_params(log_dt, a_re, a_im, b_re, b_im, c_re, c_im):
    g, n, c = b_re.shape
    n_blk = g // GROUPS_PER_BLOCK
    ch, ns = GROUPS_PER_BLOCK * c, GROUPS_PER_BLOCK * n
    f = lambda shape: jax.ShapeDtypeStruct(shape, F32)
    return pl.pallas_call(
        _s5_param_kernel,
        out_shape=(f((n_blk, ch, 2 * ns)), f((n_blk, 2 * ns, ch)), f((N_POW, g, n)), f((N_POW, g, n))),
        compiler_params=_params(),
        name="s5_params",
    )(log_dt.reshape(g, 1, 1), a_re.reshape(g, 1, n), a_im.reshape(g, 1, n), log_dt.reshape(g, 1), a_re, a_im,
      jnp.swapaxes(b_re, 1, 2), jnp.swapaxes(b_im, 1, 2), jnp.swapaxes(c_re, 1, 2), jnp.swapaxes(c_im, 1, 2))


def _pow_row(seg_rows):
    if seg_rows <= MAX_SEG:
        return seg_rows - 1
    return MAX_SEG - 1 + int(math.log2(seg_rows // MAX_SEG))


def _in_proj_kernel(x_ref, sh_ref, sc_ref, g1_ref, wqkv_ref, wu_ref, wf_ref, gq_ref, gk_ref, bf_ref,
                    q_ref, k32_ref, kbf_ref, v32_ref, vbf_ref, logf_ref, u_ref,
                    *, n_heads, q_scale, s5_chunk):
    nb, t, d = x_ref.shape
    rows = nb * t
    d_att = n_heads * HEAD_DIM
    x = x_ref[...]
    y = _rms_rows(x, g1_ref[...])
    h = (y * (1.0 + sc_ref[:, 0]) + sh_ref[:, 0]).reshape(rows, d)
    hb = h.astype(BF16)

    q = jnp.dot(hb, wqkv_ref[:, 0:d_att], preferred_element_type=F32)
    for hh in range(n_heads):
        sl = slice(hh * HEAD_DIM, (hh + 1) * HEAD_DIM)
        qn = _rms_rows(q[:, sl], gq_ref[...]) * q_scale
        q_ref[:, :, sl] = qn.reshape(nb, t, HEAD_DIM).astype(BF16)

    k = jnp.dot(hb, wqkv_ref[:, d_att:2 * d_att], preferred_element_type=F32)
    for hh in range(n_heads):
        sl = slice(hh * HEAD_DIM, (hh + 1) * HEAD_DIM)
        kn = _rms_rows(k[:, sl], gk_ref[...])
        kbf_ref[:, :, sl] = kn.reshape(nb, t, HEAD_DIM).astype(BF16)
        for bi in range(nb):
            k32_ref[bi, pl.ds(hh, t, stride=n_heads), :] = kn[bi * t:(bi + 1) * t]

    v = jnp.dot(hb, wqkv_ref[:, 2 * d_att:3 * d_att], preferred_element_type=F32)
    for hh in range(n_heads):
        sl = slice(hh * HEAD_DIM, (hh + 1) * HEAD_DIM)
        for bi in range(nb):
            v32_ref[bi, pl.ds(hh, t, stride=n_heads), :] = v[bi * t:(bi + 1) * t, sl]
    vbf_ref[...] = v.reshape(nb, t, d_att).astype(BF16)

    f = jnp.dot(hb, wf_ref[...], preferred_element_type=F32) + bf_ref[...]
    logf = jnp.minimum(f, 0.0) - jnp.log1p(jnp.exp(-jnp.abs(f)))
    logf_ref[...] = logf[:, 0:n_heads].reshape(nb, t, n_heads)

    u = jnp.dot(hb, wu_ref[...], preferred_element_type=F32).astype(BF16)
    perm = _stream_permutation(s5_chunk, inverse=False)
    for c in range(rows // s5_chunk):
        uc = jnp.dot(perm, u[c * s5_chunk:(c + 1) * s5_chunk], preferred_element_type=F32).astype(BF16)
        if nb == 1:
            u_ref[0, c * s5_chunk:(c + 1) * s5_chunk, :] = uc
        else:
            per = t // s5_chunk
            u_ref[c // per, (c % per) * s5_chunk:(c % per + 1) * s5_chunk, :] = uc


def _in_proj(x, mod4, mod_row, g1, w_qkv, w_u, w_f, g_q, g_k, b_f_pad, *, nb, t, n_heads, s5_chunk):
    n_seq, seq, d = x.shape
    d_att = n_heads * HEAD_DIM
    d_ssm = w_u.shape[1]
    assert t % s5_chunk == 0
    grid = (n_seq // nb, seq // t)
    row_blk = mod_row // nb
    act = lambda width: pl.BlockSpec((nb, t, width), lambda i, j: (i, j, 0))
    act4 = pl.BlockSpec((nb, t * n_heads, HEAD_DIM), lambda i, j: (i, j, 0))
    mod = lambda m: pl.BlockSpec((nb, 1, 1, d), lambda i, j, m=m: (row_blk + i, m, 0, 0))
    sds = lambda width, dt: jax.ShapeDtypeStruct((n_seq, seq, width), dt)
    sds4 = jax.ShapeDtypeStruct((n_seq, seq * n_heads, HEAD_DIM), F32)
    kern = functools.partial(_in_proj_kernel, n_heads=n_heads, q_scale=HEAD_DIM ** -0.5 * LOG2E,
                             s5_chunk=s5_chunk)
    return pl.pallas_call(
        kern,
        out_shape=(sds(d_att, BF16), sds4, sds(d_att, BF16), sds4, sds(d_att, BF16),
                   sds(n_heads, F32), sds(d_ssm, BF16)),
        grid=grid,
        in_specs=[act(d), mod(0), mod(1), _resident(g1.shape), _resident(w_qkv.shape), _resident(w_u.shape),
                  _resident(w_f.shape), _resident(g_q.shape), _resident(g_k.shape), _resident(b_f_pad.shape)],
        out_specs=(act(d_att), act4, act(d_att), act4, act(d_att), act(n_heads), act(d_ssm)),
        compiler_params=_params("parallel", "parallel"),
        name="in_proj",
    )(x, mod4, mod4, g1, w_qkv, w_u, w_f, g_q, g_k, b_f_pad)


def _cumsum_kernel(x_ref, o_ref, *, n_chunks):
    x = x_ref[...]
    n = x.shape[0]
    li = lax.broadcasted_iota(jnp.int32, (LANES, LANES), 0)
    lj = lax.broadcasted_iota(jnp.int32, (LANES, LANES), 1)
    tri = jnp.where(li <= lj, 1.0, 0.0)
    within = jnp.dot(x, tri, precision=HIGHEST, preferred_element_type=F32)
    tot = jnp.dot(x, jnp.ones((LANES, LANES), F32), precision=HIGHEST, preferred_element_type=F32)
    r = lax.broadcasted_iota(jnp.int32, (n, n), 0)
    c = lax.broadcasted_iota(jnp.int32, (n, n), 1)
    seq_of = lambda i: jnp.floor((i.astype(F32) + 0.5) * (1.0 / n_chunks))
    earlier = jnp.where(seq_of(r) == seq_of(c), jnp.where(c < r, 1.0, 0.0), 0.0)
    before = jnp.dot(earlier, tot, precision=HIGHEST, preferred_element_type=F32)
    o_ref[...] = (within + before) * LOG2E


def _forget_cumsum(logf):
    b, length, h = logf.shape
    n_chunks = -(-length // LANES)
    rows = jnp.swapaxes(logf, 1, 2).reshape(b * h, length)
    rows = jnp.pad(rows, ((0, 0), (0, n_chunks * LANES - length)))
    out = pl.pallas_call(
        functools.partial(_cumsum_kernel, n_chunks=n_chunks),
        out_shape=jax.ShapeDtypeStruct((b * h * n_chunks, LANES), F32),
        compiler_params=_params(),
        name="forget_cumsum",
    )(rows.reshape(b * h * n_chunks, LANES))
    by_head = out.reshape(b, h, n_chunks * LANES)
    return jnp.swapaxes(by_head[:, :, :length], 1, 2), by_head


def _head_column(f_rows, head):
    lane_h = lax.broadcasted_iota(jnp.int32, f_rows.shape, 1)
    return jnp.sum(jnp.where(lane_h == head, f_rows, 0.0), axis=-1, keepdims=True)


def _bias_columns(f_rows, head, key_side):
    col = _head_column(f_rows, head)
    hi = col.astype(BF16).astype(F32)
    rest = col - hi
    mid = rest.astype(BF16).astype(F32)
    lo = rest - mid
    lane = lax.broadcasted_iota(jnp.int32, (f_rows.shape[0], LANES), 1)
    if key_side:
        split = jnp.where(lane == 3, -hi, jnp.where(lane == 4, -mid, jnp.where(lane == 5, -lo, 0.0)))
        vals = jnp.where(lane < 3, 1.0, split)
    else:
        split = jnp.where(lane == 0, hi, jnp.where(lane == 1, mid, jnp.where(lane == 2, lo, 0.0)))
        vals = jnp.where((lane >= 3) & (lane < 6), 1.0, split)
    return vals.astype(BF16)


def _attn_kernel(q_ref, qn_ref, k_ref, v_ref, f_ref, o_ref, kaug_ref, vaug_ref, qa_ref, qnext_ref, s0_ref, s1_ref,
                 m_ref, acc_ref, *, tq, tk, build_rows, heads_per_step):
    qi = pl.program_id(2)
    t = k_ref.shape[1]
    dh = HEAD_DIM
    hps = heads_per_step
    head0 = pl.program_id(1) * hps

    def first_scores(src_ref, row0, dst_ref):
        f_rows = f_ref[0, pl.ds(row0, tq), :]
        for g in range(hps):
            qa = jnp.concatenate([src_ref[0, :, g * dh:(g + 1) * dh], _bias_columns(f_rows, head0 + g, False)],
                                 axis=-1)
            dst_ref[g] = qa
            s0_ref[g] = _dot_nt(qa, kaug_ref[g, 0:tk, :])

    def scores(c, s_ref, r0=0):
        off = pl.multiple_of(c * tk, tk)
        for g in range(hps):
            s_ref[g, r0:, :] = _dot_nt(qa_ref[g, r0:, :], kaug_ref[g, pl.ds(off, tk), :])

    @pl.when(qi == 0)
    def _first_block():
        for g in range(hps):
            kaug_ref[g, :, 0:dh] = k_ref[0, :, g * dh:(g + 1) * dh]
            vaug_ref[g, :, 0:dh] = v_ref[0, :, g * dh:(g + 1) * dh]
            vaug_ref[g, :, dh:2 * dh] = jnp.ones((t, dh), BF16)

        def piece(i, carry):
            r0 = pl.multiple_of(i * build_rows, build_rows)
            f_rows = f_ref[0, pl.ds(r0, build_rows), :]
            for g in range(hps):
                kaug_ref[g, pl.ds(r0, build_rows), dh:2 * dh] = _bias_columns(f_rows, head0 + g, True)
            return carry

        lax.fori_loop(0, t // build_rows, piece, 0)
        first_scores(q_ref, 0, qa_ref)

    @pl.when(qi > 0)
    def _take_prepared_queries():
        qa_ref[...] = qnext_ref[...]

    q0 = pl.multiple_of(qi * tq, tq)
    m_ref[...] = jnp.full(m_ref.shape, -jnp.inf, F32)
    acc_ref[...] = jnp.zeros(acc_ref.shape, F32)

    def absorb(c, s_ref, masked, r0=0):
        off = pl.multiple_of(c * tk, tk)
        for g in range(hps):
            s = s_ref[g, r0:, :]
            if masked:
                ahead = (lax.broadcasted_iota(jnp.int32, s.shape, 1)
                         - lax.broadcasted_iota(jnp.int32, s.shape, 0))
                s = jnp.where(ahead <= q0 + r0 - off, s, -jnp.inf)
            m_old = m_ref[g, r0:, :]
            m_new = jnp.maximum(m_old, jnp.max(s, axis=-1, keepdims=True))
            alpha = jnp.exp2(m_old - m_new)
            p = jnp.exp2(s - jnp.tile(m_new, (1, tk // LANES)))
            pv = jnp.dot(p.astype(BF16), vaug_ref[g, pl.ds(off, tk), :], preferred_element_type=F32)
            acc_ref[g, r0:, :] = jnp.tile(alpha, (1, 2)) * acc_ref[g, r0:, :] + pv
            m_ref[g, r0:, :] = m_new

    n_full = 2 * qi

    def pair(c):
        scores(c + 1, s1_ref)
        absorb(c, s0_ref, False)
        scores(c + 2, s0_ref)
        absorb(c + 1, s1_ref, False)

    def two_pairs(p, carry):
        pair(4 * p)
        pair(4 * p + 2)
        return carry

    lax.fori_loop(0, qi // 2, two_pairs, 0)

    @pl.when(qi % 2 == 1)
    def _odd_pair():
        pair(n_full - 2)

    scores(n_full + 1, s1_ref, r0=tk)
    absorb(n_full, s0_ref, True)
    first_scores(qn_ref, pl.multiple_of(jnp.minimum(qi + 1, pl.num_programs(2) - 1) * tq, tq), qnext_ref)
    absorb(n_full + 1, s1_ref, True, r0=tk)

    for g in range(hps):
        acc = acc_ref[g]
        o_ref[0, :, g * dh:(g + 1) * dh] = (acc[:, 0:dh] * (1.0 / acc[:, dh:2 * dh])).astype(o_ref.dtype)


def _prompt_attention(q, k, v, f_cum2, *, n_heads, tq, tk, heads_per_step):
    b, t, _ = q.shape
    assert t % tq == 0 and tq == 2 * tk and n_heads % heads_per_step == 0
    hps = heads_per_step
    width = hps * HEAD_DIM
    nq = t // tq
    q_blk = pl.BlockSpec((1, tq, width), lambda bi, h, i: (bi, i, h))
    qn_blk = pl.BlockSpec((1, tq, width), lambda bi, h, i: (bi, jnp.minimum(i + 1, nq - 1), h))
    kv_blk = pl.BlockSpec((1, t, width), lambda bi, h, i: (bi, 0, h))
    f_blk = pl.BlockSpec((1, t, n_heads), lambda bi, h, i: (bi, 0, 0), pipeline_mode=pl.Buffered(1))
    return pl.pallas_call(
        functools.partial(_attn_kernel, tq=tq, tk=tk, build_rows=1024, heads_per_step=hps),
        out_shape=jax.ShapeDtypeStruct(q.shape, BF16),
        grid=(b, n_heads // hps, nq),
        in_specs=[q_blk, qn_blk, kv_blk, kv_blk, f_blk],
        out_specs=q_blk,
        scratch_shapes=[pltpu.VMEM((hps, t, 2 * HEAD_DIM), BF16), pltpu.VMEM((hps, t, 2 * HEAD_DIM), BF16),
                        pltpu.VMEM((hps, tq, 2 * HEAD_DIM), BF16), pltpu.VMEM((hps, tq, 2 * HEAD_DIM), BF16),
                        pltpu.VMEM((hps, tq, tk), F32), pltpu.VMEM((hps, tq, tk), F32),
                        pltpu.VMEM((hps, tq, LANES), F32), pltpu.VMEM((hps, tq, 2 * HEAD_DIM), F32)],
        compiler_params=_params("parallel", "arbitrary", "arbitrary"),
        name="prompt_attention",
    )(q, q, k, v, f_cum2)


def _sample_attn_kernel(q_ref, kc_ref, vc_ref, kn_ref, vn_ref, fq_ref, fk_ref, o_ref, *, past, n_heads):
    s_len = q_ref.shape[1]
    f_new = fq_ref[0]
    row = lax.broadcasted_iota(jnp.int32, (s_len, s_len), 0)
    col = lax.broadcasted_iota(jnp.int32, (s_len, s_len), 1)
    for hh in range(n_heads):
        sl = slice(hh * HEAD_DIM, (hh + 1) * HEAD_DIM)
        q = q_ref[0, :, sl]
        fq = _head_column(f_new, hh)
        fk = fk_ref[0, hh:hh + 1, :]
        head_rows = pl.ds(hh, past, stride=n_heads)
        s_c = _dot_nt(q, kc_ref[0, head_rows, :].astype(BF16)) + fq - fk[:, 0:past]
        s_n = jnp.where(col <= row, _dot_nt(q, kn_ref[0, :, sl]) + fq - fk[:, past:past + s_len], -jnp.inf)
        m = jnp.maximum(jnp.max(s_c, axis=-1, keepdims=True), jnp.max(s_n, axis=-1, keepdims=True))
        p_c = jnp.exp2(s_c - m)
        p_n = jnp.exp2(s_n - m)
        l = jnp.sum(p_c, axis=-1, keepdims=True) + jnp.sum(p_n, axis=-1, keepdims=True)
        o = (jnp.dot(p_c.astype(BF16), vc_ref[0, head_rows, :].astype(BF16), preferred_element_type=F32)
             + jnp.dot(p_n.astype(BF16), vn_ref[0, :, sl], preferred_element_type=F32))
        o_ref[0, :, sl] = (o * (1.0 / l)).astype(o_ref.dtype)


def _sample_attention(q, k_new, v_new, cache_k, cache_v, f_cum2, f_by_head, *, n_heads):
    b, s_len, d_att = q.shape
    past = cache_k.shape[1] // n_heads
    assert past % s_len == 0
    new = pl.BlockSpec((1, s_len, d_att), lambda bi: (bi, 0, 0))
    old = pl.BlockSpec((1, past * n_heads, HEAD_DIM), lambda bi: (bi, 0, 0))
    return pl.pallas_call(
        functools.partial(_sample_attn_kernel, past=past, n_heads=n_heads),
        out_shape=jax.ShapeDtypeStruct(q.shape, BF16),
        grid=(b,),
        in_specs=[new, old, old, new, new,
                  pl.BlockSpec((1, s_len, n_heads), lambda bi: (bi, past // s_len, 0)),
                  pl.BlockSpec((1, n_heads, f_by_head.shape[-1]), lambda bi: (bi, 0, 0))],
        out_specs=new,
        compiler_params=_params("parallel"),
        name="sample_attention",
    )(q, cache_k, cache_v, k_new, v_new, f_cum2, f_by_head)


def _s5_kernel(u_ref, h0re_ref, h0im_ref, wb_ref, wc_ref, pwre_ref, pwim_ref, pbre_ref, pbim_ref, d_ref,
               wglu_ref, bglu_ref, gout_ref, o_ref, hre_ref, him_ref, bu_ref, xre_ref, xim_ref, hbre_ref, hbim_ref,
               y_ref, *, tc):
    n_rows = u_ref.shape[1]
    n_chunks = n_rows // tc
    seg = tc // SUBLANES
    n_blk, ch = wb_ref.shape[0], wb_ref.shape[1]
    ns = wb_ref.shape[2] // 2

    @pl.when(pl.program_id(1) == 0)
    def _():
        hre_ref[...] = h0re_ref[...]
        him_ref[...] = h0im_ref[...]

    def b_proj(blk):
        bu_ref[blk % 2] = jnp.dot(u_ref[0, :, blk * ch:(blk + 1) * ch], wb_ref[blk], preferred_element_type=F32)

    sub = lax.broadcasted_iota(jnp.int32, (SUBLANES, ns), 0)
    tab = lambda r: slice(r * SUBLANES, (r + 1) * SUBLANES)
    b_proj(0)
    for blk in range(n_blk):
        if blk + 1 < n_blk:
            b_proj(blk + 1)
        buf = blk % 2
        cols = slice(blk * ns, (blk + 1) * ns)
        chs = slice(blk * ch, (blk + 1) * ch)
        for c in range(n_chunks):
            _s5_scan_chunk(c, tc, seg, ns, buf, cols, sub, tab, bu_ref, xre_ref, xim_ref, hbre_ref, hbim_ref,
                           pwre_ref, pwim_ref, pbre_ref, pbim_ref, hre_ref, him_ref)
        y = (jnp.dot(hbre_ref[...], wc_ref[blk, 0:ns, :], preferred_element_type=F32)
             + jnp.dot(hbim_ref[...], wc_ref[blk, ns:2 * ns, :], preferred_element_type=F32))
        y_ref[:, chs] = y + d_ref[:, chs] * u_ref[0, :, chs].astype(F32)

    y = y_ref[...]
    g = y * (0.5 * (1.0 + jnp.tanh(math.sqrt(2.0 / math.pi) * (y + 0.044715 * (y * y * y)))))
    gate = _sigmoid(jnp.dot(g.astype(BF16), wglu_ref[...], preferred_element_type=F32) + bglu_ref[...])
    out = _rms_rows(g * gate, gout_ref[...]).astype(BF16)
    unperm = _stream_permutation(tc, inverse=True)
    for c in range(n_chunks):
        o_ref[0, c * tc:(c + 1) * tc, :] = jnp.dot(unperm, out[c * tc:(c + 1) * tc],
                                                    preferred_element_type=F32).astype(o_ref.dtype)


def _s5_scan_chunk(c, tc, seg, ns, buf, cols, sub, tab, bu_ref, xre_ref, xim_ref, hbre_ref, hbim_ref,
                   pwre_ref, pwim_ref, pbre_ref, pbim_ref, hre_ref, him_ref):
    rows = lambda r: slice(c * tc + r * SUBLANES, c * tc + (r + 1) * SUBLANES)

    a_re, a_im = pwre_ref[tab(0), cols], pwim_ref[tab(0), cols]
    h_re = bu_ref[buf, rows(0), 0:ns]
    h_im = bu_ref[buf, rows(0), ns:2 * ns]
    xre_ref[rows(0), :] = h_re
    xim_ref[rows(0), :] = h_im
    for r in range(1, seg):
        h_re, h_im = _cmul_add(bu_ref[buf, rows(r), 0:ns], bu_ref[buf, rows(r), ns:2 * ns], a_re, a_im, h_re, h_im)
        xre_ref[rows(r), :] = h_re
        xim_ref[rows(r), :] = h_im

    s_re = jnp.where(sub == 0, hre_ref[0, :, cols], pltpu.roll(h_re, 1, 0))
    s_im = jnp.where(sub == 0, him_ref[0, :, cols], pltpu.roll(h_im, 1, 0))
    for shift in (1, 2, 4):
        row = _pow_row(seg * shift)
        m_re = jnp.where(sub >= shift, pwre_ref[tab(row), cols], 0.0)
        m_im = jnp.where(sub >= shift, pwim_ref[tab(row), cols], 0.0)
        s_re, s_im = _cmul_add(s_re, s_im, m_re, m_im, pltpu.roll(s_re, shift, 0), pltpu.roll(s_im, shift, 0))

    s2_re = jnp.concatenate([s_re, s_re], axis=0).astype(BF16)
    s2_im = jnp.concatenate([s_im, s_im], axis=0).astype(BF16)
    for j in range(seg // 2):
        two = slice(c * tc + 2 * j * SUBLANES, c * tc + (2 * j + 2) * SUBLANES)
        tab2 = slice(2 * j * SUBLANES, (2 * j + 2) * SUBLANES)
        t_re, t_im = _cmul_add(xre_ref[two, :].astype(BF16), xim_ref[two, :].astype(BF16),
                               pbre_ref[tab2, cols], pbim_ref[tab2, cols], s2_re, s2_im)
        hbre_ref[two, :] = t_re
        hbim_ref[two, :] = t_im
    last = seg - 1
    c_re, c_im = _cmul_add(xre_ref[rows(last), :], xim_ref[rows(last), :],
                           pwre_ref[tab(last), cols], pwim_ref[tab(last), cols], s_re, s_im)
    hre_ref[0, :, cols] = c_re[SUBLANES - 1:SUBLANES, :]
    him_ref[0, :, cols] = c_im[SUBLANES - 1:SUBLANES, :]


def _s5_mixer(u, h0_re, h0_im, wb, wc, pw_re, pw_im, d_skip, w_glu, b_glu, g_out, *, tc, chunks_per_step):
    b, t, d_ssm = u.shape
    n_state = h0_re.shape[-1]
    rows = tc * chunks_per_step
    assert tc // SUBLANES <= MAX_SEG and t % rows == 0
    state = pl.BlockSpec((1, 1, n_state), lambda bi, j: (bi, 0, 0))
    act = pl.BlockSpec((1, rows, d_ssm), lambda bi, j: (bi, j, 0))
    ns = wb.shape[2] // 2
    pb_re = pw_re[0:MAX_SEG * SUBLANES].astype(BF16)
    pb_im = pw_im[0:MAX_SEG * SUBLANES].astype(BF16)
    return pl.pallas_call(
        functools.partial(_s5_kernel, tc=tc),
        out_shape=(jax.ShapeDtypeStruct((b, t, d_ssm), BF16),
                   jax.ShapeDtypeStruct((b, 1, n_state), F32), jax.ShapeDtypeStruct((b, 1, n_state), F32)),
        grid=(b, t // rows),
        in_specs=[act, state, state, _resident(wb.shape), _resident(wc.shape), _resident(pw_re.shape),
                  _resident(pw_im.shape), _resident(pb_re.shape), _resident(pb_im.shape),
                  _resident(d_skip.shape), _resident(w_glu.shape), _resident(b_glu.shape),
                  _resident(g_out.shape)],
        out_specs=(act, state, state),
        scratch_shapes=[pltpu.VMEM((2, rows, 2 * ns), F32), pltpu.VMEM((rows, ns), F32),
                        pltpu.VMEM((rows, ns), F32), pltpu.VMEM((rows, ns), BF16), pltpu.VMEM((rows, ns), BF16),
                        pltpu.VMEM((rows, d_ssm), F32)],
        compiler_params=_params("parallel", "arbitrary"),
        name="s5_mixer",
    )(u, h0_re, h0_im, wb, wc, pw_re, pw_im, pb_re, pb_im, d_skip, w_glu, b_glu, g_out)


def _out_proj_kernel(att_ref, ssm_ref, x_ref, gt1_ref, sh2_ref, sc2_ref, gatt_ref, wout_ref, g2_ref,
                     x1_ref, h2_ref):
    nb, t, d = x_ref.shape
    d_att, d_ssm = att_ref.shape[-1], ssm_ref.shape[-1]
    halves = [(slice(0, nb), slice(h * (t // 2), (h + 1) * (t // 2))) for h in range(2)] if nb == 1 else \
             [(slice(h * (nb // 2), (h + 1) * (nb // 2)), slice(0, t)) for h in range(2)]
    for bs, ts in halves:
        hb, ht = bs.stop - bs.start, ts.stop - ts.start
        rows = hb * ht
        a = att_ref[bs, ts, :].astype(F32).reshape(rows, d_att)
        an = _rms_rows(a, gatt_ref[...]).astype(BF16)
        mix = (jnp.dot(an, wout_ref[0:d_att, :], preferred_element_type=F32)
               + jnp.dot(ssm_ref[bs, ts, :].reshape(rows, d_ssm), wout_ref[d_att:, :],
                         preferred_element_type=F32))
        x1 = x_ref[bs, ts, :] + gt1_ref[bs, 0] * mix.reshape(hb, ht, d)
        x1_ref[bs, ts, :] = x1
        h2 = _rms_rows(x1, g2_ref[...]) * (1.0 + sc2_ref[bs, 0]) + sh2_ref[bs, 0]
        h2_ref[bs, ts, :] = h2.astype(BF16)


def _out_proj(att, ssm, x, mod4, mod_row, g_att, w_out, g2, *, nb, t):
    n_seq, seq, d = x.shape
    row_blk = mod_row // nb
    act = lambda width: pl.BlockSpec((nb, t, width), lambda i, j: (i, j, 0))
    mod = lambda m: pl.BlockSpec((nb, 1, 1, d), lambda i, j, m=m: (row_blk + i, m, 0, 0))
    return pl.pallas_call(
        _out_proj_kernel,
        out_shape=(jax.ShapeDtypeStruct(x.shape, F32), jax.ShapeDtypeStruct(x.shape, BF16)),
        grid=(n_seq // nb, seq // t),
        in_specs=[act(att.shape[-1]), act(ssm.shape[-1]), act(d), mod(2), mod(3), mod(4),
                  _resident(g_att.shape), _resident(w_out.shape), _resident(g2.shape)],
        out_specs=(act(d), act(d)),
        compiler_params=_params("parallel", "parallel"),
        name="out_proj",
    )(att, ssm, x, mod4, mod4, mod4, g_att, w_out, g2)


def _mlp_kernel(h2_ref, x1_ref, gt2_ref, w1_ref, w2_ref, o_ref, *, sub):
    nb, t, d = x1_ref.shape
    j = pl.program_id(2)
    tf = w1_ref.shape[1]

    @pl.when(j == 0)
    def _():
        o_ref[...] = jnp.zeros(o_ref.shape, F32)

    h2 = h2_ref[...].reshape(nb * t, d)
    for s in range(tf // sub):
        a = jnp.dot(h2, w1_ref[:, s * sub:(s + 1) * sub], preferred_element_type=F32)
        r = jnp.maximum(a, 0.0)
        o_ref[...] += jnp.dot((r * r).astype(BF16), w2_ref[s * sub:(s + 1) * sub, :],
                              preferred_element_type=F32).reshape(nb, t, d)

    @pl.when(j == pl.num_programs(2) - 1)
    def _():
        o_ref[...] = x1_ref[...] + gt2_ref[:, 0] * o_ref[...]


def _mlp(h2, x1, mod4, mod_row, w1, w2, *, nb, t, tf, sub):
    n_seq, seq, d = x1.shape
    d_ff = w1.shape[1]
    row_blk = mod_row // nb
    act = pl.BlockSpec((nb, t, d), lambda i, j, f: (i, j, 0))
    return pl.pallas_call(
        functools.partial(_mlp_kernel, sub=sub),
        out_shape=jax.ShapeDtypeStruct(x1.shape, F32),
        grid=(n_seq // nb, seq // t, d_ff // tf),
        in_specs=[act, act,
                  pl.BlockSpec((nb, 1, 1, d), lambda i, j, f: (row_blk + i, 5, 0, 0)),
                  pl.BlockSpec((d, tf), lambda i, j, f: (0, f)),
                  pl.BlockSpec((tf, d), lambda i, j, f: (f, 0))],
        out_specs=act,
        compiler_params=pltpu.CompilerParams(dimension_semantics=("parallel", "parallel", "arbitrary"),
                                             vmem_limit_bytes=MLP_VMEM_LIMIT),
        name="mlp",
    )(h2, x1, mod4, w1, w2)


def kernel(x_prompt, x_sample, c_prompt, c_sample, cache_k, cache_v, cache_logf, state_ssm_re, state_ssm_im,
           w_ada, b_ada, g_norm1, w_in, g_q, g_k, b_f, log_dt, a_re, a_im, b_re, b_im, c_re, c_im, d_skip,
           w_glu, b_glu, g_att_out, g_ssm_out, w_out, g_norm2, w_ff1, w_ff2):
    depth = w_ada.shape[0]
    assert depth == 1, "single-layer step"
    bsz, seq, d = x_prompt.shape
    dec_b, dec_s, _ = x_sample.shape
    past = cache_k.shape[2]
    n_heads = cache_k.shape[3]
    d_att = n_heads * HEAD_DIM
    n_groups, n_state = a_re.shape[1], a_re.shape[2]
    mod_rows = 16
    assert dec_b + bsz <= mod_rows
    l = 0

    c_rows = jnp.concatenate([c_sample, c_prompt, jnp.zeros((mod_rows - dec_b - bsz, d), F32)], axis=0)
    mod4 = _modulation(c_rows, w_ada[l], b_ada[l][None]).reshape(mod_rows, N_MOD, 1, d)
    row_sample, row_prompt = 0, dec_b

    w_qkv = w_in[l, :, 0:3 * d_att].astype(BF16)
    w_f = jnp.pad(w_in[l, :, 3 * d_att:3 * d_att + n_heads], ((0, 0), (0, LANES - n_heads))).astype(BF16)
    w_u = w_in[l, :, 3 * d_att + n_heads:].astype(BF16)
    b_f_pad = jnp.pad(b_f[l], (0, LANES - n_heads))[None]
    g1 = g_norm1[l][None]
    gq, gk = g_q[l][None], g_k[l][None]

    wb32, wc32, pw_re, pw_im = _s5_params(log_dt[l], a_re[l], a_im[l], b_re[l], b_im[l], c_re[l], c_im[l])
    wb, wc = wb32.astype(BF16), wc32.astype(BF16)
    pw_re = jnp.repeat(pw_re.reshape(N_POW, n_groups * n_state), SUBLANES, axis=0)
    pw_im = jnp.repeat(pw_im.reshape(N_POW, n_groups * n_state), SUBLANES, axis=0)
    d_row = d_skip[l][None]
    w_glu_b = w_glu[l].astype(BF16)
    b_glu_row = b_glu[l][None]
    g_ssm_row = g_ssm_out[l][None]
    g_att_row = g_att_out[l][None]
    w_out_b = w_out[l].astype(BF16)
    g2 = g_norm2[l][None]
    w1 = w_ff1[l].astype(BF16)
    w2 = w_ff2[l].astype(BF16)

    tm = 512
    tc_prompt = SUBLANES * MAX_SEG
    tc_sample = dec_s

    q_p, k32_p, kbf_p, v32_p, vbf_p, logf_p, u_p = _in_proj(
        x_prompt, mod4, row_prompt, g1, w_qkv, w_u, w_f, gq, gk, b_f_pad,
        nb=1, t=tm, n_heads=n_heads, s5_chunk=tc_prompt)
    fcum_p, _ = _forget_cumsum(logf_p)
    att_p = _prompt_attention(q_p, kbf_p, vbf_p, fcum_p, n_heads=n_heads, tq=1024, tk=512, heads_per_step=1)
    zeros_state = jnp.zeros((bsz, 1, n_groups * n_state), F32)
    ssm_p, hre_p, him_p = _s5_mixer(u_p, zeros_state, zeros_state, wb, wc, pw_re, pw_im, d_row, w_glu_b,
                                    b_glu_row, g_ssm_row, tc=tc_prompt, chunks_per_step=tm // tc_prompt)
    x1_p, h2_p = _out_proj(att_p, ssm_p, x_prompt, mod4, row_prompt, g_att_row, w_out_b, g2, nb=1, t=tm)
    y_p = _mlp(h2_p, x1_p, mod4, row_prompt, w1, w2, nb=1, t=tm, tf=2048, sub=1024)

    q_s, k32_s, kbf_s, v32_s, vbf_s, logf_s, u_s = _in_proj(
        x_sample, mod4, row_sample, g1, w_qkv, w_u, w_f, gq, gk, b_f_pad,
        nb=dec_b, t=dec_s, n_heads=n_heads, s5_chunk=tc_sample)
    logf_all = jnp.concatenate([cache_logf[l], logf_s], axis=1)
    fcum_s, fhead_s = _forget_cumsum(logf_all)
    att_s = _sample_attention(q_s, kbf_s, vbf_s, cache_k[l].reshape(dec_b, past * n_heads, HEAD_DIM),
                              cache_v[l].reshape(dec_b, past * n_heads, HEAD_DIM), fcum_s, fhead_s,
                              n_heads=n_heads)
    ssm_s, hre_s, him_s = _s5_mixer(u_s, state_ssm_re[l].reshape(dec_b, 1, -1),
                                    state_ssm_im[l].reshape(dec_b, 1, -1), wb, wc, pw_re, pw_im, d_row,
                                    w_glu_b, b_glu_row, g_ssm_row, tc=tc_sample, chunks_per_step=1)
    x1_s, h2_s = _out_proj(att_s, ssm_s, x_sample, mod4, row_sample, g_att_row, w_out_b, g2,
                           nb=dec_b, t=dec_s)
    y_s = _mlp(h2_s, x1_s, mod4, row_sample, w1, w2, nb=dec_b, t=dec_s, tf=2048, sub=1024)

    states = lambda a, n: a.reshape(1, n, n_groups, n_state)
    heads = lambda a, n, s: a.reshape(1, n, s, n_heads, HEAD_DIM)
    return (y_p, y_s,
            heads(k32_p, bsz, seq), heads(v32_p, bsz, seq), logf_p[None],
            states(hre_p, bsz), states(him_p, bsz),
            heads(k32_s, dec_b, dec_s), heads(v32_s, dec_b, dec_s), logf_s[None],
            states(hre_s, dec_b), states(him_s, dec_b))
```

```python
import functools
import math

import jax
import jax.numpy as jnp
from jax import lax
from jax.experimental import pallas as pl
from jax.experimental.pallas import tpu as pltpu

F32 = jnp.float32
BF16 = jnp.bfloat16
HIGHEST = lax.Precision.HIGHEST

HEAD_DIM = 128
N_MOD = 6
EPS = 1e-6
LOG2E = 1.4426950408889634

LANES = 128
SUBLANES = 8
GROUPS_PER_BLOCK = 16
MAX_SEG = 32
N_POW = MAX_SEG + 2
VMEM_LIMIT = 56 * 1024 * 1024
MLP_VMEM_LIMIT = 58 * 1024 * 1024

ROW_TILE = 512
MOD_COLS = 1024
ATTN_TQ = 1024
ATTN_TK = ATTN_TQ // 2
ATTN_BUILD_ROWS = 1024
MLP_TF = 2048
MLP_SUB = 1024


def _params(*sem):
    return pltpu.CompilerParams(dimension_semantics=sem, vmem_limit_bytes=VMEM_LIMIT)


def _resident(shape):
    nd = len(shape)
    return pl.BlockSpec(shape, lambda *_: (0,) * nd, pipeline_mode=pl.Buffered(1))


def _sigmoid(x):
    return 1.0 / (1.0 + jnp.exp(-x))


def _rms_rows(x, g):
    return x * lax.rsqrt(jnp.mean(x * x, axis=-1, keepdims=True) + EPS) * g


def _dot_nt(a, b):
    return lax.dot_general(a, b, (((1,), (1,)), ((), ())), preferred_element_type=F32)


def _cmul_add(x_re, x_im, a_re, a_im, h_re, h_im):
    return x_re + (a_re * h_re - a_im * h_im), x_im + (a_re * h_im + a_im * h_re)


def _stream_permutation(n, inverse):
    i = lax.broadcasted_iota(jnp.int32, (n, n), 0)
    j = lax.broadcasted_iota(jnp.int32, (n, n), 1)
    if inverse:
        i, j = j, i
    src = (i & (SUBLANES - 1)) * (n // SUBLANES) + (i >> (SUBLANES.bit_length() - 1))
    return jnp.where(j == src, 1.0, 0.0).astype(BF16)


def _split_bf16(x):
    hi = x.astype(BF16)
    return hi, (x - hi.astype(F32)).astype(BF16)


def _mod_kernel(c_ref, w_ref, b_ref, o_ref):
    c = c_ref[...]
    rows = c.shape[0]
    s_hi, s_lo = _split_bf16(c * _sigmoid(c))
    w_hi, w_lo = _split_bf16(w_ref[...])
    first = jnp.dot(jnp.concatenate([s_hi, s_lo], axis=0), w_hi, preferred_element_type=F32)
    second = jnp.dot(s_hi, w_lo, preferred_element_type=F32)
    o_ref[...] = first[0:rows] + (first[rows:2 * rows] + second) + b_ref[...]


def _modulation(c_rows, w_ada, b_ada):
    rows, d = c_rows.shape
    n = w_ada.shape[1]
    tn = MOD_COLS
    return pl.pallas_call(
        _mod_kernel,
        out_shape=jax.ShapeDtypeStruct((rows, n), F32),
        grid=(n // tn,),
        in_specs=[pl.BlockSpec((rows, d), lambda j: (0, 0)),
                  pl.BlockSpec((d, tn), lambda j: (0, j)),
                  pl.BlockSpec((1, tn), lambda j: (0, j))],
        out_specs=pl.BlockSpec((rows, tn), lambda j: (0, j)),
        compiler_params=_params("arbitrary"),
        name="modulation",
    )(c_rows, w_ada, b_ada)


def _zoh(log_dt, a_re, a_im):
    dt = jnp.exp(log_dt)
    lam_re = jnp.minimum(a_re, -1e-4)
    lam_im = a_im
    mag = jnp.exp(lam_re * dt)
    ang = lam_im * dt
    ab_re = mag * jnp.cos(ang)
    ab_im = mag * jnp.sin(ang)
    den = lam_re * lam_re + lam_im * lam_im
    z_re = ((ab_re - 1.0) * lam_re + ab_im * lam_im) / den
    z_im = (ab_im * lam_re - (ab_re - 1.0) * lam_im) / den
    return ab_re, ab_im, z_re, z_im


def _s5_param_kernel(logdt_ref, are_ref, aim_ref, logdt2_ref, are2_ref, aim2_ref, bre_ref, bim_ref, cre_ref,
                     cim_ref, wb_ref, wc_ref, pwre_ref, pwim_ref):
    n_groups, n_ch, n_st = bre_ref.shape
    gpb = GROUPS_PER_BLOCK
    ns = gpb * n_st
    _, _, z_re, z_im = _zoh(logdt_ref[...], are_ref[...], aim_ref[...])
    ab_re, ab_im, _, _ = _zoh(logdt2_ref[...], are2_ref[...], aim2_ref[...])
    b_re = bre_ref[...]
    b_im = bim_ref[...]
    zb_re = z_re * b_re - z_im * b_im
    zb_im = z_re * b_im + z_im * b_re
    wb_ref[...] = jnp.zeros(wb_ref.shape, F32)
    wc_ref[...] = jnp.zeros(wc_ref.shape, F32)
    for g in range(n_groups):
        blk, gi = divmod(g, gpb)
        chs = slice(gi * n_ch, (gi + 1) * n_ch)
        sts = slice(gi * n_st, (gi + 1) * n_st)
        sts_im = slice(ns + gi * n_st, ns + (gi + 1) * n_st)
        wb_ref[blk, chs, sts] = zb_re[g]
        wb_ref[blk, chs, sts_im] = zb_im[g]
        wc_ref[blk, sts, chs] = cre_ref[g]
        wc_ref[blk, sts_im, chs] = -cim_ref[g]
    p_re, p_im = ab_re, ab_im
    for k in range(MAX_SEG):
        pwre_ref[k] = p_re
        pwim_ref[k] = p_im
        if k + 1 < MAX_SEG:
            p_re, p_im = p_re * ab_re - p_im * ab_im, p_re * ab_im + p_im * ab_re
    for k in range(MAX_SEG, N_POW):
        p_re, p_im = p_re * p_re - p_im * p_im, 2.0 * (p_re * p_im)
        pwre_ref[k] = p_re
        pwim_ref[k] = p_im


def _s5_params(log_dt, a_re, a_im, b_re, b_im, c_re, c_im):
    g, n, c = b_re.shape
    n_blk = g // GROUPS_PER_BLOCK
    ch, ns = GROUPS_PER_BLOCK * c, GROUPS_PER_BLOCK * n
    f = lambda shape: jax.ShapeDtypeStruct(shape, F32)
    return pl.pallas_call(
        _s5_param_kernel,
        out_shape=(f((n_blk, ch, 2 * ns)), f((n_blk, 2 * ns, ch)), f((N_POW, g, n)), f((N_POW, g, n))),
        compiler_params=_params(),
        name="s5_params",
    )(log_dt.reshape(g, 1, 1), a_re.reshape(g, 1, n), a_im.reshape(g, 1, n), log_dt.reshape(g, 1), a_re, a_im,
      jnp.swapaxes(b_re, 1, 2), jnp.swapaxes(b_im, 1, 2), jnp.swapaxes(c_re, 1, 2), jnp.swapaxes(c_im, 1, 2))


def _pow_row(seg_rows):
    if seg_rows <= MAX_SEG:
        return seg_rows - 1
    return MAX_SEG - 1 + int(math.log2(seg_rows // MAX_SEG))


def _in_proj_kernel(x_ref, sh_ref, sc_ref, g1_ref, wqkv_ref, wu_ref, wf_ref, gq_ref, gk_ref, bf_ref,
                    q_ref, k32_ref, kbf_ref, v32_ref, vbf_ref, logf_ref, u_ref,
                    *, n_heads, q_scale, s5_chunk):
    nb, t, d = x_ref.shape
    rows = nb * t
    d_att = n_heads * HEAD_DIM
    x = x_ref[...]
    y = _rms_rows(x, g1_ref[...])
    h = (y * (1.0 + sc_ref[:, 0]) + sh_ref[:, 0]).reshape(rows, d)
    hb = h.astype(BF16)

    q = jnp.dot(hb, wqkv_ref[:, 0:d_att], preferred_element_type=F32)
    for hh in range(n_heads):
        sl = slice(hh * HEAD_DIM, (hh + 1) * HEAD_DIM)
        qn = _rms_rows(q[:, sl], gq_ref[...]) * q_scale
        q_ref[:, :, sl] = qn.reshape(nb, t, HEAD_DIM).astype(BF16)

    k = jnp.dot(hb, wqkv_ref[:, d_att:2 * d_att], preferred_element_type=F32)
    for hh in range(n_heads):
        sl = slice(hh * HEAD_DIM, (hh + 1) * HEAD_DIM)
        kn = _rms_rows(k[:, sl], gk_ref[...])
        kbf_ref[:, :, sl] = kn.reshape(nb, t, HEAD_DIM).astype(BF16)
        for bi in range(nb):
            k32_ref[bi, pl.ds(hh, t, stride=n_heads), :] = kn[bi * t:(bi + 1) * t]

    v = jnp.dot(hb, wqkv_ref[:, 2 * d_att:3 * d_att], preferred_element_type=F32)
    for hh in range(n_heads):
        sl = slice(hh * HEAD_DIM, (hh + 1) * HEAD_DIM)
        for bi in range(nb):
            v32_ref[bi, pl.ds(hh, t, stride=n_heads), :] = v[bi * t:(bi + 1) * t, sl]
    vbf_ref[...] = v.reshape(nb, t, d_att).astype(BF16)

    f = jnp.dot(hb, wf_ref[...], preferred_element_type=F32) + bf_ref[...]
    logf = jnp.minimum(f, 0.0) - jnp.log1p(jnp.exp(-jnp.abs(f)))
    logf_ref[...] = logf[:, 0:n_heads].reshape(nb, t, n_heads)

    u = jnp.dot(hb, wu_ref[...], preferred_element_type=F32).astype(BF16)
    perm = _stream_permutation(s5_chunk, inverse=False)
    for c in range(rows // s5_chunk):
        uc = jnp.dot(perm, u[c * s5_chunk:(c + 1) * s5_chunk], preferred_element_type=F32).astype(BF16)
        if nb == 1:
            u_ref[0, c * s5_chunk:(c + 1) * s5_chunk, :] = uc
        else:
            per = t // s5_chunk
            u_ref[c // per, (c % per) * s5_chunk:(c % per + 1) * s5_chunk, :] = uc


def _in_proj(x, mod4, mod_row, g1, w_qkv, w_u, w_f, g_q, g_k, b_f_pad, *, nb, t, n_heads, s5_chunk):
    n_seq, seq, d = x.shape
    d_att = n_heads * HEAD_DIM
    d_ssm = w_u.shape[1]
    assert t % s5_chunk == 0
    grid = (n_seq // nb, seq // t)
    row_blk = mod_row // nb
    act = lambda width: pl.BlockSpec((nb, t, width), lambda i, j: (i, j, 0))
    act4 = pl.BlockSpec((nb, t * n_heads, HEAD_DIM), lambda i, j: (i, j, 0))
    mod = lambda m: pl.BlockSpec((nb, 1, 1, d), lambda i, j, m=m: (row_blk + i, m, 0, 0))
    sds = lambda width, dt: jax.ShapeDtypeStruct((n_seq, seq, width), dt)
    sds4 = jax.ShapeDtypeStruct((n_seq, seq * n_heads, HEAD_DIM), F32)
    kern = functools.partial(_in_proj_kernel, n_heads=n_heads, q_scale=HEAD_DIM ** -0.5 * LOG2E,
                             s5_chunk=s5_chunk)
    return pl.pallas_call(
        kern,
        out_shape=(sds(d_att, BF16), sds4, sds(d_att, BF16), sds4, sds(d_att, BF16),
                   sds(n_heads, F32), sds(d_ssm, BF16)),
        grid=grid,
        in_specs=[act(d), mod(0), mod(1), _resident(g1.shape), _resident(w_qkv.shape), _resident(w_u.shape),
                  _resident(w_f.shape), _resident(g_q.shape), _resident(g_k.shape), _resident(b_f_pad.shape)],
        out_specs=(act(d_att), act4, act(d_att), act4, act(d_att), act(n_heads), act(d_ssm)),
        compiler_params=_params("parallel", "parallel"),
        name="in_proj",
    )(x, mod4, mod4, g1, w_qkv, w_u, w_f, g_q, g_k, b_f_pad)


def _cumsum_kernel(x_ref, o_ref, *, n_chunks):
    x = x_ref[...]
    n = x.shape[0]
    li = lax.broadcasted_iota(jnp.int32, (LANES, LANES), 0)
    lj = lax.broadcasted_iota(jnp.int32, (LANES, LANES), 1)
    tri = jnp.where(li <= lj, 1.0, 0.0)
    within = jnp.dot(x, tri, precision=HIGHEST, preferred_element_type=F32)
    tot = jnp.dot(x, jnp.ones((LANES, LANES), F32), precision=HIGHEST, preferred_element_type=F32)
    r = lax.broadcasted_iota(jnp.int32, (n, n), 0)
    c = lax.broadcasted_iota(jnp.int32, (n, n), 1)
    seq_of = lambda i: jnp.floor((i.astype(F32) + 0.5) * (1.0 / n_chunks))
    earlier = jnp.where(seq_of(r) == seq_of(c), jnp.where(c < r, 1.0, 0.0), 0.0)
    before = jnp.dot(earlier, tot, precision=HIGHEST, preferred_element_type=F32)
    o_ref[...] = (within + before) * LOG2E


def _forget_cumsum(logf):
    b, length, h = logf.shape
    n_chunks = -(-length // LANES)
    rows = jnp.swapaxes(logf, 1, 2).reshape(b * h, length)
    rows = jnp.pad(rows, ((0, 0), (0, n_chunks * LANES - length)))
    out = pl.pallas_call(
        functools.partial(_cumsum_kernel, n_chunks=n_chunks),
        out_shape=jax.ShapeDtypeStruct((b * h * n_chunks, LANES), F32),
        compiler_params=_params(),
        name="forget_cumsum",
    )(rows.reshape(b * h * n_chunks, LANES))
    by_head = out.reshape(b, h, n_chunks * LANES)
    return jnp.swapaxes(by_head[:, :, :length], 1, 2), by_head


def _head_column(f_rows, head):
    lane_h = lax.broadcasted_iota(jnp.int32, f_rows.shape, 1)
    return jnp.sum(jnp.where(lane_h == head, f_rows, 0.0), axis=-1, keepdims=True)


def _bias_columns(f_rows, head, key_side):
    col = _head_column(f_rows, head)
    hi = col.astype(BF16).astype(F32)
    rest = col - hi
    mid = rest.astype(BF16).astype(F32)
    lo = rest - mid
    lane = lax.broadcasted_iota(jnp.int32, (f_rows.shape[0], LANES), 1)
    if key_side:
        split = jnp.where(lane == 3, -hi, jnp.where(lane == 4, -mid, jnp.where(lane == 5, -lo, 0.0)))
        vals = jnp.where(lane < 3, 1.0, split)
    else:
        split = jnp.where(lane == 0, hi, jnp.where(lane == 1, mid, jnp.where(lane == 2, lo, 0.0)))
        vals = jnp.where((lane >= 3) & (lane < 6), 1.0, split)
    return vals.astype(BF16)


def _attn_kernel(q_ref, qn_ref, k_ref, v_ref, f_ref, o_ref, kaug_ref, vaug_ref, qa_ref, qnext_ref, s0_ref, s1_ref,
                 m_ref, acc_ref, *, tq, tk, build_rows, heads_per_step):
    qi = pl.program_id(2)
    t = k_ref.shape[1]
    dh = HEAD_DIM
    hps = heads_per_step
    head0 = pl.program_id(1) * hps

    def first_scores(src_ref, row0, dst_ref):
        f_rows = f_ref[0, pl.ds(row0, tq), :]
        for g in range(hps):
            qa = jnp.concatenate([src_ref[0, :, g * dh:(g + 1) * dh], _bias_columns(f_rows, head0 + g, False)],
                                 axis=-1)
            dst_ref[g] = qa
            s0_ref[g] = _dot_nt(qa, kaug_ref[g, 0:tk, :])

    def scores(c, s_ref, r0=0):
        off = pl.multiple_of(c * tk, tk)
        for g in range(hps):
            s_ref[g, r0:, :] = _dot_nt(qa_ref[g, r0:, :], kaug_ref[g, pl.ds(off, tk), :])

    @pl.when(qi == 0)
    def _first_block():
        for g in range(hps):
            kaug_ref[g, :, 0:dh] = k_ref[0, :, g * dh:(g + 1) * dh]
            vaug_ref[g, :, 0:dh] = v_ref[0, :, g * dh:(g + 1) * dh]
            vaug_ref[g, :, dh:2 * dh] = jnp.ones((t, dh), BF16)

        def piece(i, carry):
            r0 = pl.multiple_of(i * build_rows, build_rows)
            f_rows = f_ref[0, pl.ds(r0, build_rows), :]
            for g in range(hps):
                kaug_ref[g, pl.ds(r0, build_rows), dh:2 * dh] = _bias_columns(f_rows, head0 + g, True)
            return carry

        lax.fori_loop(0, t // build_rows, piece, 0)
        first_scores(q_ref, 0, qa_ref)

    @pl.when(qi > 0)
    def _take_prepared_queries():
        qa_ref[...] = qnext_ref[...]

    q0 = pl.multiple_of(qi * tq, tq)
    m_ref[...] = jnp.full(m_ref.shape, -jnp.inf, F32)
    acc_ref[...] = jnp.zeros(acc_ref.shape, F32)

    def absorb(c, s_ref, masked, r0=0):
        off = pl.multiple_of(c * tk, tk)
        for g in range(hps):
            s = s_ref[g, r0:, :]
            if masked:
                ahead = (lax.broadcasted_iota(jnp.int32, s.shape, 1)
                         - lax.broadcasted_iota(jnp.int32, s.shape, 0))
                s = jnp.where(ahead <= q0 + r0 - off, s, -jnp.inf)
            m_old = m_ref[g, r0:, :]
            m_new = jnp.maximum(m_old, jnp.max(s, axis=-1, keepdims=True))
            alpha = jnp.exp2(m_old - m_new)
            p = jnp.exp2(s - jnp.tile(m_new, (1, tk // LANES)))
            pv = jnp.dot(p.astype(BF16), vaug_ref[g, pl.ds(off, tk), :], preferred_element_type=F32)
            acc_ref[g, r0:, :] = jnp.tile(alpha, (1, 2)) * acc_ref[g, r0:, :] + pv
            m_ref[g, r0:, :] = m_new

    n_full = 2 * qi

    def pair(c):
        scores(c + 1, s1_ref)
        absorb(c, s0_ref, False)
        scores(c + 2, s0_ref)
        absorb(c + 1, s1_ref, False)

    def two_pairs(p, carry):
        pair(4 * p)
        pair(4 * p + 2)
        return carry

    lax.fori_loop(0, qi // 2, two_pairs, 0)

    @pl.when(qi % 2 == 1)
    def _odd_pair():
        pair(n_full - 2)

    scores(n_full + 1, s1_ref, r0=tk)
    absorb(n_full, s0_ref, True)
    first_scores(qn_ref, pl.multiple_of(jnp.minimum(qi + 1, pl.num_programs(2) - 1) * tq, tq), qnext_ref)
    absorb(n_full + 1, s1_ref, True, r0=tk)

    for g in range(hps):
        acc = acc_ref[g]
        o_ref[0, :, g * dh:(g + 1) * dh] = (acc[:, 0:dh] * (1.0 / acc[:, dh:2 * dh])).astype(o_ref.dtype)


def _prompt_attention(q, k, v, f_cum2, *, n_heads, tq, tk, heads_per_step):
    b, t, _ = q.shape
    assert t % tq == 0 and tq == 2 * tk and n_heads % heads_per_step == 0
    hps = heads_per_step
    width = hps * HEAD_DIM
    nq = t // tq
    q_blk = pl.BlockSpec((1, tq, width), lambda bi, h, i: (bi, i, h))
    qn_blk = pl.BlockSpec((1, tq, width), lambda bi, h, i: (bi, jnp.minimum(i + 1, nq - 1), h))
    kv_blk = pl.BlockSpec((1, t, width), lambda bi, h, i: (bi, 0, h))
    f_blk = pl.BlockSpec((1, t, n_heads), lambda bi, h, i: (bi, 0, 0), pipeline_mode=pl.Buffered(1))
    return pl.pallas_call(
        functools.partial(_attn_kernel, tq=tq, tk=tk, build_rows=ATTN_BUILD_ROWS, heads_per_step=hps),
        out_shape=jax.ShapeDtypeStruct(q.shape, BF16),
        grid=(b, n_heads // hps, nq),
        in_specs=[q_blk, qn_blk, kv_blk, kv_blk, f_blk],
        out_specs=q_blk,
        scratch_shapes=[pltpu.VMEM((hps, t, 2 * HEAD_DIM), BF16), pltpu.VMEM((hps, t, 2 * HEAD_DIM), BF16),
                        pltpu.VMEM((hps, tq, 2 * HEAD_DIM), BF16), pltpu.VMEM((hps, tq, 2 * HEAD_DIM), BF16),
                        pltpu.VMEM((hps, tq, tk), F32), pltpu.VMEM((hps, tq, tk), F32),
                        pltpu.VMEM((hps, tq, LANES), F32), pltpu.VMEM((hps, tq, 2 * HEAD_DIM), F32)],
        compiler_params=_params("parallel", "arbitrary", "arbitrary"),
        name="prompt_attention",
    )(q, q, k, v, f_cum2)


def _sample_attn_kernel(q_ref, kc_ref, vc_ref, kn_ref, vn_ref, fq_ref, fk_ref, o_ref, *, past, n_heads):
    s_len = q_ref.shape[1]
    f_new = fq_ref[0]
    row = lax.broadcasted_iota(jnp.int32, (s_len, s_len), 0)
    col = lax.broadcasted_iota(jnp.int32, (s_len, s_len), 1)
    for hh in range(n_heads):
        sl = slice(hh * HEAD_DIM, (hh + 1) * HEAD_DIM)
        q = q_ref[0, :, sl]
        fq = _head_column(f_new, hh)
        fk = fk_ref[0, hh:hh + 1, :]
        head_rows = pl.ds(hh, past, stride=n_heads)
        s_c = _dot_nt(q, kc_ref[0, head_rows, :].astype(BF16)) + fq - fk[:, 0:past]
        s_n = jnp.where(col <= row, _dot_nt(q, kn_ref[0, :, sl]) + fq - fk[:, past:past + s_len], -jnp.inf)
        m = jnp.maximum(jnp.max(s_c, axis=-1, keepdims=True), jnp.max(s_n, axis=-1, keepdims=True))
        p_c = jnp.exp2(s_c - m)
        p_n = jnp.exp2(s_n - m)
        l = jnp.sum(p_c, axis=-1, keepdims=True) + jnp.sum(p_n, axis=-1, keepdims=True)
        o = (jnp.dot(p_c.astype(BF16), vc_ref[0, head_rows, :].astype(BF16), preferred_element_type=F32)
             + jnp.dot(p_n.astype(BF16), vn_ref[0, :, sl], preferred_element_type=F32))
        o_ref[0, :, sl] = (o * (1.0 / l)).astype(o_ref.dtype)


def _sample_attention(q, k_new, v_new, cache_k, cache_v, f_cum2, f_by_head, *, n_heads):
    b, s_len, d_att = q.shape
    past = cache_k.shape[1] // n_heads
    assert past % s_len == 0
    new = pl.BlockSpec((1, s_len, d_att), lambda bi: (bi, 0, 0))
    old = pl.BlockSpec((1, past * n_heads, HEAD_DIM), lambda bi: (bi, 0, 0))
    return pl.pallas_call(
        functools.partial(_sample_attn_kernel, past=past, n_heads=n_heads),
        out_shape=jax.ShapeDtypeStruct(q.shape, BF16),
        grid=(b,),
        in_specs=[new, old, old, new, new,
                  pl.BlockSpec((1, s_len, n_heads), lambda bi: (bi, past // s_len, 0)),
                  pl.BlockSpec((1, n_heads, f_by_head.shape[-1]), lambda bi: (bi, 0, 0))],
        out_specs=new,
        compiler_params=_params("parallel"),
        name="sample_attention",
    )(q, cache_k, cache_v, k_new, v_new, f_cum2, f_by_head)


def _s5_kernel(u_ref, h0re_ref, h0im_ref, wb_ref, wc_ref, pwre_ref, pwim_ref, pbre_ref, pbim_ref, d_ref,
               wglu_ref, bglu_ref, gout_ref, o_ref, hre_ref, him_ref, bu_ref, xre_ref, xim_ref, hbre_ref, hbim_ref,
               y_ref, *, tc):
    nb, t, d_ssm = u_ref.shape
    n_rows = nb * t
    seg = tc // SUBLANES
    n_blk, ch = wb_ref.shape[0], wb_ref.shape[1]
    ns = wb_ref.shape[2] // 2
    chunks = [(bi, c) for bi in range(nb) for c in range(t // tc)]

    @pl.when(pl.program_id(1) == 0)
    def _():
        hre_ref[...] = h0re_ref[...]
        him_ref[...] = h0im_ref[...]

    u_cols = lambda blk: u_ref[:, :, blk * ch:(blk + 1) * ch].reshape(n_rows, ch)

    def b_proj(blk):
        bu_ref[blk % 2] = jnp.dot(u_cols(blk), wb_ref[blk], preferred_element_type=F32)

    sub = lax.broadcasted_iota(jnp.int32, (SUBLANES, ns), 0)
    tab = lambda r: slice(r * SUBLANES, (r + 1) * SUBLANES)
    b_proj(0)
    for blk in range(n_blk):
        if blk + 1 < n_blk:
            b_proj(blk + 1)
        buf = blk % 2
        cols = slice(blk * ns, (blk + 1) * ns)
        chs = slice(blk * ch, (blk + 1) * ch)
        for bi, c in chunks:
            _s5_scan_chunk(bi * t + c * tc, bi, seg, ns, buf, cols, sub, tab, bu_ref, xre_ref, xim_ref, hbre_ref,
                           hbim_ref, pwre_ref, pwim_ref, pbre_ref, pbim_ref, hre_ref, him_ref)
        y = (jnp.dot(hbre_ref[...], wc_ref[blk, 0:ns, :], preferred_element_type=F32)
             + jnp.dot(hbim_ref[...], wc_ref[blk, ns:2 * ns, :], preferred_element_type=F32))
        y_ref[:, chs] = y + d_ref[:, chs] * u_cols(blk).astype(F32)

    y = y_ref[...]
    g = y * (0.5 * (1.0 + jnp.tanh(math.sqrt(2.0 / math.pi) * (y + 0.044715 * (y * y * y)))))
    gate = _sigmoid(jnp.dot(g.astype(BF16), wglu_ref[...], preferred_element_type=F32) + bglu_ref[...])
    out = _rms_rows(g * gate, gout_ref[...]).astype(BF16)
    unperm = _stream_permutation(tc, inverse=True)
    for bi, c in chunks:
        base = bi * t + c * tc
        o_ref[bi, c * tc:(c + 1) * tc, :] = jnp.dot(unperm, out[base:base + tc],
                                                     preferred_element_type=F32).astype(o_ref.dtype)


def _s5_scan_chunk(base, bi, seg, ns, buf, cols, sub, tab, bu_ref, xre_ref, xim_ref, hbre_ref, hbim_ref,
                   pwre_ref, pwim_ref, pbre_ref, pbim_ref, hre_ref, him_ref):
    rows = lambda r: slice(base + r * SUBLANES, base + (r + 1) * SUBLANES)

    a_re, a_im = pwre_ref[tab(0), cols], pwim_ref[tab(0), cols]
    h_re = bu_ref[buf, rows(0), 0:ns]
    h_im = bu_ref[buf, rows(0), ns:2 * ns]
    xre_ref[rows(0), :] = h_re
    xim_ref[rows(0), :] = h_im
    for r in range(1, seg):
        h_re, h_im = _cmul_add(bu_ref[buf, rows(r), 0:ns], bu_ref[buf, rows(r), ns:2 * ns], a_re, a_im, h_re, h_im)
        xre_ref[rows(r), :] = h_re
        xim_ref[rows(r), :] = h_im

    s_re = jnp.where(sub == 0, hre_ref[bi, :, cols], pltpu.roll(h_re, 1, 0))
    s_im = jnp.where(sub == 0, him_ref[bi, :, cols], pltpu.roll(h_im, 1, 0))
    for shift in (1, 2, 4):
        row = _pow_row(seg * shift)
        m_re = jnp.where(sub >= shift, pwre_ref[tab(row), cols], 0.0)
        m_im = jnp.where(sub >= shift, pwim_ref[tab(row), cols], 0.0)
        s_re, s_im = _cmul_add(s_re, s_im, m_re, m_im, pltpu.roll(s_re, shift, 0), pltpu.roll(s_im, shift, 0))

    s2_re = jnp.concatenate([s_re, s_re], axis=0).astype(BF16)
    s2_im = jnp.concatenate([s_im, s_im], axis=0).astype(BF16)
    for j in range(seg // 2):
        two = slice(base + 2 * j * SUBLANES, base + (2 * j + 2) * SUBLANES)
        tab2 = slice(2 * j * SUBLANES, (2 * j + 2) * SUBLANES)
        t_re, t_im = _cmul_add(xre_ref[two, :].astype(BF16), xim_ref[two, :].astype(BF16),
                               pbre_ref[tab2, cols], pbim_ref[tab2, cols], s2_re, s2_im)
        hbre_ref[two, :] = t_re
        hbim_ref[two, :] = t_im
    last = seg - 1
    c_re, c_im = _cmul_add(xre_ref[rows(last), :], xim_ref[rows(last), :],
                           pwre_ref[tab(last), cols], pwim_ref[tab(last), cols], s_re, s_im)
    hre_ref[bi, :, cols] = c_re[SUBLANES - 1:SUBLANES, :]
    him_ref[bi, :, cols] = c_im[SUBLANES - 1:SUBLANES, :]


def _s5_mixer(u, h0_re, h0_im, wb, wc, pw_re, pw_im, d_skip, w_glu, b_glu, g_out, *, nb, t, tc):
    b, seq, d_ssm = u.shape
    n_state = h0_re.shape[-1]
    rows = nb * t
    assert tc // SUBLANES <= MAX_SEG and t % tc == 0 and seq % t == 0 and b % nb == 0
    state = pl.BlockSpec((nb, 1, n_state), lambda bi, j: (bi, 0, 0))
    act = pl.BlockSpec((nb, t, d_ssm), lambda bi, j: (bi, j, 0))
    ns = wb.shape[2] // 2
    pb_re = pw_re[0:MAX_SEG * SUBLANES].astype(BF16)
    pb_im = pw_im[0:MAX_SEG * SUBLANES].astype(BF16)
    return pl.pallas_call(
        functools.partial(_s5_kernel, tc=tc),
        out_shape=(jax.ShapeDtypeStruct((b, seq, d_ssm), BF16),
                   jax.ShapeDtypeStruct((b, 1, n_state), F32), jax.ShapeDtypeStruct((b, 1, n_state), F32)),
        grid=(b // nb, seq // t),
        in_specs=[act, state, state, _resident(wb.shape), _resident(wc.shape), _resident(pw_re.shape),
                  _resident(pw_im.shape), _resident(pb_re.shape), _resident(pb_im.shape),
                  _resident(d_skip.shape), _resident(w_glu.shape), _resident(b_glu.shape),
                  _resident(g_out.shape)],
        out_specs=(act, state, state),
        scratch_shapes=[pltpu.VMEM((2, rows, 2 * ns), F32), pltpu.VMEM((rows, ns), F32),
                        pltpu.VMEM((rows, ns), F32), pltpu.VMEM((rows, ns), BF16), pltpu.VMEM((rows, ns), BF16),
                        pltpu.VMEM((rows, d_ssm), F32)],
        compiler_params=_params("parallel", "arbitrary"),
        name="s5_mixer",
    )(u, h0_re, h0_im, wb, wc, pw_re, pw_im, pb_re, pb_im, d_skip, w_glu, b_glu, g_out)


def _out_proj_kernel(att_ref, ssm_ref, x_ref, gt1_ref, sh2_ref, sc2_ref, gatt_ref, wout_ref, g2_ref,
                     x1_ref, h2_ref):
    nb, t, d = x_ref.shape
    d_att, d_ssm = att_ref.shape[-1], ssm_ref.shape[-1]
    halves = [(slice(0, nb), slice(h * (t // 2), (h + 1) * (t // 2))) for h in range(2)] if nb == 1 else \
             [(slice(h * (nb // 2), (h + 1) * (nb // 2)), slice(0, t)) for h in range(2)]
    for bs, ts in halves:
        hb, ht = bs.stop - bs.start, ts.stop - ts.start
        rows = hb * ht
        a = att_ref[bs, ts, :].astype(F32).reshape(rows, d_att)
        an = _rms_rows(a, gatt_ref[...]).astype(BF16)
        mix = (jnp.dot(an, wout_ref[0:d_att, :], preferred_element_type=F32)
               + jnp.dot(ssm_ref[bs, ts, :].reshape(rows, d_ssm), wout_ref[d_att:, :],
                         preferred_element_type=F32))
        x1 = x_ref[bs, ts, :] + gt1_ref[bs, 0] * mix.reshape(hb, ht, d)
        x1_ref[bs, ts, :] = x1
        h2 = _rms_rows(x1, g2_ref[...]) * (1.0 + sc2_ref[bs, 0]) + sh2_ref[bs, 0]
        h2_ref[bs, ts, :] = h2.astype(BF16)


def _out_proj(att, ssm, x, mod4, mod_row, g_att, w_out, g2, *, nb, t):
    n_seq, seq, d = x.shape
    row_blk = mod_row // nb
    act = lambda width: pl.BlockSpec((nb, t, width), lambda i, j: (i, j, 0))
    mod = lambda m: pl.BlockSpec((nb, 1, 1, d), lambda i, j, m=m: (row_blk + i, m, 0, 0))
    return pl.pallas_call(
        _out_proj_kernel,
        out_shape=(jax.ShapeDtypeStruct(x.shape, F32), jax.ShapeDtypeStruct(x.shape, BF16)),
        grid=(n_seq // nb, seq // t),
        in_specs=[act(att.shape[-1]), act(ssm.shape[-1]), act(d), mod(2), mod(3), mod(4),
                  _resident(g_att.shape), _resident(w_out.shape), _resident(g2.shape)],
        out_specs=(act(d), act(d)),
        compiler_params=_params("parallel", "parallel"),
        name="out_proj",
    )(att, ssm, x, mod4, mod4, mod4, g_att, w_out, g2)


def _mlp_kernel(h2_ref, x1_ref, gt2_ref, w1_ref, w2_ref, o_ref, *, sub):
    nb, t, d = x1_ref.shape
    j = pl.program_id(2)
    tf = w1_ref.shape[1]

    @pl.when(j == 0)
    def _():
        o_ref[...] = jnp.zeros(o_ref.shape, F32)

    h2 = h2_ref[...].reshape(nb * t, d)
    for s in range(tf // sub):
        a = jnp.dot(h2, w1_ref[:, s * sub:(s + 1) * sub], preferred_element_type=F32)
        r = jnp.maximum(a, 0.0)
        o_ref[...] += jnp.dot((r * r).astype(BF16), w2_ref[s * sub:(s + 1) * sub, :],
                              preferred_element_type=F32).reshape(nb, t, d)

    @pl.when(j == pl.num_programs(2) - 1)
    def _():
        o_ref[...] = x1_ref[...] + gt2_ref[:, 0] * o_ref[...]


def _mlp(h2, x1, mod4, mod_row, w1, w2, *, nb, t, tf, sub):
    n_seq, seq, d = x1.shape
    d_ff = w1.shape[1]
    row_blk = mod_row // nb
    act = pl.BlockSpec((nb, t, d), lambda i, j, f: (i, j, 0))
    return pl.pallas_call(
        functools.partial(_mlp_kernel, sub=sub),
        out_shape=jax.ShapeDtypeStruct(x1.shape, F32),
        grid=(n_seq // nb, seq // t, d_ff // tf),
        in_specs=[act, act,
                  pl.BlockSpec((nb, 1, 1, d), lambda i, j, f: (row_blk + i, 5, 0, 0)),
                  pl.BlockSpec((d, tf), lambda i, j, f: (0, f)),
                  pl.BlockSpec((tf, d), lambda i, j, f: (f, 0))],
        out_specs=act,
        compiler_params=pltpu.CompilerParams(dimension_semantics=("parallel", "parallel", "arbitrary"),
                                             vmem_limit_bytes=MLP_VMEM_LIMIT),
        name="mlp",
    )(h2, x1, mod4, w1, w2)


def kernel(x_prompt, x_sample, c_prompt, c_sample, cache_k, cache_v, cache_logf, state_ssm_re, state_ssm_im,
           w_ada, b_ada, g_norm1, w_in, g_q, g_k, b_f, log_dt, a_re, a_im, b_re, b_im, c_re, c_im, d_skip,
           w_glu, b_glu, g_att_out, g_ssm_out, w_out, g_norm2, w_ff1, w_ff2):
    depth = w_ada.shape[0]
    assert depth == 1, "single-layer step"
    bsz, seq, d = x_prompt.shape
    dec_b, dec_s, _ = x_sample.shape
    past = cache_k.shape[2]
    n_heads = cache_k.shape[3]
    d_att = n_heads * HEAD_DIM
    n_groups, n_state = a_re.shape[1], a_re.shape[2]
    mod_rows = 16
    assert dec_b + bsz <= mod_rows
    l = 0

    c_rows = jnp.concatenate([c_sample, c_prompt, jnp.zeros((mod_rows - dec_b - bsz, d), F32)], axis=0)
    mod4 = _modulation(c_rows, w_ada[l], b_ada[l][None]).reshape(mod_rows, N_MOD, 1, d)
    row_sample, row_prompt = 0, dec_b

    w_qkv = w_in[l, :, 0:3 * d_att].astype(BF16)
    w_f = jnp.pad(w_in[l, :, 3 * d_att:3 * d_att + n_heads], ((0, 0), (0, LANES - n_heads))).astype(BF16)
    w_u = w_in[l, :, 3 * d_att + n_heads:].astype(BF16)
    b_f_pad = jnp.pad(b_f[l], (0, LANES - n_heads))[None]
    g1 = g_norm1[l][None]
    gq, gk = g_q[l][None], g_k[l][None]

    wb32, wc32, pw_re, pw_im = _s5_params(log_dt[l], a_re[l], a_im[l], b_re[l], b_im[l], c_re[l], c_im[l])
    wb, wc = wb32.astype(BF16), wc32.astype(BF16)
    pw_re = jnp.repeat(pw_re.reshape(N_POW, n_groups * n_state), SUBLANES, axis=0)
    pw_im = jnp.repeat(pw_im.reshape(N_POW, n_groups * n_state), SUBLANES, axis=0)
    d_row = d_skip[l][None]
    w_glu_b = w_glu[l].astype(BF16)
    b_glu_row = b_glu[l][None]
    g_ssm_row = g_ssm_out[l][None]
    g_att_row = g_att_out[l][None]
    w_out_b = w_out[l].astype(BF16)
    g2 = g_norm2[l][None]
    w1 = w_ff1[l].astype(BF16)
    w2 = w_ff2[l].astype(BF16)

    tm = ROW_TILE
    tc_prompt = SUBLANES * MAX_SEG
    tc_sample = dec_s

    q_p, k32_p, kbf_p, v32_p, vbf_p, logf_p, u_p = _in_proj(
        x_prompt, mod4, row_prompt, g1, w_qkv, w_u, w_f, gq, gk, b_f_pad,
        nb=1, t=tm, n_heads=n_heads, s5_chunk=tc_prompt)
    fcum_p, _ = _forget_cumsum(logf_p)
    att_p = _prompt_attention(q_p, kbf_p, vbf_p, fcum_p, n_heads=n_heads, tq=ATTN_TQ, tk=ATTN_TK,
                              heads_per_step=1)
    zeros_state = jnp.zeros((bsz, 1, n_groups * n_state), F32)
    ssm_p, hre_p, him_p = _s5_mixer(u_p, zeros_state, zeros_state, wb, wc, pw_re, pw_im, d_row, w_glu_b,
                                    b_glu_row, g_ssm_row, nb=1, t=tm, tc=tc_prompt)
    x1_p, h2_p = _out_proj(att_p, ssm_p, x_prompt, mod4, row_prompt, g_att_row, w_out_b, g2, nb=1, t=tm)
    y_p = _mlp(h2_p, x1_p, mod4, row_prompt, w1, w2, nb=1, t=tm, tf=MLP_TF, sub=MLP_SUB)

    q_s, k32_s, kbf_s, v32_s, vbf_s, logf_s, u_s = _in_proj(
        x_sample, mod4, row_sample, g1, w_qkv, w_u, w_f, gq, gk, b_f_pad,
        nb=dec_b, t=dec_s, n_heads=n_heads, s5_chunk=tc_sample)
    logf_all = jnp.concatenate([cache_logf[l], logf_s], axis=1)
    fcum_s, fhead_s = _forget_cumsum(logf_all)
    att_s = _sample_attention(q_s, kbf_s, vbf_s, cache_k[l].reshape(dec_b, past * n_heads, HEAD_DIM),
                              cache_v[l].reshape(dec_b, past * n_heads, HEAD_DIM), fcum_s, fhead_s,
                              n_heads=n_heads)
    ssm_s, hre_s, him_s = _s5_mixer(u_s, state_ssm_re[l].reshape(dec_b, 1, -1),
                                    state_ssm_im[l].reshape(dec_b, 1, -1), wb, wc, pw_re, pw_im, d_row,
                                    w_glu_b, b_glu_row, g_ssm_row, nb=dec_b, t=dec_s, tc=tc_sample)
    x1_s, h2_s = _out_proj(att_s, ssm_s, x_sample, mod4, row_sample, g_att_row, w_out_b, g2,
                           nb=dec_b, t=dec_s)
    y_s = _mlp(h2_s, x1_s, mod4, row_sample, w1, w2, nb=dec_b, t=dec_s, tf=MLP_TF, sub=MLP_SUB)

    states = lambda a, n: a.reshape(1, n, n_groups, n_state)
    heads = lambda a, n, s: a.reshape(1, n, s, n_heads, HEAD_DIM)
    return (y_p, y_s,
            heads(k32_p, bsz, seq), heads(v32_p, bsz, seq), logf_p[None],
            states(hre_p, bsz), states(him_p, bsz),
            heads(k32_s, dec_b, dec_s), heads(v32_s, dec_b, dec_s), logf_s[None],
            states(hre_s, dec_b), states(him_s, dec_b))
```

```python
import functools
import math

import jax
import jax.numpy as jnp
from jax import lax
from jax.experimental import pallas as pl
from jax.experimental.pallas import tpu as pltpu

F32 = jnp.float32
BF16 = jnp.bfloat16
HIGHEST = lax.Precision.HIGHEST

HEAD_DIM = 128
SSM_GROUP = 16
N_MOD = 6
EPS = 1e-6
LOG2E = 1.4426950408889634

LANES = 128
SUBLANES = 8
GROUPS_PER_BLOCK = 16
MAX_SEG = 32
N_POW = MAX_SEG + 2
VMEM_LIMIT = 56 * 1024 * 1024
MLP_VMEM_LIMIT = 58 * 1024 * 1024


def _params(*sem):
    return pltpu.CompilerParams(dimension_semantics=sem, vmem_limit_bytes=VMEM_LIMIT)


def _resident(shape):
    nd = len(shape)
    return pl.BlockSpec(shape, lambda *_: (0,) * nd, pipeline_mode=pl.Buffered(1))


def _whole(shape):
    nd = len(shape)
    return pl.BlockSpec(shape, lambda *_: (0,) * nd)


def _sigmoid(x):
    return 1.0 / (1.0 + jnp.exp(-x))


def _rms_rows(x, g):
    return x * lax.rsqrt(jnp.mean(x * x, axis=-1, keepdims=True) + EPS) * g


def _dot_nt(a, b):
    return lax.dot_general(a, b, (((1,), (1,)), ((), ())), preferred_element_type=F32)


def _cmul_add(x_re, x_im, a_re, a_im, h_re, h_im):
    return x_re + (a_re * h_re - a_im * h_im), x_im + (a_re * h_im + a_im * h_re)


def _stream_permutation(n, inverse):
    i = lax.broadcasted_iota(jnp.int32, (n, n), 0)
    j = lax.broadcasted_iota(jnp.int32, (n, n), 1)
    if inverse:
        i, j = j, i
    src = (i & (SUBLANES - 1)) * (n // SUBLANES) + (i >> 3)
    return jnp.where(j == src, 1.0, 0.0).astype(BF16)


def _split_bf16(x):
    hi = x.astype(BF16)
    return hi, (x - hi.astype(F32)).astype(BF16)


def _mod_kernel(c_ref, w_ref, b_ref, o_ref):
    c = c_ref[...]
    rows = c.shape[0]
    s_hi, s_lo = _split_bf16(c * _sigmoid(c))
    w_hi, w_lo = _split_bf16(w_ref[...])
    first = jnp.dot(jnp.concatenate([s_hi, s_lo], axis=0), w_hi, preferred_element_type=F32)
    second = jnp.dot(s_hi, w_lo, preferred_element_type=F32)
    o_ref[...] = first[0:rows] + (first[rows:2 * rows] + second) + b_ref[...]


def _modulation(c_rows, w_ada, b_ada):
    rows, d = c_rows.shape
    n = w_ada.shape[1]
    tn = 1024
    return pl.pallas_call(
        _mod_kernel,
        out_shape=jax.ShapeDtypeStruct((rows, n), F32),
        grid=(n // tn,),
        in_specs=[pl.BlockSpec((rows, d), lambda j: (0, 0)),
                  pl.BlockSpec((d, tn), lambda j: (0, j)),
                  pl.BlockSpec((1, tn), lambda j: (0, j))],
        out_specs=pl.BlockSpec((rows, tn), lambda j: (0, j)),
        compiler_params=_params("arbitrary"),
        name="modulation",
    )(c_rows, w_ada, b_ada)


def _zoh(log_dt, a_re, a_im):
    dt = jnp.exp(log_dt)
    lam_re = jnp.minimum(a_re, -1e-4)
    lam_im = a_im
    mag = jnp.exp(lam_re * dt)
    ang = lam_im * dt
    ab_re = mag * jnp.cos(ang)
    ab_im = mag * jnp.sin(ang)
    den = lam_re * lam_re + lam_im * lam_im
    z_re = ((ab_re - 1.0) * lam_re + ab_im * lam_im) / den
    z_im = (ab_im * lam_re - (ab_re - 1.0) * lam_im) / den
    return ab_re, ab_im, z_re, z_im


def _s5_param_kernel(logdt_ref, are_ref, aim_ref, logdt2_ref, are2_ref, aim2_ref, bre_ref, bim_ref, cre_ref,
                     cim_ref, wb_ref, wc_ref, pwre_ref, pwim_ref):
    n_groups, n_ch, n_st = bre_ref.shape
    gpb = GROUPS_PER_BLOCK
    ns = gpb * n_st
    _, _, z_re, z_im = _zoh(logdt_ref[...], are_ref[...], aim_ref[...])
    ab_re, ab_im, _, _ = _zoh(logdt2_ref[...], are2_ref[...], aim2_ref[...])
    b_re = bre_ref[...]
    b_im = bim_ref[...]
    zb_re = z_re * b_re - z_im * b_im
    zb_im = z_re * b_im + z_im * b_re
    wb_ref[...] = jnp.zeros(wb_ref.shape, F32)
    wc_ref[...] = jnp.zeros(wc_ref.shape, F32)
    for g in range(n_groups):
        blk, gi = divmod(g, gpb)
        chs = slice(gi * n_ch, (gi + 1) * n_ch)
        sts = slice(gi * n_st, (gi + 1) * n_st)
        sts_im = slice(ns + gi * n_st, ns + (gi + 1) * n_st)
        wb_ref[blk, chs, sts] = zb_re[g]
        wb_ref[blk, chs, sts_im] = zb_im[g]
        wc_ref[blk, sts, chs] = cre_ref[g]
        wc_ref[blk, sts_im, chs] = -cim_ref[g]
    p_re, p_im = ab_re, ab_im
    for k in range(MAX_SEG):
        pwre_ref[k] = p_re
        pwim_ref[k] = p_im
        if k + 1 < MAX_SEG:
            p_re, p_im = p_re * ab_re - p_im * ab_im, p_re * ab_im + p_im * ab_re
    for k in range(MAX_SEG, N_POW):
        p_re, p_im = p_re * p_re - p_im * p_im, 2.0 * (p_re * p_im)
        pwre_ref[k] = p_re
        pwim_ref[k] = p_im


def _s5_params(log_dt, a_re, a_im, b_re, b_im, c_re, c_im):
    g, n, c = b_re.shape
    n_blk = g // GROUPS_PER_BLOCK
    ch, ns = GROUPS_PER_BLOCK * c, GROUPS_PER_BLOCK * n
    f = lambda shape: jax.ShapeDtypeStruct(shape, F32)
    return pl.pallas_call(
        _s5_param_kernel,
        out_shape=(f((n_blk, ch, 2 * ns)), f((n_blk, 2 * ns, ch)), f((N_POW, g, n)), f((N_POW, g, n))),
        compiler_params=_params(),
        name="s5_params",
    )(log_dt.reshape(g, 1, 1), a_re.reshape(g, 1, n), a_im.reshape(g, 1, n), log_dt.reshape(g, 1), a_re, a_im,
      jnp.swapaxes(b_re, 1, 2), jnp.swapaxes(b_im, 1, 2), jnp.swapaxes(c_re, 1, 2), jnp.swapaxes(c_im, 1, 2))


def _pow_row(seg_rows):
    if seg_rows <= MAX_SEG:
        return seg_rows - 1
    return MAX_SEG - 1 + int(math.log2(seg_rows // MAX_SEG))


def _in_proj_kernel(x_ref, sh_ref, sc_ref, g1_ref, wqkv_ref, wu_ref, wf_ref, gq_ref, gk_ref, bf_ref,
                    q_ref, k32_ref, kbf_ref, v32_ref, vbf_ref, logf_ref, u_ref,
                    *, n_heads, q_scale, s5_chunk):
    nb, t, d = x_ref.shape
    rows = nb * t
    d_att = n_heads * HEAD_DIM
    x = x_ref[...]
    y = _rms_rows(x, g1_ref[...])
    h = (y * (1.0 + sc_ref[:, 0]) + sh_ref[:, 0]).reshape(rows, d)
    hb = h.astype(BF16)

    q = jnp.dot(hb, wqkv_ref[:, 0:d_att], preferred_element_type=F32)
    for hh in range(n_heads):
        sl = slice(hh * HEAD_DIM, (hh + 1) * HEAD_DIM)
        qn = _rms_rows(q[:, sl], gq_ref[...]) * q_scale
        q_ref[:, :, sl] = qn.reshape(nb, t, HEAD_DIM).astype(BF16)

    k = jnp.dot(hb, wqkv_ref[:, d_att:2 * d_att], preferred_element_type=F32)
    for hh in range(n_heads):
        sl = slice(hh * HEAD_DIM, (hh + 1) * HEAD_DIM)
        kn = _rms_rows(k[:, sl], gk_ref[...])
        kbf_ref[:, :, sl] = kn.reshape(nb, t, HEAD_DIM).astype(BF16)
        for bi in range(nb):
            k32_ref[bi, pl.ds(hh, t, stride=n_heads), :] = kn[bi * t:(bi + 1) * t]

    v = jnp.dot(hb, wqkv_ref[:, 2 * d_att:3 * d_att], preferred_element_type=F32)
    for hh in range(n_heads):
        sl = slice(hh * HEAD_DIM, (hh + 1) * HEAD_DIM)
        for bi in range(nb):
            v32_ref[bi, pl.ds(hh, t, stride=n_heads), :] = v[bi * t:(bi + 1) * t, sl]
    vbf_ref[...] = v.reshape(nb, t, d_att).astype(BF16)

    f = jnp.dot(hb, wf_ref[...], preferred_element_type=F32) + bf_ref[...]
    logf = jnp.minimum(f, 0.0) - jnp.log1p(jnp.exp(-jnp.abs(f)))
    logf_ref[...] = logf[:, 0:n_heads].reshape(nb, t, n_heads)

    u = jnp.dot(hb, wu_ref[...], preferred_element_type=F32).astype(BF16)
    perm = _stream_permutation(s5_chunk, inverse=False)
    for c in range(rows // s5_chunk):
        uc = jnp.dot(perm, u[c * s5_chunk:(c + 1) * s5_chunk], preferred_element_type=F32).astype(BF16)
        if nb == 1:
            u_ref[0, c * s5_chunk:(c + 1) * s5_chunk, :] = uc
        else:
            per = t // s5_chunk
            u_ref[c // per, (c % per) * s5_chunk:(c % per + 1) * s5_chunk, :] = uc


def _in_proj(x, mod4, mod_row, g1, w_qkv, w_u, w_f, g_q, g_k, b_f_pad, *, nb, t, n_heads, s5_chunk):
    n_seq, seq, d = x.shape
    d_att = n_heads * HEAD_DIM
    d_ssm = w_u.shape[1]
    assert t % s5_chunk == 0
    grid = (n_seq // nb, seq // t)
    row_blk = mod_row // nb
    act = lambda width: pl.BlockSpec((nb, t, width), lambda i, j: (i, j, 0))
    act4 = pl.BlockSpec((nb, t * n_heads, HEAD_DIM), lambda i, j: (i, j, 0))
    mod = lambda m: pl.BlockSpec((nb, 1, 1, d), lambda i, j, m=m: (row_blk + i, m, 0, 0))
    sds = lambda width, dt: jax.ShapeDtypeStruct((n_seq, seq, width), dt)
    sds4 = jax.ShapeDtypeStruct((n_seq, seq * n_heads, HEAD_DIM), F32)
    kern = functools.partial(_in_proj_kernel, n_heads=n_heads, q_scale=HEAD_DIM ** -0.5 * LOG2E,
                             s5_chunk=s5_chunk)
    return pl.pallas_call(
        kern,
        out_shape=(sds(d_att, BF16), sds4, sds(d_att, BF16), sds4, sds(d_att, BF16),
                   sds(n_heads, F32), sds(d_ssm, BF16)),
        grid=grid,
        in_specs=[act(d), mod(0), mod(1), _resident(g1.shape), _resident(w_qkv.shape), _resident(w_u.shape),
                  _resident(w_f.shape), _resident(g_q.shape), _resident(g_k.shape), _resident(b_f_pad.shape)],
        out_specs=(act(d_att), act4, act(d_att), act4, act(d_att), act(n_heads), act(d_ssm)),
        compiler_params=_params("parallel", "parallel"),
        name="in_proj",
    )(x, mod4, mod4, g1, w_qkv, w_u, w_f, g_q, g_k, b_f_pad)


def _cumsum_kernel(x_ref, o_ref, *, n_chunks):
    x = x_ref[...]
    n = x.shape[0]
    li = lax.broadcasted_iota(jnp.int32, (LANES, LANES), 0)
    lj = lax.broadcasted_iota(jnp.int32, (LANES, LANES), 1)
    tri = jnp.where(li <= lj, 1.0, 0.0)
    within = jnp.dot(x, tri, precision=HIGHEST, preferred_element_type=F32)
    tot = jnp.dot(x, jnp.ones((LANES, LANES), F32), precision=HIGHEST, preferred_element_type=F32)
    r = lax.broadcasted_iota(jnp.int32, (n, n), 0)
    c = lax.broadcasted_iota(jnp.int32, (n, n), 1)
    seq_of = lambda i: jnp.floor((i.astype(F32) + 0.5) * (1.0 / n_chunks))
    earlier = jnp.where(seq_of(r) == seq_of(c), jnp.where(c < r, 1.0, 0.0), 0.0)
    before = jnp.dot(earlier, tot, precision=HIGHEST, preferred_element_type=F32)
    o_ref[...] = (within + before) * LOG2E


def _forget_cumsum(logf):
    b, length, h = logf.shape
    n_chunks = -(-length // LANES)
    rows = jnp.swapaxes(logf, 1, 2).reshape(b * h, length)
    rows = jnp.pad(rows, ((0, 0), (0, n_chunks * LANES - length)))
    out = pl.pallas_call(
        functools.partial(_cumsum_kernel, n_chunks=n_chunks),
        out_shape=jax.ShapeDtypeStruct((b * h * n_chunks, LANES), F32),
        compiler_params=_params(),
        name="forget_cumsum",
    )(rows.reshape(b * h * n_chunks, LANES))
    by_head = out.reshape(b, h, n_chunks * LANES)
    return jnp.swapaxes(by_head[:, :, :length], 1, 2), by_head


def _head_column(f_rows, head):
    lane_h = lax.broadcasted_iota(jnp.int32, f_rows.shape, 1)
    return jnp.sum(jnp.where(lane_h == head, f_rows, 0.0), axis=-1, keepdims=True)


def _bias_columns(f_rows, head, key_side):
    col = _head_column(f_rows, head)
    hi = col.astype(BF16).astype(F32)
    rest = col - hi
    mid = rest.astype(BF16).astype(F32)
    lo = rest - mid
    lane = lax.broadcasted_iota(jnp.int32, (f_rows.shape[0], LANES), 1)
    if key_side:
        split = jnp.where(lane == 3, -hi, jnp.where(lane == 4, -mid, jnp.where(lane == 5, -lo, 0.0)))
        vals = jnp.where(lane < 3, 1.0, split)
    else:
        split = jnp.where(lane == 0, hi, jnp.where(lane == 1, mid, jnp.where(lane == 2, lo, 0.0)))
        vals = jnp.where((lane >= 3) & (lane < 6), 1.0, split)
    return vals.astype(BF16)


def _attn_kernel(q_ref, qn_ref, k_ref, v_ref, f_ref, o_ref, kaug_ref, vaug_ref, qa_ref, qnext_ref, s0_ref, s1_ref,
                 m_ref, acc_ref, *, tq, tk, build_rows, heads_per_step):
    qi = pl.program_id(2)
    t = k_ref.shape[1]
    dh = HEAD_DIM
    hps = heads_per_step
    head0 = pl.program_id(1) * hps

    def first_scores(src_ref, row0, dst_ref):
        f_rows = f_ref[0, pl.ds(row0, tq), :]
        for g in range(hps):
            qa = jnp.concatenate([src_ref[0, :, g * dh:(g + 1) * dh], _bias_columns(f_rows, head0 + g, False)],
                                 axis=-1)
            dst_ref[g] = qa
            s0_ref[g] = _dot_nt(qa, kaug_ref[g, 0:tk, :])

    def scores(c, s_ref, r0=0):
        off = pl.multiple_of(c * tk, tk)
        for g in range(hps):
            s_ref[g, r0:, :] = _dot_nt(qa_ref[g, r0:, :], kaug_ref[g, pl.ds(off, tk), :])

    @pl.when(qi == 0)
    def _first_block():
        for g in range(hps):
            kaug_ref[g, :, 0:dh] = k_ref[0, :, g * dh:(g + 1) * dh]
            vaug_ref[g, :, 0:dh] = v_ref[0, :, g * dh:(g + 1) * dh]
            vaug_ref[g, :, dh:2 * dh] = jnp.ones((t, dh), BF16)

        def piece(i, carry):
            r0 = pl.multiple_of(i * build_rows, build_rows)
            f_rows = f_ref[0, pl.ds(r0, build_rows), :]
            for g in range(hps):
                kaug_ref[g, pl.ds(r0, build_rows), dh:2 * dh] = _bias_columns(f_rows, head0 + g, True)
            return carry

        lax.fori_loop(0, t // build_rows, piece, 0)
        first_scores(q_ref, 0, qa_ref)

    @pl.when(qi > 0)
    def _take_prepared_queries():
        qa_ref[...] = qnext_ref[...]

    q0 = pl.multiple_of(qi * tq, tq)
    m_ref[...] = jnp.full(m_ref.shape, -jnp.inf, F32)
    acc_ref[...] = jnp.zeros(acc_ref.shape, F32)

    def absorb(c, s_ref, masked, r0=0):
        off = pl.multiple_of(c * tk, tk)
        for g in range(hps):
            s = s_ref[g, r0:, :]
            if masked:
                ahead = (lax.broadcasted_iota(jnp.int32, s.shape, 1)
                         - lax.broadcasted_iota(jnp.int32, s.shape, 0))
                s = jnp.where(ahead <= q0 + r0 - off, s, -jnp.inf)
            m_old = m_ref[g, r0:, :]
            m_new = jnp.maximum(m_old, jnp.max(s, axis=-1, keepdims=True))
            alpha = jnp.exp2(m_old - m_new)
            p = jnp.exp2(s - jnp.tile(m_new, (1, tk // LANES)))
            pv = jnp.dot(p.astype(BF16), vaug_ref[g, pl.ds(off, tk), :], preferred_element_type=F32)
            acc_ref[g, r0:, :] = jnp.tile(alpha, (1, 2)) * acc_ref[g, r0:, :] + pv
            m_ref[g, r0:, :] = m_new

    n_full = 2 * qi

    def pair(c):
        scores(c + 1, s1_ref)
        absorb(c, s0_ref, False)
        scores(c + 2, s0_ref)
        absorb(c + 1, s1_ref, False)

    def two_pairs(p, carry):
        pair(4 * p)
        pair(4 * p + 2)
        return carry

    lax.fori_loop(0, qi // 2, two_pairs, 0)

    @pl.when(qi % 2 == 1)
    def _odd_pair():
        pair(n_full - 2)

    scores(n_full + 1, s1_ref, r0=tk)
    absorb(n_full, s0_ref, True)
    first_scores(qn_ref, pl.multiple_of(jnp.minimum(qi + 1, pl.num_programs(2) - 1) * tq, tq), qnext_ref)
    absorb(n_full + 1, s1_ref, True, r0=tk)

    for g in range(hps):
        acc = acc_ref[g]
        o_ref[0, :, g * dh:(g + 1) * dh] = (acc[:, 0:dh] * (1.0 / acc[:, dh:2 * dh])).astype(o_ref.dtype)


def _prompt_attention(q, k, v, f_cum2, *, n_heads, tq, tk, heads_per_step):
    b, t, _ = q.shape
    assert t % tq == 0 and tq == 2 * tk and n_heads % heads_per_step == 0
    hps = heads_per_step
    width = hps * HEAD_DIM
    nq = t // tq
    q_blk = pl.BlockSpec((1, tq, width), lambda bi, h, i: (bi, i, h))
    qn_blk = pl.BlockSpec((1, tq, width), lambda bi, h, i: (bi, jnp.minimum(i + 1, nq - 1), h))
    kv_blk = pl.BlockSpec((1, t, width), lambda bi, h, i: (bi, 0, h))
    f_blk = pl.BlockSpec((1, t, n_heads), lambda bi, h, i: (bi, 0, 0))
    return pl.pallas_call(
        functools.partial(_attn_kernel, tq=tq, tk=tk, build_rows=1024, heads_per_step=hps),
        out_shape=jax.ShapeDtypeStruct(q.shape, BF16),
        grid=(b, n_heads // hps, nq),
        in_specs=[q_blk, qn_blk, kv_blk, kv_blk, f_blk],
        out_specs=q_blk,
        scratch_shapes=[pltpu.VMEM((hps, t, 2 * HEAD_DIM), BF16), pltpu.VMEM((hps, t, 2 * HEAD_DIM), BF16),
                        pltpu.VMEM((hps, tq, 2 * HEAD_DIM), BF16), pltpu.VMEM((hps, tq, 2 * HEAD_DIM), BF16),
                        pltpu.VMEM((hps, tq, tk), F32), pltpu.VMEM((hps, tq, tk), F32),
                        pltpu.VMEM((hps, tq, LANES), F32), pltpu.VMEM((hps, tq, 2 * HEAD_DIM), F32)],
        compiler_params=_params("parallel", "arbitrary", "arbitrary"),
        name="prompt_attention",
    )(q, q, k, v, f_cum2)


def _sample_attn_kernel(q_ref, kc_ref, vc_ref, kn_ref, vn_ref, fq_ref, fk_ref, o_ref, *, past, n_heads):
    s_len = q_ref.shape[1]
    f_new = fq_ref[0]
    row = lax.broadcasted_iota(jnp.int32, (s_len, s_len), 0)
    col = lax.broadcasted_iota(jnp.int32, (s_len, s_len), 1)
    for hh in range(n_heads):
        sl = slice(hh * HEAD_DIM, (hh + 1) * HEAD_DIM)
        q = q_ref[0, :, sl]
        fq = _head_column(f_new, hh)
        fk = fk_ref[0, hh:hh + 1, :]
        head_rows = pl.ds(hh, past, stride=n_heads)
        s_c = _dot_nt(q, kc_ref[0, head_rows, :].astype(BF16)) + fq - fk[:, 0:past]
        s_n = jnp.where(col <= row, _dot_nt(q, kn_ref[0, :, sl]) + fq - fk[:, past:past + s_len], -jnp.inf)
        m = jnp.maximum(jnp.max(s_c, axis=-1, keepdims=True), jnp.max(s_n, axis=-1, keepdims=True))
        p_c = jnp.exp2(s_c - m)
        p_n = jnp.exp2(s_n - m)
        l = jnp.sum(p_c, axis=-1, keepdims=True) + jnp.sum(p_n, axis=-1, keepdims=True)
        o = (jnp.dot(p_c.astype(BF16), vc_ref[0, head_rows, :].astype(BF16), preferred_element_type=F32)
             + jnp.dot(p_n.astype(BF16), vn_ref[0, :, sl], preferred_element_type=F32))
        o_ref[0, :, sl] = (o * (1.0 / l)).astype(o_ref.dtype)


def _sample_attention(q, k_new, v_new, cache_k, cache_v, f_cum2, f_by_head, *, n_heads):
    b, s_len, d_att = q.shape
    past = cache_k.shape[1] // n_heads
    assert past % s_len == 0
    new = pl.BlockSpec((1, s_len, d_att), lambda bi: (bi, 0, 0))
    old = pl.BlockSpec((1, past * n_heads, HEAD_DIM), lambda bi: (bi, 0, 0))
    return pl.pallas_call(
        functools.partial(_sample_attn_kernel, past=past, n_heads=n_heads),
        out_shape=jax.ShapeDtypeStruct(q.shape, BF16),
        grid=(b,),
        in_specs=[new, old, old, new, new,
                  pl.BlockSpec((1, s_len, n_heads), lambda bi: (bi, past // s_len, 0)),
                  pl.BlockSpec((1, n_heads, f_by_head.shape[-1]), lambda bi: (bi, 0, 0))],
        out_specs=new,
        compiler_params=_params("parallel"),
        name="sample_attention",
    )(q, cache_k, cache_v, k_new, v_new, f_cum2, f_by_head)


def _s5_kernel(u_ref, h0re_ref, h0im_ref, wb_ref, wc_ref, pwre_ref, pwim_ref, pbre_ref, pbim_ref, d_ref,
               wglu_ref, bglu_ref, gout_ref, o_ref, hre_ref, him_ref, bu_ref, xre_ref, xim_ref, hbre_ref, hbim_ref,
               y_ref, *, tc):
    n_rows = u_ref.shape[1]
    n_chunks = n_rows // tc
    seg = tc // SUBLANES
    n_blk, ch = wb_ref.shape[0], wb_ref.shape[1]
    ns = wb_ref.shape[2] // 2

    @pl.when(pl.program_id(1) == 0)
    def _():
        hre_ref[...] = h0re_ref[...]
        him_ref[...] = h0im_ref[...]

    def b_proj(blk):
        bu_ref[blk % 2] = jnp.dot(u_ref[0, :, blk * ch:(blk + 1) * ch], wb_ref[blk], preferred_element_type=F32)

    sub = lax.broadcasted_iota(jnp.int32, (SUBLANES, ns), 0)
    tab = lambda r: slice(r * SUBLANES, (r + 1) * SUBLANES)
    b_proj(0)
    for blk in range(n_blk):
        if blk + 1 < n_blk:
            b_proj(blk + 1)
        buf = blk % 2
        cols = slice(blk * ns, (blk + 1) * ns)
        chs = slice(blk * ch, (blk + 1) * ch)
        for c in range(n_chunks):
            _s5_scan_chunk(c, tc, seg, ns, buf, cols, sub, tab, bu_ref, xre_ref, xim_ref, hbre_ref, hbim_ref,
                           pwre_ref, pwim_ref, pbre_ref, pbim_ref, hre_ref, him_ref)
        y = (jnp.dot(hbre_ref[...], wc_ref[blk, 0:ns, :], preferred_element_type=F32)
             + jnp.dot(hbim_ref[...], wc_ref[blk, ns:2 * ns, :], preferred_element_type=F32))
        y_ref[:, chs] = y + d_ref[:, chs] * u_ref[0, :, chs].astype(F32)

    y = y_ref[...]
    g = y * (0.5 * (1.0 + jnp.tanh(math.sqrt(2.0 / math.pi) * (y + 0.044715 * (y * y * y)))))
    gate = _sigmoid(jnp.dot(g.astype(BF16), wglu_ref[...], preferred_element_type=F32) + bglu_ref[...])
    out = _rms_rows(g * gate, gout_ref[...]).astype(BF16)
    unperm = _stream_permutation(tc, inverse=True)
    for c in range(n_chunks):
        o_ref[0, c * tc:(c + 1) * tc, :] = jnp.dot(unperm, out[c * tc:(c + 1) * tc],
                                                    preferred_element_type=F32).astype(o_ref.dtype)


def _s5_scan_chunk(c, tc, seg, ns, buf, cols, sub, tab, bu_ref, xre_ref, xim_ref, hbre_ref, hbim_ref,
                   pwre_ref, pwim_ref, pbre_ref, pbim_ref, hre_ref, him_ref):
    rows = lambda r: slice(c * tc + r * SUBLANES, c * tc + (r + 1) * SUBLANES)

    a_re, a_im = pwre_ref[tab(0), cols], pwim_ref[tab(0), cols]
    h_re = bu_ref[buf, rows(0), 0:ns]
    h_im = bu_ref[buf, rows(0), ns:2 * ns]
    xre_ref[rows(0), :] = h_re
    xim_ref[rows(0), :] = h_im
    for r in range(1, seg):
        h_re, h_im = _cmul_add(bu_ref[buf, rows(r), 0:ns], bu_ref[buf, rows(r), ns:2 * ns], a_re, a_im, h_re, h_im)
        xre_ref[rows(r), :] = h_re
        xim_ref[rows(r), :] = h_im

    s_re = jnp.where(sub == 0, hre_ref[0, :, cols], pltpu.roll(h_re, 1, 0))
    s_im = jnp.where(sub == 0, him_ref[0, :, cols], pltpu.roll(h_im, 1, 0))
    for shift in (1, 2, 4):
        row = _pow_row(seg * shift)
        m_re = jnp.where(sub >= shift, pwre_ref[tab(row), cols], 0.0)
        m_im = jnp.where(sub >= shift, pwim_ref[tab(row), cols], 0.0)
        s_re, s_im = _cmul_add(s_re, s_im, m_re, m_im, pltpu.roll(s_re, shift, 0), pltpu.roll(s_im, shift, 0))

    s2_re = jnp.concatenate([s_re, s_re], axis=0).astype(BF16)
    s2_im = jnp.concatenate([s_im, s_im], axis=0).astype(BF16)
    for j in range(seg // 2):
        two = slice(c * tc + 2 * j * SUBLANES, c * tc + (2 * j + 2) * SUBLANES)
        tab2 = slice(2 * j * SUBLANES, (2 * j + 2) * SUBLANES)
        t_re, t_im = _cmul_add(xre_ref[two, :].astype(BF16), xim_ref[two, :].astype(BF16),
                               pbre_ref[tab2, cols], pbim_ref[tab2, cols], s2_re, s2_im)
        hbre_ref[two, :] = t_re
        hbim_ref[two, :] = t_im
    last = seg - 1
    c_re, c_im = _cmul_add(xre_ref[rows(last), :], xim_ref[rows(last), :],
                           pwre_ref[tab(last), cols], pwim_ref[tab(last), cols], s_re, s_im)
    hre_ref[0, :, cols] = c_re[SUBLANES - 1:SUBLANES, :]
    him_ref[0, :, cols] = c_im[SUBLANES - 1:SUBLANES, :]


def _s5_mixer(u, h0_re, h0_im, wb, wc, pw_re, pw_im, d_skip, w_glu, b_glu, g_out, *, tc, chunks_per_step):
    b, t, d_ssm = u.shape
    n_state = h0_re.shape[-1]
    rows = tc * chunks_per_step
    assert tc // SUBLANES <= MAX_SEG and t % rows == 0
    state = pl.BlockSpec((1, 1, n_state), lambda bi, j: (bi, 0, 0))
    act = pl.BlockSpec((1, rows, d_ssm), lambda bi, j: (bi, j, 0))
    ns = wb.shape[2] // 2
    pb_re = pw_re[0:MAX_SEG * SUBLANES].astype(BF16)
    pb_im = pw_im[0:MAX_SEG * SUBLANES].astype(BF16)
    return pl.pallas_call(
        functools.partial(_s5_kernel, tc=tc),
        out_shape=(jax.ShapeDtypeStruct((b, t, d_ssm), BF16),
                   jax.ShapeDtypeStruct((b, 1, n_state), F32), jax.ShapeDtypeStruct((b, 1, n_state), F32)),
        grid=(b, t // rows),
        in_specs=[act, state, state, _resident(wb.shape), _resident(wc.shape), _resident(pw_re.shape),
                  _resident(pw_im.shape), _resident(pb_re.shape), _resident(pb_im.shape),
                  _resident(d_skip.shape), _resident(w_glu.shape), _resident(b_glu.shape),
                  _resident(g_out.shape)],
        out_specs=(act, state, state),
        scratch_shapes=[pltpu.VMEM((2, rows, 2 * ns), F32), pltpu.VMEM((rows, ns), F32),
                        pltpu.VMEM((rows, ns), F32), pltpu.VMEM((rows, ns), BF16), pltpu.VMEM((rows, ns), BF16),
                        pltpu.VMEM((rows, d_ssm), F32)],
        compiler_params=_params("parallel", "arbitrary"),
        name="s5_mixer",
    )(u, h0_re, h0_im, wb, wc, pw_re, pw_im, pb_re, pb_im, d_skip, w_glu, b_glu, g_out)


def _out_proj_kernel(att_ref, ssm_ref, x_ref, gt1_ref, sh2_ref, sc2_ref, gatt_ref, wout_ref, g2_ref,
                     x1_ref, h2_ref):
    nb, t, d = x_ref.shape
    d_att, d_ssm = att_ref.shape[-1], ssm_ref.shape[-1]
    halves = [(slice(0, nb), slice(h * (t // 2), (h + 1) * (t // 2))) for h in range(2)] if nb == 1 else \
             [(slice(h * (nb // 2), (h + 1) * (nb // 2)), slice(0, t)) for h in range(2)]
    for bs, ts in halves:
        hb, ht = bs.stop - bs.start, ts.stop - ts.start
        rows = hb * ht
        a = att_ref[bs, ts, :].astype(F32).reshape(rows, d_att)
        an = _rms_rows(a, gatt_ref[...]).astype(BF16)
        mix = (jnp.dot(an, wout_ref[0:d_att, :], preferred_element_type=F32)
               + jnp.dot(ssm_ref[bs, ts, :].reshape(rows, d_ssm), wout_ref[d_att:, :],
                         preferred_element_type=F32))
        x1 = x_ref[bs, ts, :] + gt1_ref[bs, 0] * mix.reshape(hb, ht, d)
        x1_ref[bs, ts, :] = x1
        h2 = _rms_rows(x1, g2_ref[...]) * (1.0 + sc2_ref[bs, 0]) + sh2_ref[bs, 0]
        h2_ref[bs, ts, :] = h2.astype(BF16)


def _out_proj(att, ssm, x, mod4, mod_row, g_att, w_out, g2, *, nb, t):
    n_seq, seq, d = x.shape
    row_blk = mod_row // nb
    act = lambda width: pl.BlockSpec((nb, t, width), lambda i, j: (i, j, 0))
    mod = lambda m: pl.BlockSpec((nb, 1, 1, d), lambda i, j, m=m: (row_blk + i, m, 0, 0))
    return pl.pallas_call(
        _out_proj_kernel,
        out_shape=(jax.ShapeDtypeStruct(x.shape, F32), jax.ShapeDtypeStruct(x.shape, BF16)),
        grid=(n_seq // nb, seq // t),
        in_specs=[act(att.shape[-1]), act(ssm.shape[-1]), act(d), mod(2), mod(3), mod(4),
                  _whole(g_att.shape), _whole(w_out.shape), _whole(g2.shape)],
        out_specs=(act(d), act(d)),
        compiler_params=_params("parallel", "parallel"),
        name="out_proj",
    )(att, ssm, x, mod4, mod4, mod4, g_att, w_out, g2)


def _mlp_kernel(h2_ref, x1_ref, gt2_ref, w1_ref, w2_ref, o_ref, *, sub):
    nb, t, d = x1_ref.shape
    j = pl.program_id(2)
    tf = w1_ref.shape[1]

    @pl.when(j == 0)
    def _():
        o_ref[...] = jnp.zeros(o_ref.shape, F32)

    h2 = h2_ref[...].reshape(nb * t, d)
    for s in range(tf // sub):
        a = jnp.dot(h2, w1_ref[:, s * sub:(s + 1) * sub], preferred_element_type=F32)
        r = jnp.maximum(a, 0.0)
        o_ref[...] += jnp.dot((r * r).astype(BF16), w2_ref[s * sub:(s + 1) * sub, :],
                              preferred_element_type=F32).reshape(nb, t, d)

    @pl.when(j == pl.num_programs(2) - 1)
    def _():
        o_ref[...] = x1_ref[...] + gt2_ref[:, 0] * o_ref[...]


def _mlp(h2, x1, mod4, mod_row, w1, w2, *, nb, t, tf, sub):
    n_seq, seq, d = x1.shape
    d_ff = w1.shape[1]
    row_blk = mod_row // nb
    act = pl.BlockSpec((nb, t, d), lambda i, j, f: (i, j, 0))
    return pl.pallas_call(
        functools.partial(_mlp_kernel, sub=sub),
        out_shape=jax.ShapeDtypeStruct(x1.shape, F32),
        grid=(n_seq // nb, seq // t, d_ff // tf),
        in_specs=[act, act,
                  pl.BlockSpec((nb, 1, 1, d), lambda i, j, f: (row_blk + i, 5, 0, 0)),
                  pl.BlockSpec((d, tf), lambda i, j, f: (0, f)),
                  pl.BlockSpec((tf, d), lambda i, j, f: (f, 0))],
        out_specs=act,
        compiler_params=pltpu.CompilerParams(dimension_semantics=("parallel", "parallel", "arbitrary"),
                                             vmem_limit_bytes=MLP_VMEM_LIMIT),
        name="mlp",
    )(h2, x1, mod4, w1, w2)


def kernel(x_prompt, x_sample, c_prompt, c_sample, cache_k, cache_v, cache_logf, state_ssm_re, state_ssm_im,
           w_ada, b_ada, g_norm1, w_in, g_q, g_k, b_f, log_dt, a_re, a_im, b_re, b_im, c_re, c_im, d_skip,
           w_glu, b_glu, g_att_out, g_ssm_out, w_out, g_norm2, w_ff1, w_ff2):
    depth = w_ada.shape[0]
    assert depth == 1, "single-layer step"
    bsz, seq, d = x_prompt.shape
    dec_b, dec_s, _ = x_sample.shape
    past = cache_k.shape[2]
    n_heads = cache_k.shape[3]
    d_att = n_heads * HEAD_DIM
    n_groups, n_state = a_re.shape[1], a_re.shape[2]
    mod_rows = 16
    assert dec_b + bsz <= mod_rows
    l = 0

    c_rows = jnp.concatenate([c_sample, c_prompt, jnp.zeros((mod_rows - dec_b - bsz, d), F32)], axis=0)
    mod4 = _modulation(c_rows, w_ada[l], b_ada[l][None]).reshape(mod_rows, N_MOD, 1, d)
    row_sample, row_prompt = 0, dec_b

    w_qkv = w_in[l, :, 0:3 * d_att].astype(BF16)
    w_f = jnp.pad(w_in[l, :, 3 * d_att:3 * d_att + n_heads], ((0, 0), (0, LANES - n_heads))).astype(BF16)
    w_u = w_in[l, :, 3 * d_att + n_heads:].astype(BF16)
    b_f_pad = jnp.pad(b_f[l], (0, LANES - n_heads))[None]
    g1 = g_norm1[l][None]
    gq, gk = g_q[l][None], g_k[l][None]

    wb32, wc32, pw_re, pw_im = _s5_params(log_dt[l], a_re[l], a_im[l], b_re[l], b_im[l], c_re[l], c_im[l])
    wb, wc = wb32.astype(BF16), wc32.astype(BF16)
    pw_re = jnp.repeat(pw_re.reshape(N_POW, n_groups * n_state), SUBLANES, axis=0)
    pw_im = jnp.repeat(pw_im.reshape(N_POW, n_groups * n_state), SUBLANES, axis=0)
    d_row = d_skip[l][None]
    w_glu_b = w_glu[l].astype(BF16)
    b_glu_row = b_glu[l][None]
    g_ssm_row = g_ssm_out[l][None]
    g_att_row = g_att_out[l][None]
    w_out_b = w_out[l].astype(BF16)
    g2 = g_norm2[l][None]
    w1 = w_ff1[l].astype(BF16)
    w2 = w_ff2[l].astype(BF16)

    tm = 512
    tc_prompt = SUBLANES * MAX_SEG
    tc_sample = dec_s

    q_p, k32_p, kbf_p, v32_p, vbf_p, logf_p, u_p = _in_proj(
        x_prompt, mod4, row_prompt, g1, w_qkv, w_u, w_f, gq, gk, b_f_pad,
        nb=1, t=tm, n_heads=n_heads, s5_chunk=tc_prompt)
    fcum_p, _ = _forget_cumsum(logf_p)
    att_p = _prompt_attention(q_p, kbf_p, vbf_p, fcum_p, n_heads=n_heads, tq=1024, tk=512, heads_per_step=1)
    zeros_state = jnp.zeros((bsz, 1, n_groups * n_state), F32)
    ssm_p, hre_p, him_p = _s5_mixer(u_p, zeros_state, zeros_state, wb, wc, pw_re, pw_im, d_row, w_glu_b,
                                    b_glu_row, g_ssm_row, tc=tc_prompt, chunks_per_step=tm // tc_prompt)
    x1_p, h2_p = _out_proj(att_p, ssm_p, x_prompt, mod4, row_prompt, g_att_row, w_out_b, g2, nb=1, t=tm)
    y_p = _mlp(h2_p, x1_p, mod4, row_prompt, w1, w2, nb=1, t=tm, tf=2048, sub=1024)

    q_s, k32_s, kbf_s, v32_s, vbf_s, logf_s, u_s = _in_proj(
        x_sample, mod4, row_sample, g1, w_qkv, w_u, w_f, gq, gk, b_f_pad,
        nb=dec_b, t=dec_s, n_heads=n_heads, s5_chunk=tc_sample)
    logf_all = jnp.concatenate([cache_logf[l], logf_s], axis=1)
    fcum_s, fhead_s = _forget_cumsum(logf_all)
    att_s = _sample_attention(q_s, kbf_s, vbf_s, cache_k[l].reshape(dec_b, past * n_heads, HEAD_DIM),
                              cache_v[l].reshape(dec_b, past * n_heads, HEAD_DIM), fcum_s, fhead_s,
                              n_heads=n_heads)
    ssm_s, hre_s, him_s = _s5_mixer(u_s, state_ssm_re[l].reshape(dec_b, 1, -1),
                                    state_ssm_im[l].reshape(dec_b, 1, -1), wb, wc, pw_re, pw_im, d_row,
                                    w_glu_b, b_glu_row, g_ssm_row, tc=tc_sample, chunks_per_step=1)
    x1_s, h2_s = _out_proj(att_s, ssm_s, x_sample, mod4, row_sample, g_att_row, w_out_b, g2,
                           nb=dec_b, t=dec_s)
    y_s = _mlp(h2_s, x1_s, mod4, row_sample, w1, w2, nb=dec_b, t=dec_s, tf=2048, sub=1024)

    states = lambda a, n: a.reshape(1, n, n_groups, n_state)
    heads = lambda a, n, s: a.reshape(1, n, s, n_heads, HEAD_DIM)
    return (y_p, y_s,
            heads(k32_p, bsz, seq), heads(v32_p, bsz, seq), logf_p[None],
            states(hre_p, bsz), states(him_p, bsz),
            heads(k32_s, dec_b, dec_s), heads(v32_s, dec_b, dec_s), logf_s[None],
            states(hre_s, dec_b), states(him_s, dec_b))
```
